```python
import math
import jax, jax.numpy as jnp
from jax import lax
import numpy as np

D_MODEL = 1024
BATCH = 4
SEQ = 8192
DEPTH = 1
DEC_BATCH = 32
DEC_SEQ = 4
PAST_LEN = 16384
PAGE_SIZE = 128

HA = 4
DA = 64
DVA = 2 * DA
WA = HA * DVA
HB = 8
DB = 64
WB = HB * DB
MIX = WA + WB
N_IN = 3 * WA + 3 * WB + HB
Q_BLOCK = 128
N_KEYS = 128
N_EXPERTS = N_KEYS * N_KEYS
PEER_HEADS = 8
PEER_TOPK = 16
D_KEY = 256
PEER_BLOCK = 128
EPS = 1e-6

kernel_name = "hymba_diff_fox_peer_adaln_step"


def rmsnorm(x, g):
    xf = x.astype(jnp.float32)
    y = xf * lax.rsqrt(jnp.mean(xf * xf, axis=-1, keepdims=True) + EPS)
    return (y * g.astype(jnp.float32)).astype(x.dtype)


def modulate(h, shift, scale):
    return h * (1 + scale) + shift


def ada_params(c, w, b):
    a = jax.nn.silu(c) @ w + b
    return jnp.split(a[:, None, :], 6, axis=-1)


def alibi_slopes(n):
    return 2.0 ** (-8.0 * jnp.arange(1, n + 1, dtype=jnp.float32) / n)


def project(h, w_in_l, b_f_l):
    b, t, _ = h.shape
    p = h @ w_in_l
    qa, ka, va, qb, kb, vb, fl = jnp.split(
        p, [WA, 2 * WA, 3 * WA, 3 * WA + WB, 3 * WA + 2 * WB, 3 * WA + 3 * WB], axis=-1)
    qa = qa.reshape(b, t, HA, 2 * DA)
    ka = ka.reshape(b, t, HA, 2 * DA)
    va = va.reshape(b, t, HA, DVA)
    qb = qb.reshape(b, t, HB, DB)
    kb = kb.reshape(b, t, HB, DB)
    vb = vb.reshape(b, t, HB, DB)
    logf = jax.nn.log_sigmoid((fl + b_f_l).astype(jnp.float32))
    return qa, ka, va, qb, kb, vb, logf


def diff_attention(q, k, v, q_pos, k_pos, lam):
    scale = DA ** -0.5
    dist = (q_pos[:, None] - k_pos[None, :]).astype(jnp.float32)
    bias = -alibi_slopes(HA)[:, None, None] * dist[None]
    causal = (dist >= 0)[None, None]

    def probs(qh, kh):
        s = jnp.einsum("bqhd,bkhd->bhqk", qh, kh).astype(jnp.float32) * scale + bias
        return jax.nn.softmax(jnp.where(causal, s, -jnp.inf), axis=-1)

    p = probs(q[..., :DA], k[..., :DA]) - lam * probs(q[..., DA:], k[..., DA:])
    return jnp.einsum("bhqk,bkhd->bqhd", p.astype(v.dtype), v)


def forgetting_attention(q, k, v, cum_q, cum_k, q_pos, k_pos):
    decay = jnp.swapaxes(cum_q, 1, 2)[..., :, None] - jnp.swapaxes(cum_k, 1, 2)[..., None, :]
    causal = (k_pos[None, :] <= q_pos[:, None])[None, None]
    s = jnp.einsum("bqhd,bkhd->bhqk", q, k).astype(jnp.float32) * (DB ** -0.5) + decay
    p = jax.nn.softmax(jnp.where(causal, s, -jnp.inf), axis=-1)
    return jnp.einsum("bhqk,bkhd->bqhd", p.astype(v.dtype), v)


def to_blocks(a):
    b, t = a.shape[:2]
    a = a.reshape((b, t // Q_BLOCK, Q_BLOCK) + a.shape[2:])
    return jnp.moveaxis(a, 1, 0)


def from_blocks(a):
    a = jnp.moveaxis(a, 0, 1)
    return a.reshape((a.shape[0], a.shape[1] * a.shape[2]) + a.shape[3:])


def sweep(fn, q_arrays, q_pos):
    blk = tuple(to_blocks(a) for a in q_arrays)
    pos_blk = q_pos.reshape(-1, Q_BLOCK)
    out = lax.map(lambda args: fn(*args[0], args[1]), (blk, pos_blk))
    return from_blocks(out)


def gather_pages(cache, page_table):
    g = cache[page_table]
    return g.reshape((page_table.shape[0], -1) + cache.shape[2:])


def merge(oa, ob, lam_init, subln_g_l, w_out_l):
    b, t = oa.shape[:2]
    oa = rmsnorm(oa, subln_g_l) * (1 - lam_init)
    o = jnp.concatenate([oa.reshape(b, t, WA), ob.reshape(b, t, WB)], axis=-1)
    return o @ w_out_l


def peer_block(h, wq, subkeys, u, v):
    n = h.shape[0]
    q = (h @ wq).reshape(n, PEER_HEADS, 2, D_KEY // 2)
    s = jnp.einsum("nhpd,pkd->nhpk", q, subkeys).astype(jnp.float32)
    top_s, top_i = lax.top_k(s, PEER_TOPK)
    n_cand = PEER_TOPK * PEER_TOPK
    cand_s = (top_s[:, :, 0, :, None] + top_s[:, :, 1, None, :]).reshape(n, PEER_HEADS, n_cand)
    cand_i = (top_i[:, :, 0, :, None] * N_KEYS + top_i[:, :, 1, None, :]).reshape(n, PEER_HEADS, n_cand)
    best_s, best_j = lax.top_k(cand_s, PEER_TOPK)
    idx = jnp.take_along_axis(cand_i, best_j, axis=-1)
    gate = jax.nn.softmax(best_s, axis=-1)
    act = jax.nn.gelu(jnp.einsum("nd,nhkd->nhk", h, u[idx]).astype(jnp.float32), approximate=False)
    return jnp.einsum("nhk,nhkd->nd", (gate * act).astype(h.dtype), v[idx])


def peer(h, wq, subkeys, u, v):
    b, t, d = h.shape
    n = b * t
    n_blk = -(-n // PEER_BLOCK)
    flat = jnp.pad(h.reshape(n, d), ((0, n_blk * PEER_BLOCK - n), (0, 0)))
    out = lax.map(lambda hb: peer_block(hb, wq, subkeys, u, v), flat.reshape(n_blk, PEER_BLOCK, d))
    return out.reshape(n_blk * PEER_BLOCK, d)[:n].reshape(b, t, d)


def setup_inputs(seed: int = 0) -> dict:
    key = jax.random.key(seed)
    ks = jax.random.split(key, 27)
    f32 = jnp.float32
    n_pages = PAST_LEN // PAGE_SIZE
    n_pool = (DEC_BATCH * n_pages * 5) // 4

    def nrm(k, shape, s):
        return jax.random.normal(k, shape, f32) * s

    page_table = jax.random.permutation(ks[9], n_pool)[: DEC_BATCH * n_pages]
    page_table = page_table.reshape(DEC_BATCH, n_pages).astype(jnp.int32)
    return {
        "x_prompt": nrm(ks[0], (BATCH, SEQ, D_MODEL), 1.0),
        "x_sample": nrm(ks[1], (DEC_BATCH, DEC_SEQ, D_MODEL), 1.0),
        "c_prompt": nrm(ks[2], (BATCH, D_MODEL), 1.0),
        "c_sample": nrm(ks[3], (DEC_BATCH, D_MODEL), 1.0),
        "cache_a_k": nrm(ks[4], (DEPTH, n_pool, PAGE_SIZE, HA, 2 * DA), 1.0),
        "cache_a_v": nrm(ks[5], (DEPTH, n_pool, PAGE_SIZE, HA, DVA), 1.0),
        "cache_b_k": nrm(ks[6], (DEPTH, n_pool, PAGE_SIZE, HB, DB), 1.0),
        "cache_b_v": nrm(ks[7], (DEPTH, n_pool, PAGE_SIZE, HB, DB), 1.0),
        "cache_b_logf": jax.nn.log_sigmoid(nrm(ks[8], (DEPTH, n_pool, PAGE_SIZE, HB), 1.0) + 2.5),
        "page_table": page_table,
        "w_ada": nrm(ks[10], (DEPTH, D_MODEL, 6 * D_MODEL), D_MODEL ** -0.5),
        "b_ada": nrm(ks[11], (DEPTH, 6 * D_MODEL), 0.01),
        "norm1_g": 1.0 + nrm(ks[12], (DEPTH, D_MODEL), 0.02),
        "w_in": nrm(ks[13], (DEPTH, D_MODEL, N_IN), D_MODEL ** -0.5),
        "b_f": jax.random.uniform(ks[14], (DEPTH, HB), f32, 1.0, 4.0),
        "lambda_q1": nrm(ks[15], (DEPTH, DA), 0.1),
        "lambda_k1": nrm(ks[16], (DEPTH, DA), 0.1),
        "lambda_q2": nrm(ks[17], (DEPTH, DA), 0.1),
        "lambda_k2": nrm(ks[18], (DEPTH, DA), 0.1),
        "subln_g": 1.0 + nrm(ks[19], (DEPTH, DVA), 0.02),
        "w_out": nrm(ks[20], (DEPTH, MIX, D_MODEL), MIX ** -0.5),
        "norm2_g": 1.0 + nrm(ks[21], (DEPTH, D_MODEL), 0.02),
        "peer_wq": nrm(ks[22], (DEPTH, D_MODEL, PEER_HEADS * D_KEY), D_MODEL ** -0.5),
        "peer_subkeys": nrm(ks[23], (DEPTH, 2, N_KEYS, D_KEY // 2), (D_KEY // 2) ** -0.5),
        "peer_u": nrm(ks[24], (DEPTH, N_EXPERTS, D_MODEL), D_MODEL ** -0.5),
        "peer_v": nrm(ks[25], (DEPTH, N_EXPERTS, D_MODEL), PEER_HEADS ** -0.5),
        "final_g": 1.0 + nrm(ks[26], (D_MODEL,), 0.02),
    }


def reference(x_prompt, x_sample, c_prompt, c_sample, cache_a_k, cache_a_v, cache_b_k, cache_b_v,
              cache_b_logf, page_table, w_ada, b_ada, norm1_g, w_in, b_f, lambda_q1, lambda_k1,
              lambda_q2, lambda_k2, subln_g, w_out, norm2_g, peer_wq, peer_subkeys, peer_u, peer_v,
              final_g):
    t_prompt = x_prompt.shape[1]
    t_new = x_sample.shape[1]
    past_len = page_table.shape[1] * cache_a_k.shape[2]
    pos_p = jnp.arange(t_prompt, dtype=jnp.int32)
    pos_s = past_len + jnp.arange(t_new, dtype=jnp.int32)
    pos_all = jnp.arange(past_len + t_new, dtype=jnp.int32)
    xp, xs = x_prompt, x_sample
    akp, avp, bkp, bvp, bfp = [], [], [], [], []
    aks, avs, bks, bvs, bfs = [], [], [], [], []
    for l in range(DEPTH):
        lam_init = 0.8 - 0.6 * math.exp(-0.3 * l)
        lam = (jnp.exp(jnp.sum(lambda_q1[l] * lambda_k1[l]).astype(jnp.float32))
               - jnp.exp(jnp.sum(lambda_q2[l] * lambda_k2[l]).astype(jnp.float32)) + lam_init)

        sh1, sc1, g1, sh2, sc2, g2 = ada_params(c_prompt, w_ada[l], b_ada[l])
        h = modulate(rmsnorm(xp, norm1_g[l]), sh1, sc1)
        qa, ka, va, qb, kb, vb, logf = project(h, w_in[l], b_f[l])
        cum = jnp.cumsum(logf, axis=1)
        oa = sweep(lambda q, qp: diff_attention(q, ka, va, qp, pos_p, lam), (qa,), pos_p)
        ob = sweep(lambda q, cq, qp: forgetting_attention(q, kb, vb, cq, cum, qp, pos_p), (qb, cum), pos_p)
        xp = xp + g1 * merge(oa, ob, lam_init, subln_g[l], w_out[l])
        h2 = modulate(rmsnorm(xp, norm2_g[l]), sh2, sc2)
        xp = xp + g2 * peer(h2, peer_wq[l], peer_subkeys[l], peer_u[l], peer_v[l])
        akp.append(ka); avp.append(va); bkp.append(kb); bvp.append(vb); bfp.append(logf)

        sh1, sc1, g1, sh2, sc2, g2 = ada_params(c_sample, w_ada[l], b_ada[l])
        h = modulate(rmsnorm(xs, norm1_g[l]), sh1, sc1)
        qa_s, ka_s, va_s, qb_s, kb_s, vb_s, logf_s = project(h, w_in[l], b_f[l])
        ka_all = jnp.concatenate([gather_pages(cache_a_k[l], page_table), ka_s], axis=1)
        va_all = jnp.concatenate([gather_pages(cache_a_v[l], page_table), va_s], axis=1)
        kb_all = jnp.concatenate([gather_pages(cache_b_k[l], page_table), kb_s], axis=1)
        vb_all = jnp.concatenate([gather_pages(cache_b_v[l], page_table), vb_s], axis=1)
        logf_all = jnp.concatenate(
            [gather_pages(cache_b_logf[l], page_table).astype(jnp.float32), logf_s], axis=1)
        cum_all = jnp.cumsum(logf_all, axis=1)
        oa = diff_attention(qa_s, ka_all, va_all, pos_s, pos_all, lam)
        ob = forgetting_attention(qb_s, kb_all, vb_all, cum_all[:, past_len:], cum_all, pos_s, pos_all)
        xs = xs + g1 * merge(oa, ob, lam_init, subln_g[l], w_out[l])
        h2 = modulate(rmsnorm(xs, norm2_g[l]), sh2, sc2)
        xs = xs + g2 * peer(h2, peer_wq[l], peer_subkeys[l], peer_u[l], peer_v[l])
        aks.append(ka_s); avs.append(va_s); bks.append(kb_s); bvs.append(vb_s); bfs.append(logf_s)

    y_prompt = rmsnorm(xp, final_g)
    y_sample = rmsnorm(xs, final_g)
    return (y_prompt, y_sample,
            jnp.stack(akp), jnp.stack(avp), jnp.stack(bkp), jnp.stack(bvp), jnp.stack(bfp),
            jnp.stack(aks), jnp.stack(avs), jnp.stack(bks), jnp.stack(bvs), jnp.stack(bfs))
```

```python
import functools
import math

import jax
import jax.numpy as jnp
import numpy as np
from jax import lax
from jax.experimental import pallas as pl
from jax.experimental.pallas import tpu as pltpu

F32, BF16, I32 = jnp.float32, jnp.bfloat16, jnp.int32
EPS = 1e-6
D_MODEL = 1024
HA, DA, DVA = 4, 64, 128
HB, DB = 8, 64
WA, WB = HA * DVA, HB * DB
N_KEYS = 128
PEER_HEADS = 8
PEER_TOPK = 16
N_SLOTS = PEER_HEADS * PEER_TOPK
N_EXPERTS = N_KEYS * N_KEYS
HALF_EXPERTS = N_EXPERTS // 2
LANES = 128
SUBLANES = 8
NEG_INF = float("-inf")
VMEM_LIMIT = 56 * 1024 * 1024
NT_DIMS = (((1,), (1,)), ((), ()))


def _cparams(sem):
    return pltpu.CompilerParams(dimension_semantics=sem, vmem_limit_bytes=VMEM_LIMIT)


def _split3(x):
    hi = x.astype(BF16)
    r = x - hi.astype(F32)
    mid = r.astype(BF16)
    lo = (r - mid.astype(F32)).astype(BF16)
    return hi, mid, lo


def _ada_kernel(c_ref, w_ref, b_ref, o_ref):
    c = c_ref[...]
    s = c * (1.0 / (1.0 + jnp.exp(-c)))
    o_ref[...] = jnp.dot(s.astype(BF16), w_ref[...].astype(BF16),
                         preferred_element_type=F32) + b_ref[...]


def _ada(c_all, w, b):
    r = c_all.shape[0]
    n_out = w.shape[1]
    return pl.pallas_call(
        _ada_kernel,
        grid=(n_out // D_MODEL,),
        in_specs=[pl.BlockSpec((r, D_MODEL), lambda j: (0, 0)),
                  pl.BlockSpec((D_MODEL, D_MODEL), lambda j: (0, j)),
                  pl.BlockSpec((1, D_MODEL), lambda j: (0, j))],
        out_specs=pl.BlockSpec((r, D_MODEL), lambda j: (0, j)),
        out_shape=jax.ShapeDtypeStruct((r, n_out), F32),
        compiler_params=_cparams(("parallel",)),
        name="ada",
    )(c_all, w, b.reshape(1, n_out))


def _lam_kernel(v_ref, o_ref, *, lam_init):
    v = v_ref[...]
    a = jnp.sum(v[0:1] * v[1:2], axis=1, keepdims=True)
    b = jnp.sum(v[2:3] * v[3:4], axis=1, keepdims=True)
    lam = jnp.exp(a) - jnp.exp(b) + lam_init
    o_ref[...] = jnp.broadcast_to(lam, o_ref.shape)


def _lam(lq1, lk1, lq2, lk2, lam_init):
    v = jnp.zeros((SUBLANES, LANES), F32)
    v = v.at[0:4, 0:DA].set(jnp.stack([lq1, lk1, lq2, lk2]))
    return pl.pallas_call(
        functools.partial(_lam_kernel, lam_init=lam_init),
        out_shape=jax.ShapeDtypeStruct((SUBLANES, LANES), F32),
        name="lam",
    )(v)


def _mod_spec(per_row, tm, rows_per_mod):
    if per_row:
        return pl.BlockSpec((tm, D_MODEL), lambda i: (i, 0))
    blocks = rows_per_mod // tm
    return pl.BlockSpec((None, 1, D_MODEL), lambda i: (i // blocks, 0, 0))


def _proj_kernel(x_ref, sh_ref, sc_ref, g_ref, w_ref, wf_ref, bf_ref, tri_ref, pq_ref, pk_ref,
                 qc_ref, kc_ref,
                 ka_ref, va_ref, kb_ref, vb_ref, lf_ref,
                 qa16_ref, ka16_ref, va16_ref, qb16_ref, kb16_ref, vb16_ref, qbias_ref, kbias_ref,
                 carry_ref, *, blocks_per_seq):
    i = pl.program_id(0)

    @pl.when(i % blocks_per_seq == 0)
    def _():
        carry_ref[...] = jnp.zeros_like(carry_ref)

    x = x_ref[...]
    ms = jnp.mean(x * x, axis=-1, keepdims=True)
    h = x * lax.rsqrt(ms + EPS) * g_ref[...]
    h = h * (1.0 + sc_ref[...]) + sh_ref[...]
    hb = h.astype(BF16)
    p = jnp.dot(hb, w_ref[...], preferred_element_type=F32)
    qa, ka, va = p[:, 0:WA], p[:, WA:2 * WA], p[:, 2 * WA:3 * WA]
    o = 3 * WA
    qb, kb, vb = p[:, o:o + WB], p[:, o + WB:o + 2 * WB], p[:, o + 2 * WB:o + 3 * WB]
    ka_ref[...] = ka
    va_ref[...] = va
    kb_ref[...] = kb
    vb_ref[...] = vb
    qa16_ref[...] = (qa * (DA ** -0.5)).astype(BF16)
    ka16_ref[...] = ka.astype(BF16)
    va16_ref[...] = va.astype(BF16)
    qb16_ref[...] = (qb * (DB ** -0.5)).astype(BF16)
    kb16_ref[...] = kb.astype(BF16)
    vb16_ref[...] = vb.astype(BF16)

    z = jnp.dot(hb, wf_ref[...], preferred_element_type=F32) + bf_ref[...]
    lf = jnp.minimum(z, 0.0) - jnp.log1p(jnp.exp(-jnp.abs(z)))
    lane = lax.broadcasted_iota(I32, lf.shape, 1)
    lf = jnp.where(lane < HB, lf, 0.0)
    lf_ref[...] = lf[:, 0:HB]

    tri = tri_ref[...]
    hi, mid, lo = _split3(lf)
    cum = (jnp.dot(tri, hi, preferred_element_type=F32)
           + jnp.dot(tri, mid, preferred_element_type=F32)
           + jnp.dot(tri, lo, preferred_element_type=F32)) + carry_ref[...]
    tm = cum.shape[0]
    carry_ref[...] = cum[tm - 1:tm, :]
    ccat = jnp.concatenate(_split3(cum), axis=1)
    qbias_ref[...] = (jnp.dot(ccat, pq_ref[...], preferred_element_type=F32) + qc_ref[...]).astype(BF16)
    kbias_ref[...] = (jnp.dot(ccat, pk_ref[...], preferred_element_type=F32) + kc_ref[...]).astype(BF16)


def _decay_placement():
    pq = np.zeros((3 * LANES, WB), np.float32)
    pk = np.zeros((3 * LANES, WB), np.float32)
    qc = np.zeros((1, WB), np.float32)
    kc = np.zeros((1, WB), np.float32)
    for h in range(HB):
        pair, which = divmod(h, 2)
        base = pair * LANES + which * 6
        for t in range(3):
            pq[t * LANES + h, base + t] = 1.0
            qc[0, base + 3 + t] = 1.0
            kc[0, base + t] = 1.0
            pk[t * LANES + h, base + 3 + t] = -1.0
    return (jnp.asarray(pq, BF16), jnp.asarray(pk, BF16), jnp.asarray(qc), jnp.asarray(kc))


def _proj(x2d, sh, sc, g, w_main, w_f, b_f, per_row, rows_per_mod, seq_len, tm):
    n = x2d.shape[0]
    nw = w_main.shape[1]
    tri = (np.arange(tm)[None, :] <= np.arange(tm)[:, None]).astype(np.float32)
    pq, pk, qc, kc = _decay_placement()
    mod = _mod_spec(per_row, tm, rows_per_mod)
    row = lambda w: pl.BlockSpec((tm, w), lambda i: (i, 0))
    const = lambda a: pl.BlockSpec(a.shape, lambda i: (0,) * a.ndim)
    tri = jnp.asarray(tri, BF16)
    outs = ([jax.ShapeDtypeStruct((n, WA), F32)] * 4 + [jax.ShapeDtypeStruct((n, HB), F32)]
            + [jax.ShapeDtypeStruct((n, WA), BF16)] * 8)
    return pl.pallas_call(
        functools.partial(_proj_kernel, blocks_per_seq=seq_len // tm),
        grid=(n // tm,),
        in_specs=[row(D_MODEL), mod, mod, const(g), const(w_main), const(w_f), const(b_f),
                  const(tri), const(pq), const(pk), const(qc), const(kc)],
        out_specs=[row(WA)] * 4 + [row(HB)] + [row(WA)] * 8,
        out_shape=outs,
        scratch_shapes=[pltpu.VMEM((1, LANES), F32)],
        compiler_params=_cparams(("arbitrary",)),
        name="proj",
    )(x2d, sh, sc, g, w_main, w_f, b_f, tri, pq, pk, qc, kc)


def _attn_kernel(q_ref, qb_ref, k_ref, kb_ref, v_ref, lam_ref, o_ref, m_ref, l_ref, acc_ref,
                 *, tq, fox):
    qi = pl.program_id(2)
    q = q_ref[...].astype(F32)
    qb = qb_ref[...].astype(F32)
    lane = lax.broadcasted_iota(I32, (tq, LANES), 1)
    q_lo = jnp.where(lane < DA, q, 0.0)
    q_hi = jnp.where(lane >= DA, q, 0.0)
    if fox:
        qb_lo = jnp.where(lane < 6, qb, 0.0)
        qb_hi = jnp.where((lane >= 6) & (lane < 12), qb, 0.0)
    else:
        qb_lo = qb_hi = qb
    qq = jnp.concatenate([jnp.concatenate([q_lo, qb_lo], axis=1),
                          jnp.concatenate([q_hi, qb_hi], axis=1)], axis=0).astype(BF16)
    m_ref[...] = jnp.full(m_ref.shape, NEG_INF, F32)
    l_ref[...] = jnp.zeros(l_ref.shape, F32)
    acc_ref[...] = jnp.zeros(acc_ref.shape, F32)

    def step(j, masked):
        off = pl.multiple_of(j * tq, tq)
        kk = jnp.concatenate([k_ref[pl.ds(off, tq), :], kb_ref[pl.ds(off, tq), :]], axis=1)
        s = lax.dot_general(qq, kk, NT_DIMS, preferred_element_type=F32)
        if masked:
            r = lax.broadcasted_iota(I32, s.shape, 0)
            c = lax.broadcasted_iota(I32, s.shape, 1)
            r = jnp.where(r >= tq, r - tq, r)
            s = jnp.where(c <= r, s, NEG_INF)
        m_prev = m_ref[...]
        m_new = jnp.maximum(m_prev, jnp.max(s, axis=1, keepdims=True))
        alpha = jnp.exp(m_prev - m_new)
        p = jnp.exp(s - m_new)
        l_ref[...] = alpha * l_ref[...] + jnp.sum(p, axis=1, keepdims=True)
        acc_ref[...] = alpha * acc_ref[...] + jnp.dot(p.astype(BF16), v_ref[pl.ds(off, tq), :],
                                                      preferred_element_type=F32)
        m_ref[...] = m_new

    def body(j, carry):
        step(j, False)
        return carry

    lax.fori_loop(0, qi, body, 0)
    step(qi, True)
    o = acc_ref[...] / l_ref[...]
    if fox:
        o_ref[...] = jnp.where(lane < DB, o[0:tq], o[tq:2 * tq])
    else:
        o_ref[...] = o[0:tq] - lam_ref[0:1, :] * o[tq:2 * tq]


def _attention(q16, qbias, k16, kbias, v16, lam, batch, seq_len, tq, fox):
    n = q16.shape[0]
    groups = q16.shape[1] // LANES
    nq = seq_len // tq
    q_spec = pl.BlockSpec((tq, LANES), lambda b, g, i: (b * nq + i, g))
    kv_spec = pl.BlockSpec((seq_len, LANES), lambda b, g, i: (b, g))
    if fox:
        qb_spec, kb_spec = q_spec, kv_spec
    else:
        qb_spec = pl.BlockSpec((None, tq, LANES), lambda b, g, i: (g, i, 0))
        kb_spec = pl.BlockSpec((None, seq_len, LANES), lambda b, g, i: (g, 0, 0))
    return pl.pallas_call(
        functools.partial(_attn_kernel, tq=tq, fox=fox),
        grid=(batch, groups, nq),
        in_specs=[q_spec, qb_spec, kv_spec, kb_spec, kv_spec,
                  pl.BlockSpec((SUBLANES, LANES), lambda b, g, i: (0, 0))],
        out_specs=q_spec,
        out_shape=jax.ShapeDtypeStruct((n, groups * LANES), F32),
        scratch_shapes=[pltpu.VMEM((2 * tq, 1), F32), pltpu.VMEM((2 * tq, 1), F32),
                        pltpu.VMEM((2 * tq, LANES), F32)],
        compiler_params=_cparams(("parallel", "parallel", "arbitrary")),
        name="attn_fox" if fox else "attn_diff",
    )(q16, qbias, k16, kbias, v16, lam)


def _alibi_bias(seq_len):
    slopes = 2.0 ** (-8.0 * np.arange(1, HA + 1, dtype=np.float64) / HA)
    pos = np.arange(seq_len)
    hi, lo = (pos // LANES) * float(LANES), (pos % LANES).astype(np.float64)
    qb = np.zeros((HA, seq_len, LANES), np.float32)
    kb = np.zeros((HA, seq_len, LANES), np.float32)
    for h in range(HA):
        qb[h, :, 0], qb[h, :, 1], qb[h, :, 2], qb[h, :, 3] = -slopes[h] * hi, -slopes[h] * lo, 1.0, 1.0
        kb[h, :, 0], kb[h, :, 1], kb[h, :, 2], kb[h, :, 3] = 1.0, 1.0, slopes[h] * hi, slopes[h] * lo
    return jnp.asarray(qb, BF16), jnp.asarray(kb, BF16)


def _decode_kernel(pt_ref, *refs, pages_per_step, n_steps, page):
    del pt_ref
    pp = pages_per_step
    paged = [refs[c * pp:(c + 1) * pp] for c in range(5)]
    (nka_ref, nva_ref, nkb_ref, nvb_ref, nlf_ref, wqa_ref, wqb_ref, biasa_ref, slope_ref,
     maska_ref, maskb_ref, triu_ref, dmaska_ref, dmaskb_ref, lam_ref) = refs[5 * pp:5 * pp + 15]
    oa_ref, ob_ref = refs[5 * pp + 15:5 * pp + 17]
    (ma_ref, la_ref, acca_ref, mb_ref, lb_ref, accb_ref, carry_ref, pad_ref) = refs[5 * pp + 17:]
    j = pl.program_id(1)

    @pl.when(j == 0)
    def _():
        ma_ref[...] = jnp.full(ma_ref.shape, NEG_INF, F32)
        mb_ref[...] = jnp.full(mb_ref.shape, NEG_INF, F32)
        la_ref[...] = jnp.zeros(la_ref.shape, F32)
        lb_ref[...] = jnp.zeros(lb_ref.shape, F32)
        acca_ref[...] = jnp.zeros(acca_ref.shape, F32)
        accb_ref[...] = jnp.zeros(accb_ref.shape, F32)
        carry_ref[...] = jnp.zeros(carry_ref.shape, F32)
        pad_ref[...] = jnp.zeros(pad_ref.shape, F32)

    wqa = wqa_ref[...]
    wqb = wqb_ref[...]

    def flash(s, v16, m_ref, l_ref, acc_ref):
        m_prev = m_ref[...]
        m_new = jnp.maximum(m_prev, jnp.max(s, axis=1, keepdims=True))
        alpha = jnp.exp(m_prev - m_new)
        p = jnp.exp(s - m_new)
        l_ref[...] = alpha * l_ref[...] + jnp.sum(p, axis=1, keepdims=True)
        acc_ref[...] = alpha * acc_ref[...] + jnp.dot(p.astype(BF16), v16, preferred_element_type=F32)
        m_ref[...] = m_new

    def process(ka_r, va_r, kb_r, vb_r, lf_r, page_idx, masked):
        sa = lax.dot_general(wqa, ka_r[...].astype(BF16), NT_DIMS, preferred_element_type=F32)
        sa = sa + biasa_ref[...] + slope_ref[...] * (page_idx * float(page))
        if masked:
            sa = jnp.where(maska_ref[...] > 0.0, sa, NEG_INF)
        flash(sa, va_r[...].astype(BF16), ma_ref, la_ref, acca_ref)
        pad_ref[:, 0:HB] = lf_r[...]
        lft = pad_ref[...].T[0:HB, :]
        triu = triu_ref[...]
        hi, mid, lo = _split3(lft)
        cum = (jnp.dot(hi, triu, preferred_element_type=F32)
               + jnp.dot(mid, triu, preferred_element_type=F32)
               + jnp.dot(lo, triu, preferred_element_type=F32)) + carry_ref[...]
        carry_ref[...] = jnp.broadcast_to(cum[:, page - 1:page], carry_ref.shape)
        sb = lax.dot_general(wqb, kb_r[...].astype(BF16), NT_DIMS, preferred_element_type=F32)
        sb = sb - jnp.concatenate([cum] * 4, axis=0)
        if masked:
            sb = jnp.where(maskb_ref[...] > 0.0, sb, NEG_INF)
        flash(sb, vb_r[...].astype(BF16), mb_ref, lb_ref, accb_ref)

    for k in range(pp):
        process(paged[0][k], paged[1][k], paged[2][k], paged[3][k], paged[4][k],
                (j * pp + k).astype(F32), False)

    @pl.when(j == n_steps - 1)
    def _():
        process(nka_ref, nva_ref, nkb_ref, nvb_ref, nlf_ref, float(n_steps * pp), True)
        oa = acca_ref[...] / la_ref[...]
        oa = (oa[0:16] - lam_ref[0:1, 0:1] * oa[16:32]) * dmaska_ref[...]
        oa_ref[...] = oa[:, 0:128] + oa[:, 128:256] + oa[:, 256:384] + oa[:, 384:512]
        ob = (accb_ref[...] / lb_ref[...]) * dmaskb_ref[...]
        ob_ref[...] = ob[:, 0:128] + ob[:, 128:256] + ob[:, 256:384] + ob[:, 384:512]


def _decode_attention(page_table, caches, new_pages, wqa, wqb, lam, new_tokens, pages_per_step):
    dec_batch, n_pages = page_table.shape
    page = caches[0].shape[1]
    pp = pages_per_step
    n_steps = n_pages // pp
    past = n_pages * page
    rows_a = 2 * HA * new_tokens
    rows_b = new_tokens * HB
    slopes = 2.0 ** (-8.0 * np.arange(1, HA + 1, dtype=np.float64) / HA)
    ra = np.arange(rows_a)
    ha, qa_i = (ra // new_tokens) % HA, ra % new_tokens
    col = np.arange(page)
    biasa = slopes[ha][:, None] * (col[None, :] - past - qa_i[:, None])
    slope_rep = np.repeat(slopes[ha][:, None], page, axis=1)
    maska = (col[None, :] <= qa_i[:, None]).astype(np.float32)
    rb = np.arange(rows_b)
    maskb = (col[None, :] <= (rb // HB)[:, None]).astype(np.float32)
    triu = (np.arange(page)[:, None] <= np.arange(page)[None, :]).astype(np.float32)
    ca = np.arange(WA)
    dmaska = (ca[None, :] // DVA == (np.arange(HA * new_tokens) // new_tokens)[:, None]).astype(np.float32)
    cb = np.arange(WB)
    dmaskb = (cb[None, :] // DB == (rb % HB)[:, None]).astype(np.float32)
    consts = [jnp.asarray(biasa, F32), jnp.asarray(slope_rep, F32), jnp.asarray(maska),
              jnp.asarray(maskb), jnp.asarray(triu, BF16), jnp.asarray(dmaska), jnp.asarray(dmaskb), lam]

    def page_spec(width, k):
        return pl.BlockSpec((None, page, width),
                            lambda b, j, pt: (pt[b * n_pages + j * pp + k], 0, 0))

    widths = [c.shape[2] for c in caches]
    in_specs, args = [], []
    for c, w in zip(caches, widths):
        for k in range(pp):
            in_specs.append(page_spec(w, k))
            args.append(c)
    for a in new_pages + [wqa, wqb]:
        in_specs.append(pl.BlockSpec((None,) + a.shape[1:], lambda b, j, pt: (b, 0, 0)))
        args.append(a)
    for a in consts:
        in_specs.append(pl.BlockSpec(a.shape, lambda b, j, pt: (0, 0)))
        args.append(a)
    grid_spec = pltpu.PrefetchScalarGridSpec(
        num_scalar_prefetch=1,
        grid=(dec_batch, n_steps),
        in_specs=in_specs,
        out_specs=[pl.BlockSpec((None, rows_a // 2, LANES), lambda b, j, pt: (b, 0, 0)),
                   pl.BlockSpec((None, rows_b, LANES), lambda b, j, pt: (b, 0, 0))],
        scratch_shapes=[pltpu.VMEM((rows_a, 1), F32), pltpu.VMEM((rows_a, 1), F32),
                        pltpu.VMEM((rows_a, WA), F32),
                        pltpu.VMEM((rows_b, 1), F32), pltpu.VMEM((rows_b, 1), F32),
                        pltpu.VMEM((rows_b, WB), F32),
                        pltpu.VMEM((HB, page), F32), pltpu.VMEM((page, LANES), F32)],
    )
    return pl.pallas_call(
        functools.partial(_decode_kernel, pages_per_step=pp, n_steps=n_steps, page=page),
        grid_spec=grid_spec,
        out_shape=[jax.ShapeDtypeStruct((dec_batch, rows_a // 2, LANES), F32),
                   jax.ShapeDtypeStruct((dec_batch, rows_b, LANES), F32)],
        compiler_params=_cparams(("parallel", "arbitrary")),
        name="decode_attn",
    )(page_table.reshape(-1), *args)


def _merge_kernel(oa_ref, ob_ref, x_ref, g1_ref, sh2_ref, sc2_ref, sub_ref, wo_ref, n2_ref,
                  xp_ref, h2_ref, *, out_scale):
    oa = oa_ref[...]
    parts = []
    for h in range(HA):
        o = oa[:, h * DVA:(h + 1) * DVA]
        ms = jnp.mean(o * o, axis=-1, keepdims=True)
        parts.append(o * lax.rsqrt(ms + EPS) * sub_ref[...] * out_scale)
    o = jnp.concatenate(parts + [ob_ref[...]], axis=1).astype(BF16)
    y = jnp.dot(o, wo_ref[...], preferred_element_type=F32)
    xp = x_ref[...] + g1_ref[...] * y
    xp_ref[...] = xp
    ms = jnp.mean(xp * xp, axis=-1, keepdims=True)
    h2 = xp * lax.rsqrt(ms + EPS) * n2_ref[...]
    h2_ref[...] = h2 * (1.0 + sc2_ref[...]) + sh2_ref[...]


def _merge(oa, ob, x2d, g1, sh2, sc2, sub_g, w_out16, n2_g, per_row, rows_per_mod, tm, out_scale):
    n = x2d.shape[0]
    mod = _mod_spec(per_row, tm, rows_per_mod)
    row = lambda w: pl.BlockSpec((tm, w), lambda i: (i, 0))
    const = lambda a: pl.BlockSpec(a.shape, lambda i: (0,) * a.ndim)
    return pl.pallas_call(
        functools.partial(_merge_kernel, out_scale=out_scale),
        grid=(n // tm,),
        in_specs=[row(WA), row(WB), row(D_MODEL), mod, mod, mod, const(sub_g), const(w_out16),
                  const(n2_g)],
        out_specs=[row(D_MODEL), row(D_MODEL)],
        out_shape=[jax.ShapeDtypeStruct((n, D_MODEL), F32)] * 2,
        compiler_params=_cparams(("parallel",)),
        name="merge",
    )(oa, ob, x2d, g1, sh2, sc2, sub_g, w_out16, n2_g)


def _topk_kernel(h_ref, wq_ref, sk_ref, idx_ref, gate_ref, s_ref, ts_ref, ti_ref, be_ref, bg_ref, *, tb):
    hb = h_ref[...].astype(BF16)
    qt = lax.dot_general(wq_ref[...], hb, NT_DIMS, preferred_element_type=F32)
    for hp in range(2 * PEER_HEADS):
        sub = sk_ref[hp % 2]
        s_ref[hp] = jnp.dot(sub, qt[hp * N_KEYS:(hp + 1) * N_KEYS, :].astype(BF16),
                            preferred_element_type=F32)
    row_k = lax.broadcasted_iota(I32, (N_KEYS, tb), 0).astype(F32)

    def stage1(hp, carry):
        s = s_ref[hp]
        for r in range(PEER_TOPK):
            m = jnp.max(s, axis=0, keepdims=True)
            pick = jnp.min(jnp.where(s == m, row_k, float(N_KEYS)), axis=0, keepdims=True)
            ts_ref[hp, r:r + 1, :] = m
            ti_ref[hp, r:r + 1, :] = pick
            s = jnp.where(row_k == pick, NEG_INF, s)
        return carry

    lax.fori_loop(0, 2 * PEER_HEADS, stage1, 0)
    n_cand = PEER_TOPK * PEER_TOPK
    row_c = lax.broadcasted_iota(I32, (n_cand, tb), 0).astype(F32)

    def stage2(h, carry):
        a, b = ts_ref[2 * h], ts_ref[2 * h + 1]
        ia, ib = ti_ref[2 * h], ti_ref[2 * h + 1]
        cs = jnp.concatenate([a[i:i + 1, :] + b for i in range(PEER_TOPK)], axis=0)
        ce = jnp.concatenate([ia[i:i + 1, :] * float(N_KEYS) + ib for i in range(PEER_TOPK)], axis=0)
        best, experts = [], []
        for r in range(PEER_TOPK):
            m = jnp.max(cs, axis=0, keepdims=True)
            pick = jnp.min(jnp.where(cs == m, row_c, float(n_cand)), axis=0, keepdims=True)
            sel = row_c == pick
            experts.append(jnp.max(jnp.where(sel, ce, -1.0), axis=0, keepdims=True))
            best.append(m)
            cs = jnp.where(sel, NEG_INF, cs)
        bs = jnp.concatenate(best, axis=0)
        ex = jnp.exp(bs - bs[0:1, :])
        off = pl.multiple_of(h * PEER_TOPK, PEER_TOPK)
        bg_ref[pl.ds(off, PEER_TOPK), :] = ex / jnp.sum(ex, axis=0, keepdims=True)
        be_ref[pl.ds(off, PEER_TOPK), :] = jnp.concatenate(experts, axis=0)
        return carry

    lax.fori_loop(0, PEER_HEADS, stage2, 0)
    idx_ref[...] = be_ref[...].T.astype(I32)
    gate_ref[...] = bg_ref[...].T


def _topk(h2, wq_t16, subkeys16, tb):
    n = h2.shape[0]
    const = lambda a: pl.BlockSpec(a.shape, lambda i: (0,) * a.ndim)
    return pl.pallas_call(
        functools.partial(_topk_kernel, tb=tb),
        grid=(n // tb,),
        in_specs=[pl.BlockSpec((tb, D_MODEL), lambda i: (i, 0)), const(wq_t16), const(subkeys16)],
        out_specs=[pl.BlockSpec((tb, N_SLOTS), lambda i: (i, 0))] * 2,
        out_shape=[jax.ShapeDtypeStruct((n, N_SLOTS), I32), jax.ShapeDtypeStruct((n, N_SLOTS), F32)],
        scratch_shapes=[pltpu.VMEM((2 * PEER_HEADS, N_KEYS, tb), F32),
                        pltpu.VMEM((2 * PEER_HEADS, PEER_TOPK, tb), F32),
                        pltpu.VMEM((2 * PEER_HEADS, PEER_TOPK, tb), F32),
                        pltpu.VMEM((N_SLOTS, tb), F32), pltpu.VMEM((N_SLOTS, tb), F32)],
        compiler_params=_cparams(("parallel",)),
        name="peer_topk",
    )(h2, wq_t16, subkeys16)


def _pack_table(t):
    bits = lax.bitcast_convert_type(t.astype(BF16), jnp.uint16).astype(jnp.uint32)
    word = bits[:HALF_EXPERTS] | (bits[HALF_EXPERTS:] << 16)
    return lax.bitcast_convert_type(word, I32).reshape(HALF_EXPERTS, SUBLANES, LANES)


def _gather_row(tbl_ref, idx):
    word = tbl_ref[idx & (HALF_EXPERTS - 1)]
    shift = 16 - ((idx >> 13) << 4)
    return lax.bitcast_convert_type((word << shift) & jnp.int32(-65536), F32)


def _peer_u_kernel(idx_ref, h_ref, gate_ref, tbl_ref, w_ref, prod_ref, *, tb):
    ones = jnp.ones((SUBLANES, LANES), F32)

    def token(t, carry):
        h = h_ref[t]
        for e in range(N_SLOTS):
            prod_ref[e * SUBLANES:(e + 1) * SUBLANES, :] = _gather_row(tbl_ref, idx_ref[t, e]) * h
        rows = []
        for g in range(N_SLOTS // SUBLANES):
            base = g * SUBLANES * SUBLANES
            acc = prod_ref[pl.ds(base, SUBLANES, stride=SUBLANES), :]
            for s in range(1, SUBLANES):
                acc = acc + prod_ref[pl.ds(base + s, SUBLANES, stride=SUBLANES), :]
            rows.append(acc)
        part = jnp.concatenate(rows, axis=0)
        act = lax.dot_general(ones, part, NT_DIMS, precision=lax.Precision.HIGHEST,
                              preferred_element_type=F32)
        w_ref[pl.ds(t, 1), :] = act[0:1, :]
        return carry

    lax.fori_loop(0, tb, token, 0)
    a = w_ref[...]
    w_ref[...] = gate_ref[...] * (0.5 * a * (1.0 + lax.erf(a * (2.0 ** -0.5))))


def _peer_u(idx, h2r, gate, tbl, tb):
    n = idx.shape[0]
    return pl.pallas_call(
        functools.partial(_peer_u_kernel, tb=tb),
        grid=(n // tb,),
        in_specs=[pl.BlockSpec((tb, N_SLOTS), lambda i: (i, 0), memory_space=pltpu.SMEM),
                  pl.BlockSpec((tb, SUBLANES, LANES), lambda i: (i, 0, 0)),
                  pl.BlockSpec((tb, N_SLOTS), lambda i: (i, 0)),
                  pl.BlockSpec(memory_space=pltpu.VMEM)],
        out_specs=pl.BlockSpec((tb, N_SLOTS), lambda i: (i, 0)),
        out_shape=jax.ShapeDtypeStruct((n, N_SLOTS), F32),
        scratch_shapes=[pltpu.VMEM((N_SLOTS * SUBLANES, LANES), F32)],
        compiler_params=_cparams(("arbitrary",)),
        name="peer_u",
    )(idx, h2r, gate, tbl)


def _peer_v_kernel(idx_ref, w_ref, tbl_ref, o_ref, *, tb):
    n_acc = 4

    def token(t, carry):
        accs = [jnp.zeros((SUBLANES, LANES), F32) for _ in range(n_acc)]
        for e in range(N_SLOTS):
            accs[e % n_acc] = accs[e % n_acc] + w_ref[t, e] * _gather_row(tbl_ref, idx_ref[t, e])
        o_ref[t] = (accs[0] + accs[1]) + (accs[2] + accs[3])
        return carry

    lax.fori_loop(0, tb, token, 0)


def _peer_v(idx, w, tbl, tb):
    n = idx.shape[0]
    smem = pl.BlockSpec((tb, N_SLOTS), lambda i: (i, 0), memory_space=pltpu.SMEM)
    return pl.pallas_call(
        functools.partial(_peer_v_kernel, tb=tb),
        grid=(n // tb,),
        in_specs=[smem, smem, pl.BlockSpec(memory_space=pltpu.VMEM)],
        out_specs=pl.BlockSpec((tb, SUBLANES, LANES), lambda i: (i, 0, 0)),
        out_shape=jax.ShapeDtypeStruct((n, SUBLANES, LANES), F32),
        compiler_params=_cparams(("arbitrary",)),
        name="peer_v",
    )(idx, w, tbl)


def _final_kernel(xp_ref, po_ref, g2_ref, fg_ref, y_ref):
    x = xp_ref[...] + g2_ref[...] * po_ref[...]
    ms = jnp.mean(x * x, axis=-1, keepdims=True)
    y_ref[...] = x * lax.rsqrt(ms + EPS) * fg_ref[...]


def _final(xp, po, g2, fg, per_row, rows_per_mod, tm):
    n = xp.shape[0]
    row = pl.BlockSpec((tm, D_MODEL), lambda i: (i, 0))
    return pl.pallas_call(
        _final_kernel,
        grid=(n // tm,),
        in_specs=[row, row, _mod_spec(per_row, tm, rows_per_mod),
                  pl.BlockSpec((1, D_MODEL), lambda i: (0, 0))],
        out_specs=row,
        out_shape=jax.ShapeDtypeStruct((n, D_MODEL), F32),
        compiler_params=_cparams(("parallel",)),
        name="final_norm",
    )(xp, po, g2, fg)


def _peer(h2, xp, g2, fg, wq_t16, subkeys16, u_tbl, v_tbl, per_row, rows_per_mod, tm, tb_topk, tb_gather):
    n = h2.shape[0]
    idx, gate = _topk(h2, wq_t16, subkeys16, tb_topk)
    w = _peer_u(idx, h2.reshape(n, SUBLANES, LANES), gate, u_tbl, tb_gather)
    po = _peer_v(idx, w, v_tbl, tb_gather).reshape(n, D_MODEL)
    return _final(xp, po, g2, fg, per_row, rows_per_mod, tm)


def kernel(x_prompt, x_sample, c_prompt, c_sample, cache_a_k, cache_a_v, cache_b_k, cache_b_v, cache_b_logf, page_table, w_ada, b_ada, norm1_g, w_in, b_f, lambda_q1, lambda_k1, lambda_q2, lambda_k2, subln_g, w_out, norm2_g, peer_wq, peer_subkeys, peer_u, peer_v, final_g):
    batch, seq_len, d = x_prompt.shape
    dec_batch, new_tokens, _ = x_sample.shape
    depth = w_ada.shape[0]
    assert depth == 1 and d == D_MODEL and new_tokens == 4
    n_p, n_s = batch * seq_len, dec_batch * new_tokens
    n_pool, page = cache_a_k.shape[1], cache_a_k.shape[2]
    l = 0
    lam_init = 0.8 - 0.6 * math.exp(-0.3 * l)

    n_c = batch + dec_batch
    pad = (-n_c) % SUBLANES
    c_all = jnp.concatenate([c_prompt, c_sample, jnp.zeros((pad, d), F32)], axis=0)
    ada = _ada(c_all, w_ada[l], b_ada[l])
    mods_p = [m.reshape(batch, 1, d) for m in jnp.split(ada[:batch], 6, axis=-1)]
    mods_s = [jnp.repeat(m, new_tokens, axis=0) for m in jnp.split(ada[batch:n_c], 6, axis=-1)]
    lam = _lam(lambda_q1[l], lambda_k1[l], lambda_q2[l], lambda_k2[l], lam_init)

    n_main = 3 * WA + 3 * WB
    w_main = w_in[l][:, :n_main].astype(BF16)
    w_f = jnp.pad(w_in[l][:, n_main:], ((0, 0), (0, LANES - HB))).astype(BF16)
    bf = jnp.pad(b_f[l], (0, LANES - HB)).reshape(1, LANES)
    g1n = norm1_g[l].reshape(1, d)
    g2n = norm2_g[l].reshape(1, d)
    sub_g = subln_g[l].reshape(1, DVA)
    w_out16 = w_out[l].astype(BF16)
    wq_t16 = peer_wq[l].T.astype(BF16)
    subkeys16 = peer_subkeys[l].astype(BF16)
    u_tbl = _pack_table(peer_u[l])
    v_tbl = _pack_table(peer_v[l])
    fg = final_g.reshape(1, d)

    tm_p = 256
    xp2d = x_prompt.reshape(n_p, d)
    sh1, sc1, g1, sh2, sc2, g2 = mods_p
    (ka, va, kb, vb, lf, qa16, ka16, va16, qb16, kb16, vb16, qbias, kbias) = _proj(
        xp2d, sh1, sc1, g1n, w_main, w_f, bf, False, seq_len, seq_len, tm_p)
    qbias_a, kbias_a = _alibi_bias(seq_len)
    tq = 256
    oa = _attention(qa16, qbias_a, ka16, kbias_a, va16, lam, batch, seq_len, tq, fox=False)
    ob = _attention(qb16, qbias, kb16, kbias, vb16, lam, batch, seq_len, tq, fox=True)
    xp_mid, h2 = _merge(oa, ob, xp2d, g1, sh2, sc2, sub_g, w_out16, g2n, False, seq_len, tm_p,
                        1.0 - lam_init)
    y_prompt = _peer(h2, xp_mid, g2, fg, wq_t16, subkeys16, u_tbl, v_tbl, False, seq_len, tm_p,
                     256, 64).reshape(batch, seq_len, d)

    tm_s = n_s
    xs2d = x_sample.reshape(n_s, d)
    sh1, sc1, g1, sh2, sc2, g2 = mods_s
    (ka_s, va_s, kb_s, vb_s, lf_s, qa16_s, _, _, qb16_s, _, _, _, _) = _proj(
        xs2d, sh1, sc1, g1n, w_main, w_f, bf, True, n_s, n_s, tm_s)
    qa5 = qa16_s.reshape(dec_batch, new_tokens, HA, 2, DA)
    eye_h = jnp.eye(HA, dtype=BF16)
    eye_2 = jnp.eye(2, dtype=BF16)
    wqa = (qa5.transpose(0, 3, 2, 1, 4)[:, :, :, :, None, None, :]
           * eye_2[None, :, None, None, None, :, None] * eye_h[None, None, :, None, :, None, None])
    wqa = wqa.reshape(dec_batch, 2 * HA * new_tokens, WA)
    qb4 = qb16_s.reshape(dec_batch, new_tokens, HB, DB)
    wqb = (qb4[:, :, :, None, :] * jnp.eye(HB, dtype=BF16)[None, None, :, :, None])
    wqb = wqb.reshape(dec_batch, new_tokens * HB, WB)

    def new_page(a):
        w = a.shape[1]
        a = a.reshape(dec_batch, new_tokens, w)
        return jnp.pad(a, ((0, 0), (0, page - new_tokens), (0, 0)))

    caches = [cache_a_k[l].reshape(n_pool, page, WA), cache_a_v[l].reshape(n_pool, page, WA),
              cache_b_k[l].reshape(n_pool, page, WB), cache_b_v[l].reshape(n_pool, page, WB),
              cache_b_logf[l]]
    new_pages = [new_page(ka_s), new_page(va_s), new_page(kb_s), new_page(vb_s), new_page(lf_s)]
    oa_s, ob_s = _decode_attention(page_table, caches, new_pages, wqa, wqb, lam, new_tokens, 4)
    oa_s = oa_s.reshape(dec_batch, HA, new_tokens, DVA).transpose(0, 2, 1, 3).reshape(n_s, WA)
    ob_s = ob_s.reshape(dec_batch, new_tokens, HB, 2, DB)
    ob_s = jnp.where((jnp.arange(HB) % 2 == 0)[None, None, :, None], ob_s[:, :, :, 0, :], ob_s[:, :, :, 1, :])
    ob_s = ob_s.reshape(n_s, WB)
    xs_mid, h2_s = _merge(oa_s, ob_s, xs2d, g1, sh2, sc2, sub_g, w_out16, g2n, True, n_s, tm_s,
                          1.0 - lam_init)
    y_sample = _peer(h2_s, xs_mid, g2, fg, wq_t16, subkeys16, u_tbl, v_tbl, True, n_s, tm_s,
                     n_s, 64).reshape(dec_batch, new_tokens, d)

    def kv(a, b, t, h, w):
        return a.reshape(1, b, t, h, w)

    return (y_prompt, y_sample,
            kv(ka, batch, seq_len, HA, 2 * DA), kv(va, batch, seq_len, HA, DVA),
            kv(kb, batch, seq_len, HB, DB), kv(vb, batch, seq_len, HB, DB),
            lf.reshape(1, batch, seq_len, HB),
            kv(ka_s, dec_batch, new_tokens, HA, 2 * DA), kv(va_s, dec_batch, new_tokens, HA, DVA),
            kv(kb_s, dec_batch, new_tokens, HB, DB), kv(vb_s, dec_batch, new_tokens, HB, DB),
            lf_s.reshape(1, dec_batch, new_tokens, HB))
```

```python
import functools
import math

import jax
import jax.numpy as jnp
import numpy as np
from jax import lax
from jax.experimental import pallas as pl
from jax.experimental.pallas import tpu as pltpu

F32, BF16, I32 = jnp.float32, jnp.bfloat16, jnp.int32
EPS = 1e-6
D_MODEL = 1024
HA, DA, DVA = 4, 64, 128
HB, DB = 8, 64
WA, WB = HA * DVA, HB * DB
N_KEYS = 128
PEER_HEADS = 8
PEER_TOPK = 16
N_SLOTS = PEER_HEADS * PEER_TOPK
N_EXPERTS = N_KEYS * N_KEYS
LANES = 128
SUBLANES = 8
ROW_TILE = D_MODEL // (2 * LANES)
QPAD = SUBLANES
NEG_INF = float("-inf")
VMEM_LIMIT = 56 * 1024 * 1024
NT_DIMS = (((1,), (1,)), ((), ()))


def _cparams(sem):
    return pltpu.CompilerParams(dimension_semantics=sem, vmem_limit_bytes=VMEM_LIMIT)


def _split3(x):
    hi = x.astype(BF16)
    r = x - hi.astype(F32)
    mid = r.astype(BF16)
    lo = (r - mid.astype(F32)).astype(BF16)
    return hi, mid, lo


def _ada_kernel(c_ref, w_ref, b_ref, o_ref):
    c = c_ref[...]
    s = c * (1.0 / (1.0 + jnp.exp(-c)))
    o_ref[...] = jnp.dot(s.astype(BF16), w_ref[...].astype(BF16),
                         preferred_element_type=F32) + b_ref[...]


def _ada(c_all, w, b):
    r = c_all.shape[0]
    n_out = w.shape[1]
    return pl.pallas_call(
        _ada_kernel,
        grid=(n_out // D_MODEL,),
        in_specs=[pl.BlockSpec((r, D_MODEL), lambda j: (0, 0)),
                  pl.BlockSpec((D_MODEL, D_MODEL), lambda j: (0, j)),
                  pl.BlockSpec((1, D_MODEL), lambda j: (0, j))],
        out_specs=pl.BlockSpec((r, D_MODEL), lambda j: (0, j)),
        out_shape=jax.ShapeDtypeStruct((r, n_out), F32),
        compiler_params=_cparams(("parallel",)),
        name="ada",
    )(c_all, w, b.reshape(1, n_out))


def _lam_kernel(v_ref, o_ref, *, lam_init):
    v = v_ref[...]
    a = jnp.sum(v[0:1] * v[1:2], axis=1, keepdims=True)
    b = jnp.sum(v[2:3] * v[3:4], axis=1, keepdims=True)
    lam = jnp.exp(a) - jnp.exp(b) + lam_init
    o_ref[...] = jnp.broadcast_to(lam, o_ref.shape)


def _lam(lq1, lk1, lq2, lk2, lam_init):
    v = jnp.zeros((SUBLANES, LANES), F32)
    v = v.at[0:4, 0:DA].set(jnp.stack([lq1, lk1, lq2, lk2]))
    return pl.pallas_call(
        functools.partial(_lam_kernel, lam_init=lam_init),
        out_shape=jax.ShapeDtypeStruct((SUBLANES, LANES), F32),
        name="lam",
    )(v)


def _mod_spec(per_row, tm, rows_per_mod):
    if per_row:
        return pl.BlockSpec((tm, D_MODEL), lambda i: (i, 0))
    blocks = rows_per_mod // tm
    return pl.BlockSpec((None, 1, D_MODEL), lambda i: (i // blocks, 0, 0))


def _proj_kernel(x_ref, sh_ref, sc_ref, g_ref, w_ref, wf_ref, bf_ref, tri_ref, pq_ref, pk_ref,
                 qc_ref, kc_ref,
                 ka_ref, va_ref, kb_ref, vb_ref, lf_ref,
                 qa16_ref, ka16_ref, va16_ref, qb16_ref, kb16_ref, vb16_ref, qbias_ref, kbias_ref,
                 carry_ref, *, blocks_per_seq):
    i = pl.program_id(0)

    @pl.when(i % blocks_per_seq == 0)
    def _():
        carry_ref[...] = jnp.zeros_like(carry_ref)

    x = x_ref[...]
    ms = jnp.mean(x * x, axis=-1, keepdims=True)
    h = x * lax.rsqrt(ms + EPS) * g_ref[...]
    h = h * (1.0 + sc_ref[...]) + sh_ref[...]
    hb = h.astype(BF16)
    p = jnp.dot(hb, w_ref[...], preferred_element_type=F32)
    qa, ka, va = p[:, 0:WA], p[:, WA:2 * WA], p[:, 2 * WA:3 * WA]
    o = 3 * WA
    qb, kb, vb = p[:, o:o + WB], p[:, o + WB:o + 2 * WB], p[:, o + 2 * WB:o + 3 * WB]
    ka_ref[...] = ka
    va_ref[...] = va
    kb_ref[...] = kb
    vb_ref[...] = vb
    qa16_ref[...] = (qa * (DA ** -0.5)).astype(BF16)
    ka16_ref[...] = ka.astype(BF16)
    va16_ref[...] = va.astype(BF16)
    qb16_ref[...] = (qb * (DB ** -0.5)).astype(BF16)
    kb16_ref[...] = kb.astype(BF16)
    vb16_ref[...] = vb.astype(BF16)

    z = jnp.dot(hb, wf_ref[...], preferred_element_type=F32) + bf_ref[...]
    lf = jnp.minimum(z, 0.0) - jnp.log1p(jnp.exp(-jnp.abs(z)))
    lane = lax.broadcasted_iota(I32, lf.shape, 1)
    lf = jnp.where(lane < HB, lf, 0.0)
    lf_ref[...] = lf[:, 0:HB]

    tri = tri_ref[...]
    hi, mid, lo = _split3(lf)
    cum = (jnp.dot(tri, hi, preferred_element_type=F32)
           + jnp.dot(tri, mid, preferred_element_type=F32)
           + jnp.dot(tri, lo, preferred_element_type=F32)) + carry_ref[...]
    tm = cum.shape[0]
    carry_ref[...] = cum[tm - 1:tm, :]
    ccat = jnp.concatenate(_split3(cum), axis=1)
    qbias_ref[...] = (jnp.dot(ccat, pq_ref[...], preferred_element_type=F32) + qc_ref[...]).astype(BF16)
    kbias_ref[...] = (jnp.dot(ccat, pk_ref[...], preferred_element_type=F32) + kc_ref[...]).astype(BF16)


def _decay_placement():
    pq = np.zeros((3 * LANES, WB), np.float32)
    pk = np.zeros((3 * LANES, WB), np.float32)
    qc = np.zeros((1, WB), np.float32)
    kc = np.zeros((1, WB), np.float32)
    for h in range(HB):
        pair, which = divmod(h, 2)
        base = pair * LANES + which * 6
        for t in range(3):
            pq[t * LANES + h, base + t] = 1.0
            qc[0, base + 3 + t] = 1.0
            kc[0, base + t] = 1.0
            pk[t * LANES + h, base + 3 + t] = -1.0
    return (jnp.asarray(pq, BF16), jnp.asarray(pk, BF16), jnp.asarray(qc), jnp.asarray(kc))


def _proj(x2d, sh, sc, g, w_main, w_f, b_f, per_row, rows_per_mod, seq_len, tm):
    n = x2d.shape[0]
    nw = w_main.shape[1]
    tri = (np.arange(tm)[None, :] <= np.arange(tm)[:, None]).astype(np.float32)
    pq, pk, qc, kc = _decay_placement()
    mod = _mod_spec(per_row, tm, rows_per_mod)
    row = lambda w: pl.BlockSpec((tm, w), lambda i: (i, 0))
    const = lambda a: pl.BlockSpec(a.shape, lambda i: (0,) * a.ndim)
    tri = jnp.asarray(tri, BF16)
    outs = ([jax.ShapeDtypeStruct((n, WA), F32)] * 4 + [jax.ShapeDtypeStruct((n, HB), F32)]
            + [jax.ShapeDtypeStruct((n, WA), BF16)] * 8)
    return pl.pallas_call(
        functools.partial(_proj_kernel, blocks_per_seq=seq_len // tm),
        grid=(n // tm,),
        in_specs=[row(D_MODEL), mod, mod, const(g), const(w_main), const(w_f), const(b_f),
                  const(tri), const(pq), const(pk), const(qc), const(kc)],
        out_specs=[row(WA)] * 4 + [row(HB)] + [row(WA)] * 8,
        out_shape=outs,
        scratch_shapes=[pltpu.VMEM((1, LANES), F32)],
        compiler_params=_cparams(("arbitrary",)),
        name="proj",
    )(x2d, sh, sc, g, w_main, w_f, b_f, tri, pq, pk, qc, kc)


def _attn_kernel(q_ref, qb_ref, k_ref, kb_ref, v_ref, lam_ref, o_ref, m_ref, l_ref, acc_ref,
                 *, tq, fox):
    qi = pl.program_id(2)
    q = q_ref[...].astype(F32)
    qb = qb_ref[...].astype(F32)
    lane = lax.broadcasted_iota(I32, (tq, LANES), 1)
    q_lo = jnp.where(lane < DA, q, 0.0)
    q_hi = jnp.where(lane >= DA, q, 0.0)
    if fox:
        qb_lo = jnp.where(lane < 6, qb, 0.0)
        qb_hi = jnp.where((lane >= 6) & (lane < 12), qb, 0.0)
    else:
        qb_lo = qb_hi = qb
    qq = jnp.concatenate([jnp.concatenate([q_lo, qb_lo], axis=1),
                          jnp.concatenate([q_hi, qb_hi], axis=1)], axis=0).astype(BF16)
    m_ref[...] = jnp.full(m_ref.shape, NEG_INF, F32)
    l_ref[...] = jnp.zeros(l_ref.shape, F32)
    acc_ref[...] = jnp.zeros(acc_ref.shape, F32)

    n_chunks = tq // LANES

    def step(j, masked):
        off = pl.multiple_of(j * tq, tq)
        kk = jnp.concatenate([k_ref[pl.ds(off, tq), :], kb_ref[pl.ds(off, tq), :]], axis=1)
        s = lax.dot_general(qq, kk, NT_DIMS, preferred_element_type=F32)
        if masked:
            r = lax.broadcasted_iota(I32, s.shape, 0)
            c = lax.broadcasted_iota(I32, s.shape, 1)
            r = jnp.where(r >= tq, r - tq, r)
            s = jnp.where(c <= r, s, NEG_INF)
        chunks = [s[:, c * LANES:(c + 1) * LANES] for c in range(n_chunks)]
        m_prev = m_ref[...]
        m_cur = functools.reduce(jnp.maximum, chunks)
        m_new = jnp.maximum(m_prev, jnp.max(m_cur, axis=1, keepdims=True))
        alpha = jnp.exp(m_prev - m_new)
        ps = [jnp.exp(ch - m_new) for ch in chunks]
        l_ref[...] = alpha * l_ref[...] + functools.reduce(jnp.add, ps)
        p = jnp.concatenate([x.astype(BF16) for x in ps], axis=1)
        acc_ref[...] = alpha * acc_ref[...] + jnp.dot(p, v_ref[pl.ds(off, tq), :],
                                                      preferred_element_type=F32)
        m_ref[...] = m_new

    def body(j, carry):
        step(j, False)
        return carry

    lax.fori_loop(0, qi, body, 0)
    step(qi, True)
    o = acc_ref[...] / jnp.sum(l_ref[...], axis=1, keepdims=True)
    if fox:
        o_ref[...] = jnp.where(lane < DB, o[0:tq], o[tq:2 * tq])
    else:
        o_ref[...] = o[0:tq] - lam_ref[0:1, :] * o[tq:2 * tq]


def _attention(q16, qbias, k16, kbias, v16, lam, batch, seq_len, tq, fox):
    n = q16.shape[0]
    groups = q16.shape[1] // LANES
    nq = seq_len // tq
    q_spec = pl.BlockSpec((tq, LANES), lambda b, g, i: (b * nq + i, g))
    kv_spec = pl.BlockSpec((seq_len, LANES), lambda b, g, i: (b, g))
    if fox:
        qb_spec, kb_spec = q_spec, kv_spec
    else:
        qb_spec = pl.BlockSpec((None, tq, LANES), lambda b, g, i: (g, i, 0))
        kb_spec = pl.BlockSpec((None, seq_len, LANES), lambda b, g, i: (g, 0, 0))
    return pl.pallas_call(
        functools.partial(_attn_kernel, tq=tq, fox=fox),
        grid=(batch, groups, nq),
        in_specs=[q_spec, qb_spec, kv_spec, kb_spec, kv_spec,
                  pl.BlockSpec((SUBLANES, LANES), lambda b, g, i: (0, 0))],
        out_specs=q_spec,
        out_shape=jax.ShapeDtypeStruct((n, groups * LANES), F32),
        scratch_shapes=[pltpu.VMEM((2 * tq, LANES), F32)] * 3,
        compiler_params=_cparams(("parallel", "parallel", "arbitrary")),
        name="attn_fox" if fox else "attn_diff",
    )(q16, qbias, k16, kbias, v16, lam)


def _alibi_bias(seq_len):
    slopes = 2.0 ** (-8.0 * np.arange(1, HA + 1, dtype=np.float64) / HA)
    pos = np.arange(seq_len)
    hi, lo = (pos // LANES) * float(LANES), (pos % LANES).astype(np.float64)
    qb = np.zeros((HA, seq_len, LANES), np.float32)
    kb = np.zeros((HA, seq_len, LANES), np.float32)
    for h in range(HA):
        qb[h, :, 0], qb[h, :, 1], qb[h, :, 2], qb[h, :, 3] = -slopes[h] * hi, -slopes[h] * lo, 1.0, 1.0
        kb[h, :, 0], kb[h, :, 1], kb[h, :, 2], kb[h, :, 3] = 1.0, 1.0, slopes[h] * hi, slopes[h] * lo
    return jnp.asarray(qb, BF16), jnp.asarray(kb, BF16)


def _decode_kernel(pt_ref, *refs, pages_per_step, n_steps, page):
    del pt_ref
    pp = pages_per_step
    paged = [refs[c * pp:(c + 1) * pp] for c in range(5)]
    (nka_ref, nva_ref, nkb_ref, nvb_ref, nlf_ref, wqa_ref, wqb_ref, biasa_ref, slope_ref,
     maska_ref, maskb_ref, triu_ref, dmaskb_ref, lam_ref) = refs[5 * pp:5 * pp + 14]
    oa_ref, ob_ref = refs[5 * pp + 14:5 * pp + 16]
    (ma_ref, la_ref, acca_ref, mb_ref, lb_ref, accb_ref, carry_ref) = refs[5 * pp + 16:]
    j = pl.program_id(1)
    half_rows = HA * QPAD

    @pl.when(j == 0)
    def _():
        ma_ref[...] = jnp.full(ma_ref.shape, NEG_INF, F32)
        mb_ref[...] = jnp.full(mb_ref.shape, NEG_INF, F32)
        la_ref[...] = jnp.zeros(la_ref.shape, F32)
        lb_ref[...] = jnp.zeros(lb_ref.shape, F32)
        acca_ref[...] = jnp.zeros(acca_ref.shape, F32)
        accb_ref[...] = jnp.zeros(accb_ref.shape, F32)
        carry_ref[...] = jnp.zeros(carry_ref.shape, F32)

    def softmax_step(chunks, m_ref, l_ref):
        m_prev = m_ref[...]
        m_cur = functools.reduce(jnp.maximum, chunks)
        m_new = jnp.maximum(m_prev, jnp.max(m_cur, axis=1, keepdims=True))
        alpha = jnp.exp(m_prev - m_new)
        ps = [jnp.exp(s - m_new) for s in chunks]
        l_ref[...] = alpha * l_ref[...] + functools.reduce(jnp.add, ps)
        m_ref[...] = m_new
        return alpha, ps

    def process(pages):
        chunks = []
        for ka_r, _, _, _, _, page_idx, masked in pages:
            parts = []
            for h in range(HA):
                k_h = ka_r[pl.ds(h, page, stride=HA), :].astype(BF16)
                parts.append(lax.dot_general(wqa_ref[h * 2 * QPAD:(h + 1) * 2 * QPAD, :], k_h, NT_DIMS,
                                             preferred_element_type=F32))
            sa = jnp.concatenate([x[0:QPAD] for x in parts] + [x[QPAD:2 * QPAD] for x in parts], axis=0)
            sa = sa + biasa_ref[...] + slope_ref[...] * (page_idx * float(page))
            if masked:
                sa = jnp.where(maska_ref[...] > 0.0, sa, NEG_INF)
            chunks.append(sa)
        alpha, ps = softmax_step(chunks, ma_ref, la_ref)
        acc = alpha * acca_ref[...]
        for (_, va_r, _, _, _, _, _), p in zip(pages, ps):
            pv = []
            for h in range(HA):
                v_h = va_r[pl.ds(h, page, stride=HA), :].astype(BF16)
                p_h = jnp.concatenate([p[h * QPAD:(h + 1) * QPAD],
                                       p[half_rows + h * QPAD:half_rows + (h + 1) * QPAD]], axis=0)
                pv.append(jnp.dot(p_h.astype(BF16), v_h, preferred_element_type=F32))
            acc = acc + jnp.concatenate([x[0:QPAD] for x in pv] + [x[QPAD:2 * QPAD] for x in pv], axis=0)
        acca_ref[...] = acc
        triu = triu_ref[...]
        carry = carry_ref[...]
        chunks = []
        for _, _, kb_r, _, lf_r, _, masked in pages:
            hi, mid, lo = _split3(lf_r[...])
            cum = (jnp.dot(hi, triu, preferred_element_type=F32)
                   + jnp.dot(mid, triu, preferred_element_type=F32)
                   + jnp.dot(lo, triu, preferred_element_type=F32)) + carry
            carry = jnp.broadcast_to(cum[:, page - 1:page], carry.shape)
            sb = jnp.dot(wqb_ref[...], kb_r[...].astype(BF16), preferred_element_type=F32)
            sb = sb - jnp.concatenate([cum] * 4, axis=0)
            if masked:
                sb = jnp.where(maskb_ref[...] > 0.0, sb, NEG_INF)
            chunks.append(sb)
        carry_ref[...] = carry
        alpha, ps = softmax_step(chunks, mb_ref, lb_ref)
        acc = jnp.concatenate([alpha] * (WB // LANES), axis=1) * accb_ref[...]
        for (_, _, _, vb_r, _, _, _), p in zip(pages, ps):
            acc = acc + lax.dot_general(p.astype(BF16), vb_r[...].astype(BF16), NT_DIMS,
                                        preferred_element_type=F32)
        accb_ref[...] = acc

    cached = [(paged[0][k], paged[1][k], paged[2][k], paged[3][k], paged[4][k],
               (j * pp + k).astype(F32), False) for k in range(pp)]

    @pl.when(j < n_steps - 1)
    def _():
        process(cached)

    @pl.when(j == n_steps - 1)
    def _():
        process(cached + [(nka_ref, nva_ref, nkb_ref, nvb_ref, nlf_ref, float(n_steps * pp), True)])
        oa = acca_ref[...] / jnp.sum(la_ref[...], axis=1, keepdims=True)
        oa_ref[...] = oa[0:half_rows] - lam_ref[0:1, :] * oa[half_rows:2 * half_rows]
        ob = accb_ref[...] / jnp.sum(lb_ref[...], axis=1, keepdims=True) * dmaskb_ref[...]
        ob_ref[...] = ob[:, 0:128] + ob[:, 128:256] + ob[:, 256:384] + ob[:, 384:512]


def _decode_attention(page_table, caches, new_pages, wqa, wqb, lam, new_tokens, pages_per_step):
    dec_batch, n_pages = page_table.shape
    page = caches[4].shape[2]
    pp = pages_per_step
    n_steps = n_pages // pp
    past = n_pages * page
    rows_a = 2 * HA * QPAD
    rows_b = new_tokens * HB
    slopes = 2.0 ** (-8.0 * np.arange(1, HA + 1, dtype=np.float64) / HA)
    ra = np.arange(rows_a)
    ha, qa_i = (ra // QPAD) % HA, np.minimum(ra % QPAD, new_tokens - 1)
    col = np.arange(page)
    biasa = slopes[ha][:, None] * (col[None, :] - past - qa_i[:, None])
    slope_rep = np.repeat(slopes[ha][:, None], page, axis=1)
    maska = (col[None, :] <= qa_i[:, None]).astype(np.float32)
    rb = np.arange(rows_b)
    maskb = (col[None, :] <= (rb // HB)[:, None]).astype(np.float32)
    triu = (np.arange(page)[:, None] <= np.arange(page)[None, :]).astype(np.float32)
    cb = np.arange(WB)
    dmaskb = (cb[None, :] // DB == (rb % HB)[:, None]).astype(np.float32)
    consts = [jnp.asarray(biasa, F32), jnp.asarray(slope_rep, F32), jnp.asarray(maska),
              jnp.asarray(maskb), jnp.asarray(triu, BF16), jnp.asarray(dmaskb), lam]

    def page_spec(shape, k):
        return pl.BlockSpec((None,) + shape,
                            lambda b, j, pt: (pt[b * n_pages + j * pp + k], 0, 0))

    in_specs, args = [], []
    for c in caches:
        for k in range(pp):
            in_specs.append(page_spec(c.shape[1:], k))
            args.append(c)
    for a in new_pages + [wqa, wqb]:
        in_specs.append(pl.BlockSpec((None,) + a.shape[1:], lambda b, j, pt: (b, 0, 0)))
        args.append(a)
    for a in consts:
        in_specs.append(pl.BlockSpec(a.shape, lambda b, j, pt: (0, 0)))
        args.append(a)
    grid_spec = pltpu.PrefetchScalarGridSpec(
        num_scalar_prefetch=1,
        grid=(dec_batch, n_steps),
        in_specs=in_specs,
        out_specs=[pl.BlockSpec((None, rows_a // 2, LANES), lambda b, j, pt: (b, 0, 0)),
                   pl.BlockSpec((None, rows_b, LANES), lambda b, j, pt: (b, 0, 0))],
        scratch_shapes=[pltpu.VMEM((rows_a, LANES), F32), pltpu.VMEM((rows_a, LANES), F32),
                        pltpu.VMEM((rows_a, DVA), F32),
                        pltpu.VMEM((rows_b, LANES), F32), pltpu.VMEM((rows_b, LANES), F32),
                        pltpu.VMEM((rows_b, WB), F32),
                        pltpu.VMEM((HB, page), F32)],
    )
    return pl.pallas_call(
        functools.partial(_decode_kernel, pages_per_step=pp, n_steps=n_steps, page=page),
        grid_spec=grid_spec,
        out_shape=[jax.ShapeDtypeStruct((dec_batch, rows_a // 2, LANES), F32),
                   jax.ShapeDtypeStruct((dec_batch, rows_b, LANES), F32)],
        compiler_params=_cparams(("parallel", "arbitrary")),
        name="decode_attn",
    )(page_table.reshape(-1), *args)


def _merge_kernel(oa_ref, ob_ref, x_ref, g1_ref, sh2_ref, sc2_ref, sub_ref, wo_ref, n2_ref,
                  xp_ref, h2_ref, *, out_scale):
    oa = oa_ref[...]
    parts = []
    for h in range(HA):
        o = oa[:, h * DVA:(h + 1) * DVA]
        ms = jnp.mean(o * o, axis=-1, keepdims=True)
        parts.append(o * lax.rsqrt(ms + EPS) * sub_ref[...] * out_scale)
    o = jnp.concatenate(parts + [ob_ref[...]], axis=1).astype(BF16)
    y = jnp.dot(o, wo_ref[...], preferred_element_type=F32)
    xp = x_ref[...] + g1_ref[...] * y
    xp_ref[...] = xp
    ms = jnp.mean(xp * xp, axis=-1, keepdims=True)
    h2 = xp * lax.rsqrt(ms + EPS) * n2_ref[...]
    h2_ref[...] = h2 * (1.0 + sc2_ref[...]) + sh2_ref[...]


def _merge(oa, ob, x2d, g1, sh2, sc2, sub_g, w_out16, n2_g, per_row, rows_per_mod, tm, out_scale):
    n = x2d.shape[0]
    mod = _mod_spec(per_row, tm, rows_per_mod)
    row = lambda w: pl.BlockSpec((tm, w), lambda i: (i, 0))
    const = lambda a: pl.BlockSpec(a.shape, lambda i: (0,) * a.ndim)
    return pl.pallas_call(
        functools.partial(_merge_kernel, out_scale=out_scale),
        grid=(n // tm,),
        in_specs=[row(WA), row(WB), row(D_MODEL), mod, mod, mod, const(sub_g), const(w_out16),
                  const(n2_g)],
        out_specs=[row(D_MODEL), row(D_MODEL)],
        out_shape=[jax.ShapeDtypeStruct((n, D_MODEL), F32)] * 2,
        compiler_params=_cparams(("parallel",)),
        name="merge",
    )(oa, ob, x2d, g1, sh2, sc2, sub_g, w_out16, n2_g)


def _topk_kernel(h_ref, wq_ref, sk_ref, idx_ref, gate_ref, s_ref, ts_ref, ti_ref, be_ref, bg_ref, *, tb):
    hb = h_ref[...].astype(BF16)
    qt = lax.dot_general(wq_ref[...], hb, NT_DIMS, preferred_element_type=F32)
    for hp in range(2 * PEER_HEADS):
        sub = sk_ref[hp % 2]
        s_ref[hp] = jnp.dot(sub, qt[hp * N_KEYS:(hp + 1) * N_KEYS, :].astype(BF16),
                            preferred_element_type=F32)
    row_k = lax.broadcasted_iota(I32, (N_KEYS, tb), 0).astype(F32)

    def stage1(hp, carry):
        s = s_ref[hp]
        for r in range(PEER_TOPK):
            m = jnp.max(s, axis=0, keepdims=True)
            pick = jnp.min(jnp.where(s == m, row_k, float(N_KEYS)), axis=0, keepdims=True)
            ts_ref[hp, r:r + 1, :] = m
            ti_ref[hp, r:r + 1, :] = pick
            s = jnp.where(row_k == pick, NEG_INF, s)
        return carry

    lax.fori_loop(0, 2 * PEER_HEADS, stage1, 0)
    half = PEER_TOPK // 2
    n_cand = PEER_TOPK + (half - 1) * half + half
    row_c = lax.broadcasted_iota(I32, (n_cand, tb), 0).astype(F32)

    def candidates(x, y, combine):
        rows = [combine(x[0:1, :], y)]
        rows += [combine(x[i:i + 1, :], y[0:half, :]) for i in range(1, half)]
        rows.append(combine(x[half:PEER_TOPK, :], y[0:1, :]))
        return jnp.concatenate(rows, axis=0)

    def stage2(h, carry):
        a, b = ts_ref[2 * h], ts_ref[2 * h + 1]
        ia, ib = ti_ref[2 * h], ti_ref[2 * h + 1]
        cs = candidates(a, b, lambda x, y: x + y)
        ce = candidates(ia, ib, lambda x, y: x * float(N_KEYS) + y)
        best, experts = [], []
        for r in range(PEER_TOPK):
            m = jnp.max(cs, axis=0, keepdims=True)
            pick = jnp.min(jnp.where(cs == m, row_c, float(n_cand)), axis=0, keepdims=True)
            sel = row_c == pick
            experts.append(jnp.max(jnp.where(sel, ce, -1.0), axis=0, keepdims=True))
            best.append(m)
            cs = jnp.where(sel, NEG_INF, cs)
        bs = jnp.concatenate(best, axis=0)
        ex = jnp.exp(bs - bs[0:1, :])
        off = pl.multiple_of(h * PEER_TOPK, PEER_TOPK)
        bg_ref[pl.ds(off, PEER_TOPK), :] = ex / jnp.sum(ex, axis=0, keepdims=True)
        be_ref[pl.ds(off, PEER_TOPK), :] = jnp.concatenate(experts, axis=0)
        return carry

    lax.fori_loop(0, PEER_HEADS, stage2, 0)
    idx_ref[...] = (be_ref[...].T * float(ROW_TILE)).astype(I32)
    gate_ref[...] = bg_ref[...].T


def _topk(h2, wq_t16, subkeys16, tb):
    n = h2.shape[0]
    const = lambda a: pl.BlockSpec(a.shape, lambda i: (0,) * a.ndim)
    return pl.pallas_call(
        functools.partial(_topk_kernel, tb=tb),
        grid=(n // tb,),
        in_specs=[pl.BlockSpec((tb, D_MODEL), lambda i: (i, 0)), const(wq_t16), const(subkeys16)],
        out_specs=[pl.BlockSpec((tb, N_SLOTS), lambda i: (i, 0))] * 2,
        out_shape=[jax.ShapeDtypeStruct((n, N_SLOTS), I32), jax.ShapeDtypeStruct((n, N_SLOTS), F32)],
        scratch_shapes=[pltpu.VMEM((2 * PEER_HEADS, N_KEYS, tb), F32),
                        pltpu.VMEM((2 * PEER_HEADS, PEER_TOPK, tb), F32),
                        pltpu.VMEM((2 * PEER_HEADS, PEER_TOPK, tb), F32),
                        pltpu.VMEM((N_SLOTS, tb), F32), pltpu.VMEM((N_SLOTS, tb), F32)],
        compiler_params=_cparams(("parallel",)),
        name="peer_topk",
    )(h2, wq_t16, subkeys16)


def _pack_table(t):
    n_e, d = t.shape
    bits = lax.bitcast_convert_type(t.astype(BF16), jnp.uint16).astype(jnp.uint32)
    word = bits[:, :d // 2] | (bits[:, d // 2:] << 16)
    return lax.bitcast_convert_type(word, I32).reshape(n_e * ROW_TILE, LANES)


def _gather_row(tbl_ref, row):
    word = tbl_ref[pl.ds(pl.multiple_of(row, ROW_TILE), ROW_TILE), :]
    lo = lax.bitcast_convert_type(word << 16, F32)
    hi = lax.bitcast_convert_type(word & jnp.int32(-65536), F32)
    return lo, hi


def _peer_u_kernel(idx_ref, h_ref, gate_ref, tbl_ref, w_ref, part_ref, *prod_refs, tb):
    ones = jnp.ones((2 * SUBLANES, LANES), BF16)

    n_groups = N_SLOTS // SUBLANES

    def products(t, prod_ref, h_lo, h_hi, g):
        for e in range(g * SUBLANES, (g + 1) * SUBLANES):
            lo, hi = _gather_row(tbl_ref, idx_ref[t, e])
            prod_ref[e * SUBLANES:e * SUBLANES + ROW_TILE, :] = lo * h_lo + hi * h_hi

    def fold(prod_ref, g):
        base = g * SUBLANES * SUBLANES
        acc = prod_ref[pl.ds(base, SUBLANES, stride=SUBLANES), :]
        for s in range(1, ROW_TILE):
            acc = acc + prod_ref[pl.ds(base + s, SUBLANES, stride=SUBLANES), :]
        return acc

    def tokens(i, carry):
        for k, prod_ref in enumerate(prod_refs):
            t = i * len(prod_refs) + k
            h = h_ref[t]
            for g in range(n_groups):
                products(t, prod_ref, h[0:ROW_TILE], h[ROW_TILE:SUBLANES], g)
        for k, prod_ref in enumerate(prod_refs):
            t = i * len(prod_refs) + k
            for g in range(n_groups):
                row = pl.multiple_of(t * N_SLOTS + g * SUBLANES, SUBLANES)
                part_ref[pl.ds(row, SUBLANES), :] = fold(prod_ref, g)
        return carry

    lax.fori_loop(0, tb // len(prod_refs), tokens, 0)
    chunk_tokens = 8
    for c in range(tb // chunk_tokens):
        part = part_ref[c * chunk_tokens * N_SLOTS:(c + 1) * chunk_tokens * N_SLOTS, :]
        act = sum(lax.dot_general(ones, x, NT_DIMS, preferred_element_type=F32) for x in _split3(part))
        for k in range(chunk_tokens):
            t = c * chunk_tokens + k
            w_ref[t:t + 1, :] = act[0:1, k * N_SLOTS:(k + 1) * N_SLOTS]
    a = w_ref[...]
    w_ref[...] = gate_ref[...] * (0.5 * a * (1.0 + lax.erf(a * (2.0 ** -0.5))))


def _peer_u(idx, h2r, gate, tbl, tb):
    n = idx.shape[0]
    return pl.pallas_call(
        functools.partial(_peer_u_kernel, tb=tb),
        grid=(n // tb,),
        in_specs=[pl.BlockSpec((tb, N_SLOTS), lambda i: (i, 0), memory_space=pltpu.SMEM),
                  pl.BlockSpec((tb, SUBLANES, LANES), lambda i: (i, 0, 0)),
                  pl.BlockSpec((tb, N_SLOTS), lambda i: (i, 0)),
                  pl.BlockSpec(memory_space=pltpu.VMEM)],
        out_specs=pl.BlockSpec((tb, N_SLOTS), lambda i: (i, 0)),
        out_shape=jax.ShapeDtypeStruct((n, N_SLOTS), F32),
        scratch_shapes=[pltpu.VMEM((tb * N_SLOTS, LANES), F32)]
        + [pltpu.VMEM((N_SLOTS * SUBLANES, LANES), F32)] * 2,
        compiler_params=_cparams(("arbitrary",)),
        name="peer_u",
    )(idx, h2r, gate, tbl)


def _peer_v_kernel(idx_ref, w_ref, tbl_ref, o_ref, *, tb):
    n_acc = 4

    def token(t, carry):
        acc_lo = [jnp.zeros((ROW_TILE, LANES), F32) for _ in range(n_acc)]
        acc_hi = [jnp.zeros((ROW_TILE, LANES), F32) for _ in range(n_acc)]
        for e in range(N_SLOTS):
            lo, hi = _gather_row(tbl_ref, idx_ref[t, e])
            w = w_ref[t, e]
            acc_lo[e % n_acc] = acc_lo[e % n_acc] + w * lo
            acc_hi[e % n_acc] = acc_hi[e % n_acc] + w * hi
        o_ref[t] = jnp.concatenate([(acc_lo[0] + acc_lo[1]) + (acc_lo[2] + acc_lo[3]),
                                    (acc_hi[0] + acc_hi[1]) + (acc_hi[2] + acc_hi[3])], axis=0)
        return carry

    lax.fori_loop(0, tb, token, 0)


def _peer_v(idx, w, tbl, tb):
    n = idx.shape[0]
    smem = pl.BlockSpec((tb, N_SLOTS), lambda i: (i, 0), memory_space=pltpu.SMEM)
    return pl.pallas_call(
        functools.partial(_peer_v_kernel, tb=tb),
        grid=(n // tb,),
        in_specs=[smem, smem, pl.BlockSpec(memory_space=pltpu.VMEM)],
        out_specs=pl.BlockSpec((tb, SUBLANES, LANES), lambda i: (i, 0, 0)),
        out_shape=jax.ShapeDtypeStruct((n, SUBLANES, LANES), F32),
        compiler_params=_cparams(("arbitrary",)),
        name="peer_v",
    )(idx, w, tbl)


def _final_kernel(xp_ref, po_ref, g2_ref, fg_ref, y_ref):
    x = xp_ref[...] + g2_ref[...] * po_ref[...]
    ms = jnp.mean(x * x, axis=-1, keepdims=True)
    y_ref[...] = x * lax.rsqrt(ms + EPS) * fg_ref[...]


def _final(xp, po, g2, fg, per_row, rows_per_mod, tm):
    n = xp.shape[0]
    row = pl.BlockSpec((tm, D_MODEL), lambda i: (i, 0))
    return pl.pallas_call(
        _final_kernel,
        grid=(n // tm,),
        in_specs=[row, row, _mod_spec(per_row, tm, rows_per_mod),
                  pl.BlockSpec((1, D_MODEL), lambda i: (0, 0))],
        out_specs=row,
        out_shape=jax.ShapeDtypeStruct((n, D_MODEL), F32),
        compiler_params=_cparams(("parallel",)),
        name="final_norm",
    )(xp, po, g2, fg)


def _peer(h2, xp, g2, fg, wq_t16, subkeys16, u_tbl, v_tbl, per_row, rows_per_mod, tm, tb_topk, tb_gather):
    n = h2.shape[0]
    idx, gate = _topk(h2, wq_t16, subkeys16, tb_topk)
    w = _peer_u(idx, h2.reshape(n, SUBLANES, LANES), gate, u_tbl, tb_gather)
    po = _peer_v(idx, w, v_tbl, tb_gather).reshape(n, D_MODEL)
    return _final(xp, po, g2, fg, per_row, rows_per_mod, tm)


def kernel(x_prompt, x_sample, c_prompt, c_sample, cache_a_k, cache_a_v, cache_b_k, cache_b_v, cache_b_logf, page_table, w_ada, b_ada, norm1_g, w_in, b_f, lambda_q1, lambda_k1, lambda_q2, lambda_k2, subln_g, w_out, norm2_g, peer_wq, peer_subkeys, peer_u, peer_v, final_g):
    batch, seq_len, d = x_prompt.shape
    dec_batch, new_tokens, _ = x_sample.shape
    depth = w_ada.shape[0]
    assert depth == 1 and d == D_MODEL and new_tokens == 4
    n_p, n_s = batch * seq_len, dec_batch * new_tokens
    n_pool, page = cache_a_k.shape[1], cache_a_k.shape[2]
    l = 0
    lam_init = 0.8 - 0.6 * math.exp(-0.3 * l)

    n_c = batch + dec_batch
    pad = (-n_c) % SUBLANES
    c_all = jnp.concatenate([c_prompt, c_sample, jnp.zeros((pad, d), F32)], axis=0)
    ada = _ada(c_all, w_ada[l], b_ada[l])
    mods_p = [m.reshape(batch, 1, d) for m in jnp.split(ada[:batch], 6, axis=-1)]
    mods_s = [jnp.repeat(m, new_tokens, axis=0) for m in jnp.split(ada[batch:n_c], 6, axis=-1)]
    lam = _lam(lambda_q1[l], lambda_k1[l], lambda_q2[l], lambda_k2[l], lam_init)

    n_main = 3 * WA + 3 * WB
    w_main = w_in[l][:, :n_main].astype(BF16)
    w_f = jnp.pad(w_in[l][:, n_main:], ((0, 0), (0, LANES - HB))).astype(BF16)
    bf = jnp.pad(b_f[l], (0, LANES - HB)).reshape(1, LANES)
    g1n = norm1_g[l].reshape(1, d)
    g2n = norm2_g[l].reshape(1, d)
    sub_g = subln_g[l].reshape(1, DVA)
    w_out16 = w_out[l].astype(BF16)
    wq_t16 = peer_wq[l].T.astype(BF16)
    subkeys16 = peer_subkeys[l].astype(BF16)
    u_tbl = _pack_table(peer_u[l])
    v_tbl = _pack_table(peer_v[l])
    fg = final_g.reshape(1, d)

    tm_p = 256
    xp2d = x_prompt.reshape(n_p, d)
    sh1, sc1, g1, sh2, sc2, g2 = mods_p
    (ka, va, kb, vb, lf, qa16, ka16, va16, qb16, kb16, vb16, qbias, kbias) = _proj(
        xp2d, sh1, sc1, g1n, w_main, w_f, bf, False, seq_len, seq_len, tm_p)
    qbias_a, kbias_a = _alibi_bias(seq_len)
    tq = 512
    oa = _attention(qa16, qbias_a, ka16, kbias_a, va16, lam, batch, seq_len, tq, fox=False)
    ob = _attention(qb16, qbias, kb16, kbias, vb16, lam, batch, seq_len, tq, fox=True)
    xp_mid, h2 = _merge(oa, ob, xp2d, g1, sh2, sc2, sub_g, w_out16, g2n, False, seq_len, tm_p,
                        1.0 - lam_init)
    y_prompt = _peer(h2, xp_mid, g2, fg, wq_t16, subkeys16, u_tbl, v_tbl, False, seq_len, tm_p,
                     256, 64).reshape(batch, seq_len, d)

    tm_s = n_s
    xs2d = x_sample.reshape(n_s, d)
    sh1, sc1, g1, sh2, sc2, g2 = mods_s
    (ka_s, va_s, kb_s, vb_s, lf_s, qa16_s, _, _, qb16_s, _, _, _, _) = _proj(
        xs2d, sh1, sc1, g1n, w_main, w_f, bf, True, n_s, n_s, tm_s)
    assert page == LANES
    qa5 = qa16_s.reshape(dec_batch, new_tokens, HA, 2, DA)
    zeros = jnp.zeros_like(qa5[:, :, :, 0, :])
    wqa = jnp.stack([jnp.concatenate([qa5[:, :, :, 0, :], zeros], axis=-1),
                     jnp.concatenate([zeros, qa5[:, :, :, 1, :]], axis=-1)], axis=3)
    wqa = jnp.pad(wqa, ((0, 0), (0, QPAD - new_tokens), (0, 0), (0, 0), (0, 0)))
    wqa = wqa.transpose(0, 2, 3, 1, 4).reshape(dec_batch, HA * 2 * QPAD, 2 * DA)
    qb4 = qb16_s.reshape(dec_batch, new_tokens, HB, DB)
    wqb = (qb4[:, :, :, None, :] * jnp.eye(HB, dtype=BF16)[None, None, :, :, None])
    wqb = wqb.reshape(dec_batch, new_tokens * HB, WB)

    def pos_major(c):
        return c.reshape(n_pool, page * HA, c.shape[-1])

    def pos_minor(c):
        return c.transpose(0, 2, 3, 1).reshape(n_pool, HB * DB, page)

    caches = [pos_major(cache_a_k[l]), pos_major(cache_a_v[l]), pos_minor(cache_b_k[l]),
              pos_minor(cache_b_v[l]), cache_b_logf[l].transpose(0, 2, 1)]

    def new_pos_major(a):
        a = a.reshape(dec_batch, new_tokens * HA, a.shape[1] // HA)
        return jnp.pad(a, ((0, 0), (0, (page - new_tokens) * HA), (0, 0)))

    def new_pos_minor(a, heads):
        w = a.shape[1] // heads
        a = a.reshape(dec_batch, new_tokens, heads, w).transpose(0, 2, 3, 1)
        return jnp.pad(a, ((0, 0), (0, 0), (0, 0), (0, page - new_tokens))).reshape(dec_batch, heads * w, page)

    new_pages = [new_pos_major(ka_s), new_pos_major(va_s), new_pos_minor(kb_s, HB),
                 new_pos_minor(vb_s, HB), new_pos_minor(lf_s, HB)]
    oa_s, ob_s = _decode_attention(page_table, caches, new_pages, wqa, wqb, lam, new_tokens, 4)
    oa_s = oa_s.reshape(dec_batch, HA, QPAD, DVA)[:, :, :new_tokens].transpose(0, 2, 1, 3).reshape(n_s, WA)
    ob_s = ob_s.reshape(dec_batch, new_tokens, HB, 2, DB)
    ob_s = jnp.where((jnp.arange(HB) % 2 == 0)[None, None, :, None], ob_s[:, :, :, 0, :], ob_s[:, :, :, 1, :])
    ob_s = ob_s.reshape(n_s, WB)
    xs_mid, h2_s = _merge(oa_s, ob_s, xs2d, g1, sh2, sc2, sub_g, w_out16, g2n, True, n_s, tm_s,
                          1.0 - lam_init)
    y_sample = _peer(h2_s, xs_mid, g2, fg, wq_t16, subkeys16, u_tbl, v_tbl, True, n_s, tm_s,
                     n_s, 64).reshape(dec_batch, new_tokens, d)

    def kv(a, b, t, h, w):
        return a.reshape(1, b, t, h, w)

    return (y_prompt, y_sample,
            kv(ka, batch, seq_len, HA, 2 * DA), kv(va, batch, seq_len, HA, DVA),
            kv(kb, batch, seq_len, HB, DB), kv(vb, batch, seq_len, HB, DB),
            lf.reshape(1, batch, seq_len, HB),
            kv(ka_s, dec_batch, new_tokens, HA, 2 * DA), kv(va_s, dec_batch, new_tokens, HA, DVA),
            kv(kb_s, dec_batch, new_tokens, HB, DB), kv(vb_s, dec_batch, new_tokens, HB, DB),
            lf_s.reshape(1, dec_batch, new_tokens, HB))
```

```python
import functools
import math

import jax
import jax.numpy as jnp
import numpy as np
from jax import lax
from jax.experimental import pallas as pl
from jax.experimental.pallas import tpu as pltpu

F32, BF16, I32 = jnp.float32, jnp.bfloat16, jnp.int32
EPS = 1e-6
D_MODEL = 1024
HA, DA, DVA = 4, 64, 128
HB, DB = 8, 64
WA, WB = HA * DVA, HB * DB
N_KEYS = 128
PEER_HEADS = 8
PEER_TOPK = 16
N_SLOTS = PEER_HEADS * PEER_TOPK
N_EXPERTS = N_KEYS * N_KEYS
LANES = 128
SUBLANES = 8
ROW_TILE = D_MODEL // (2 * LANES)
QPAD = SUBLANES
PAIRS_PER_TRIP = 4
NEG_INF = float("-inf")
VMEM_LIMIT = 56 * 1024 * 1024
NT_DIMS = (((1,), (1,)), ((), ()))


def _cparams(sem):
    return pltpu.CompilerParams(dimension_semantics=sem, vmem_limit_bytes=VMEM_LIMIT)


def _split3(x):
    hi = x.astype(BF16)
    r = x - hi.astype(F32)
    mid = r.astype(BF16)
    lo = (r - mid.astype(F32)).astype(BF16)
    return hi, mid, lo


def _ada_kernel(c_ref, w_ref, b_ref, o_ref):
    c = c_ref[...]
    s = c * (1.0 / (1.0 + jnp.exp(-c)))
    o_ref[...] = jnp.dot(s.astype(BF16), w_ref[...].astype(BF16),
                         preferred_element_type=F32) + b_ref[...]


def _ada(c_all, w, b):
    r = c_all.shape[0]
    n_out = w.shape[1]
    return pl.pallas_call(
        _ada_kernel,
        grid=(n_out // D_MODEL,),
        in_specs=[pl.BlockSpec((r, D_MODEL), lambda j: (0, 0)),
                  pl.BlockSpec((D_MODEL, D_MODEL), lambda j: (0, j)),
                  pl.BlockSpec((1, D_MODEL), lambda j: (0, j))],
        out_specs=pl.BlockSpec((r, D_MODEL), lambda j: (0, j)),
        out_shape=jax.ShapeDtypeStruct((r, n_out), F32),
        compiler_params=_cparams(("parallel",)),
        name="ada",
    )(c_all, w, b.reshape(1, n_out))


def _lam_kernel(v_ref, o_ref, *, lam_init):
    v = v_ref[...]
    a = jnp.sum(v[0:1] * v[1:2], axis=1, keepdims=True)
    b = jnp.sum(v[2:3] * v[3:4], axis=1, keepdims=True)
    lam = jnp.exp(a) - jnp.exp(b) + lam_init
    o_ref[...] = jnp.broadcast_to(lam, o_ref.shape)


def _lam(lq1, lk1, lq2, lk2, lam_init):
    v = jnp.zeros((SUBLANES, LANES), F32)
    v = v.at[0:4, 0:DA].set(jnp.stack([lq1, lk1, lq2, lk2]))
    return pl.pallas_call(
        functools.partial(_lam_kernel, lam_init=lam_init),
        out_shape=jax.ShapeDtypeStruct((SUBLANES, LANES), F32),
        name="lam",
    )(v)


def _mod_spec(per_row, tm, rows_per_mod):
    if per_row:
        return pl.BlockSpec((tm, D_MODEL), lambda i: (i, 0))
    blocks = rows_per_mod // tm
    return pl.BlockSpec((None, 1, D_MODEL), lambda i: (i // blocks, 0, 0))


def _proj_kernel(x_ref, sh_ref, sc_ref, g_ref, w_ref, wf_ref, bf_ref, tri_ref, pq_ref, pk_ref,
                 qc_ref, kc_ref,
                 ka_ref, va_ref, kb_ref, vb_ref, lf_ref,
                 qa16_ref, ka16_ref, va16_ref, qb16_ref, kb16_ref, vb16_ref, qbias_ref, kbias_ref,
                 carry_ref, *, blocks_per_seq):
    i = pl.program_id(0)

    @pl.when(i % blocks_per_seq == 0)
    def _():
        carry_ref[...] = jnp.zeros_like(carry_ref)

    x = x_ref[...]
    ms = jnp.mean(x * x, axis=-1, keepdims=True)
    h = x * lax.rsqrt(ms + EPS) * g_ref[...]
    h = h * (1.0 + sc_ref[...]) + sh_ref[...]
    hb = h.astype(BF16)
    p = jnp.dot(hb, w_ref[...], preferred_element_type=F32)
    qa, ka, va = p[:, 0:WA], p[:, WA:2 * WA], p[:, 2 * WA:3 * WA]
    o = 3 * WA
    qb, kb, vb = p[:, o:o + WB], p[:, o + WB:o + 2 * WB], p[:, o + 2 * WB:o + 3 * WB]
    ka_ref[...] = ka
    va_ref[...] = va
    kb_ref[...] = kb
    vb_ref[...] = vb
    qa16_ref[...] = (qa * (DA ** -0.5)).astype(BF16)
    ka16_ref[...] = ka.astype(BF16)
    va16_ref[...] = va.astype(BF16)
    qb16_ref[...] = (qb * (DB ** -0.5)).astype(BF16)
    kb16_ref[...] = kb.astype(BF16)
    vb16_ref[...] = vb.astype(BF16)

    z = jnp.dot(hb, wf_ref[...], preferred_element_type=F32) + bf_ref[...]
    lf = jnp.minimum(z, 0.0) - jnp.log1p(jnp.exp(-jnp.abs(z)))
    lane = lax.broadcasted_iota(I32, lf.shape, 1)
    lf = jnp.where(lane < HB, lf, 0.0)
    lf_ref[...] = lf[:, 0:HB]

    tri = tri_ref[...]
    hi, mid, lo = _split3(lf)
    cum = (jnp.dot(tri, hi, preferred_element_type=F32)
           + jnp.dot(tri, mid, preferred_element_type=F32)
           + jnp.dot(tri, lo, preferred_element_type=F32)) + carry_ref[...]
    tm = cum.shape[0]
    carry_ref[...] = cum[tm - 1:tm, :]
    ccat = jnp.concatenate(_split3(cum), axis=1)
    qbias_ref[...] = (jnp.dot(ccat, pq_ref[...], preferred_element_type=F32) + qc_ref[...]).astype(BF16)
    kbias_ref[...] = (jnp.dot(ccat, pk_ref[...], preferred_element_type=F32) + kc_ref[...]).astype(BF16)


def _decay_placement():
    pq = np.zeros((3 * LANES, WB), np.float32)
    pk = np.zeros((3 * LANES, WB), np.float32)
    qc = np.zeros((1, WB), np.float32)
    kc = np.zeros((1, WB), np.float32)
    for h in range(HB):
        pair, which = divmod(h, 2)
        base = pair * LANES + which * 6
        for t in range(3):
            pq[t * LANES + h, base + t] = 1.0
            qc[0, base + 3 + t] = 1.0
            kc[0, base + t] = 1.0
            pk[t * LANES + h, base + 3 + t] = -1.0
    return (jnp.asarray(pq, BF16), jnp.asarray(pk, BF16), jnp.asarray(qc), jnp.asarray(kc))


def _proj(x2d, sh, sc, g, w_main, w_f, b_f, per_row, rows_per_mod, seq_len, tm):
    n = x2d.shape[0]
    nw = w_main.shape[1]
    tri = (np.arange(tm)[None, :] <= np.arange(tm)[:, None]).astype(np.float32)
    pq, pk, qc, kc = _decay_placement()
    mod = _mod_spec(per_row, tm, rows_per_mod)
    row = lambda w: pl.BlockSpec((tm, w), lambda i: (i, 0))
    const = lambda a: pl.BlockSpec(a.shape, lambda i: (0,) * a.ndim)
    tri = jnp.asarray(tri, BF16)
    outs = ([jax.ShapeDtypeStruct((n, WA), F32)] * 4 + [jax.ShapeDtypeStruct((n, HB), F32)]
            + [jax.ShapeDtypeStruct((n, WA), BF16)] * 8)
    return pl.pallas_call(
        functools.partial(_proj_kernel, blocks_per_seq=seq_len // tm),
        grid=(n // tm,),
        in_specs=[row(D_MODEL), mod, mod, const(g), const(w_main), const(w_f), const(b_f),
                  const(tri), const(pq), const(pk), const(qc), const(kc)],
        out_specs=[row(WA)] * 4 + [row(HB)] + [row(WA)] * 8,
        out_shape=outs,
        scratch_shapes=[pltpu.VMEM((1, LANES), F32)],
        compiler_params=_cparams(("arbitrary",)),
        name="proj",
    )(x2d, sh, sc, g, w_main, w_f, b_f, tri, pq, pk, qc, kc)


def _attn_kernel(q_ref, qb_ref, k_ref, kb_ref, v_ref, lam_ref, o_ref, m_ref, l_ref, acc_ref,
                 *, tq, fox):
    qi = pl.program_id(2)
    q = q_ref[...].astype(F32)
    qb = qb_ref[...].astype(F32)
    lane = lax.broadcasted_iota(I32, (tq, LANES), 1)
    q_lo = jnp.where(lane < DA, q, 0.0)
    q_hi = jnp.where(lane >= DA, q, 0.0)
    if fox:
        qb_lo = jnp.where(lane < 6, qb, 0.0)
        qb_hi = jnp.where((lane >= 6) & (lane < 12), qb, 0.0)
    else:
        qb_lo = qb_hi = qb
    qq = jnp.concatenate([jnp.concatenate([q_lo, qb_lo], axis=1),
                          jnp.concatenate([q_hi, qb_hi], axis=1)], axis=0).astype(BF16)
    m_ref[...] = jnp.full(m_ref.shape, NEG_INF, F32)
    l_ref[...] = jnp.zeros(l_ref.shape, F32)
    acc_ref[...] = jnp.zeros(acc_ref.shape, F32)

    n_chunks = tq // LANES

    def step(j, masked):
        off = pl.multiple_of(j * tq, tq)
        kk = jnp.concatenate([k_ref[pl.ds(off, tq), :], kb_ref[pl.ds(off, tq), :]], axis=1)
        s = lax.dot_general(qq, kk, NT_DIMS, preferred_element_type=F32)
        if masked:
            r = lax.broadcasted_iota(I32, s.shape, 0)
            c = lax.broadcasted_iota(I32, s.shape, 1)
            r = jnp.where(r >= tq, r - tq, r)
            s = jnp.where(c <= r, s, NEG_INF)
        chunks = [s[:, c * LANES:(c + 1) * LANES] for c in range(n_chunks)]
        m_prev = m_ref[...]
        m_cur = functools.reduce(jnp.maximum, chunks)
        m_new = jnp.maximum(m_prev, jnp.max(m_cur, axis=1, keepdims=True))
        alpha = jnp.exp(m_prev - m_new)
        ps = [jnp.exp(ch - m_new) for ch in chunks]
        l_ref[...] = alpha * l_ref[...] + functools.reduce(jnp.add, ps)
        p = jnp.concatenate([x.astype(BF16) for x in ps], axis=1)
        acc_ref[...] = alpha * acc_ref[...] + jnp.dot(p, v_ref[pl.ds(off, tq), :],
                                                      preferred_element_type=F32)
        m_ref[...] = m_new

    def body(j, carry):
        step(j, False)
        return carry

    lax.fori_loop(0, qi, body, 0)
    step(qi, True)
    o = acc_ref[...] / jnp.sum(l_ref[...], axis=1, keepdims=True)
    if fox:
        o_ref[...] = jnp.where(lane < DB, o[0:tq], o[tq:2 * tq])
    else:
        o_ref[...] = o[0:tq] - lam_ref[0:1, :] * o[tq:2 * tq]


def _attention(q16, qbias, k16, kbias, v16, lam, batch, seq_len, tq, fox):
    n = q16.shape[0]
    groups = q16.shape[1] // LANES
    nq = seq_len // tq
    q_spec = pl.BlockSpec((tq, LANES), lambda b, g, i: (b * nq + i, g))
    kv_spec = pl.BlockSpec((seq_len, LANES), lambda b, g, i: (b, g))
    if fox:
        qb_spec, kb_spec = q_spec, kv_spec
    else:
        qb_spec = pl.BlockSpec((None, tq, LANES), lambda b, g, i: (g, i, 0))
        kb_spec = pl.BlockSpec((None, seq_len, LANES), lambda b, g, i: (g, 0, 0))
    return pl.pallas_call(
        functools.partial(_attn_kernel, tq=tq, fox=fox),
        grid=(batch, groups, nq),
        in_specs=[q_spec, qb_spec, kv_spec, kb_spec, kv_spec,
                  pl.BlockSpec((SUBLANES, LANES), lambda b, g, i: (0, 0))],
        out_specs=q_spec,
        out_shape=jax.ShapeDtypeStruct((n, groups * LANES), F32),
        scratch_shapes=[pltpu.VMEM((2 * tq, LANES), F32)] * 3,
        compiler_params=_cparams(("parallel", "parallel", "arbitrary")),
        name="attn_fox" if fox else "attn_diff",
    )(q16, qbias, k16, kbias, v16, lam)


def _alibi_bias(seq_len):
    slopes = 2.0 ** (-8.0 * np.arange(1, HA + 1, dtype=np.float64) / HA)
    pos = np.arange(seq_len)
    hi, lo = (pos // LANES) * float(LANES), (pos % LANES).astype(np.float64)
    qb = np.zeros((HA, seq_len, LANES), np.float32)
    kb = np.zeros((HA, seq_len, LANES), np.float32)
    for h in range(HA):
        qb[h, :, 0], qb[h, :, 1], qb[h, :, 2], qb[h, :, 3] = -slopes[h] * hi, -slopes[h] * lo, 1.0, 1.0
        kb[h, :, 0], kb[h, :, 1], kb[h, :, 2], kb[h, :, 3] = 1.0, 1.0, slopes[h] * hi, slopes[h] * lo
    return jnp.asarray(qb, BF16), jnp.asarray(kb, BF16)


def _decode_kernel(pt_ref, *refs, pages_per_step, n_steps, page):
    del pt_ref
    pp = pages_per_step
    paged = [refs[c * pp:(c + 1) * pp] for c in range(5)]
    (nka_ref, nva_ref, nkb_ref, nvb_ref, nlf_ref, wqa_ref, wqb_ref, biasa_ref, slope_ref,
     maska_ref, maskb_ref, triu_ref, dmaskb_ref, lam_ref) = refs[5 * pp:5 * pp + 14]
    oa_ref, ob_ref = refs[5 * pp + 14:5 * pp + 16]
    (ma_ref, la_ref, acca_ref, mb_ref, lb_ref, accb_ref, carry_ref) = refs[5 * pp + 16:]
    j = pl.program_id(1)
    half_rows = HA * QPAD

    @pl.when(j == 0)
    def _():
        ma_ref[...] = jnp.full(ma_ref.shape, NEG_INF, F32)
        mb_ref[...] = jnp.full(mb_ref.shape, NEG_INF, F32)
        la_ref[...] = jnp.zeros(la_ref.shape, F32)
        lb_ref[...] = jnp.zeros(lb_ref.shape, F32)
        acca_ref[...] = jnp.zeros(acca_ref.shape, F32)
        accb_ref[...] = jnp.zeros(accb_ref.shape, F32)
        carry_ref[...] = jnp.zeros(carry_ref.shape, F32)

    def softmax_step(chunks, m_ref, l_ref):
        m_prev = m_ref[...]
        m_cur = functools.reduce(jnp.maximum, chunks)
        m_new = jnp.maximum(m_prev, jnp.max(m_cur, axis=1, keepdims=True))
        alpha = jnp.exp(m_prev - m_new)
        ps = [jnp.exp(s - m_new) for s in chunks]
        l_ref[...] = alpha * l_ref[...] + functools.reduce(jnp.add, ps)
        m_ref[...] = m_new
        return alpha, ps

    def process(pages):
        chunks = []
        for ka_r, _, _, _, _, page_idx, masked in pages:
            parts = []
            for h in range(HA):
                k_h = ka_r[pl.ds(h, page, stride=HA), :].astype(BF16)
                parts.append(lax.dot_general(wqa_ref[h * 2 * QPAD:(h + 1) * 2 * QPAD, :], k_h, NT_DIMS,
                                             preferred_element_type=F32))
            sa = jnp.concatenate([x[0:QPAD] for x in parts] + [x[QPAD:2 * QPAD] for x in parts], axis=0)
            sa = sa + biasa_ref[...] + slope_ref[...] * (page_idx * float(page))
            if masked:
                sa = jnp.where(maska_ref[...] > 0.0, sa, NEG_INF)
            chunks.append(sa)
        alpha, ps = softmax_step(chunks, ma_ref, la_ref)
        acc = alpha * acca_ref[...]
        for (_, va_r, _, _, _, _, _), p in zip(pages, ps):
            pv = []
            for h in range(HA):
                v_h = va_r[pl.ds(h, page, stride=HA), :].astype(BF16)
                p_h = jnp.concatenate([p[h * QPAD:(h + 1) * QPAD],
                                       p[half_rows + h * QPAD:half_rows + (h + 1) * QPAD]], axis=0)
                pv.append(jnp.dot(p_h.astype(BF16), v_h, preferred_element_type=F32))
            acc = acc + jnp.concatenate([x[0:QPAD] for x in pv] + [x[QPAD:2 * QPAD] for x in pv], axis=0)
        acca_ref[...] = acc
        triu = triu_ref[...]
        carry = carry_ref[...]
        chunks = []
        for _, _, kb_r, _, lf_r, _, masked in pages:
            hi, mid, lo = _split3(lf_r[...])
            cum = (jnp.dot(hi, triu, preferred_element_type=F32)
                   + jnp.dot(mid, triu, preferred_element_type=F32)
                   + jnp.dot(lo, triu, preferred_element_type=F32)) + carry
            carry = jnp.broadcast_to(cum[:, page - 1:page], carry.shape)
            sb = jnp.dot(wqb_ref[...], kb_r[...].astype(BF16), preferred_element_type=F32)
            sb = sb - jnp.concatenate([cum] * 4, axis=0)
            if masked:
                sb = jnp.where(maskb_ref[...] > 0.0, sb, NEG_INF)
            chunks.append(sb)
        carry_ref[...] = carry
        alpha, ps = softmax_step(chunks, mb_ref, lb_ref)
        acc = jnp.concatenate([alpha] * (WB // LANES), axis=1) * accb_ref[...]
        for (_, _, _, vb_r, _, _, _), p in zip(pages, ps):
            acc = acc + lax.dot_general(p.astype(BF16), vb_r[...].astype(BF16), NT_DIMS,
                                        preferred_element_type=F32)
        accb_ref[...] = acc

    cached = [(paged[0][k], paged[1][k], paged[2][k], paged[3][k], paged[4][k],
               (j * pp + k).astype(F32), False) for k in range(pp)]

    @pl.when(j < n_steps - 1)
    def _():
        process(cached)

    @pl.when(j == n_steps - 1)
    def _():
        process(cached + [(nka_ref, nva_ref, nkb_ref, nvb_ref, nlf_ref, float(n_steps * pp), True)])
        oa = acca_ref[...] / jnp.sum(la_ref[...], axis=1, keepdims=True)
        oa_ref[...] = oa[0:half_rows] - lam_ref[0:1, :] * oa[half_rows:2 * half_rows]
        ob = accb_ref[...] / jnp.sum(lb_ref[...], axis=1, keepdims=True) * dmaskb_ref[...]
        ob_ref[...] = ob[:, 0:128] + ob[:, 128:256] + ob[:, 256:384] + ob[:, 384:512]


def _decode_attention(page_table, caches, new_pages, wqa, wqb, lam, new_tokens, pages_per_step):
    dec_batch, n_pages = page_table.shape
    page = caches[4].shape[2]
    pp = pages_per_step
    n_steps = n_pages // pp
    past = n_pages * page
    rows_a = 2 * HA * QPAD
    rows_b = new_tokens * HB
    slopes = 2.0 ** (-8.0 * np.arange(1, HA + 1, dtype=np.float64) / HA)
    ra = np.arange(rows_a)
    ha, qa_i = (ra // QPAD) % HA, np.minimum(ra % QPAD, new_tokens - 1)
    col = np.arange(page)
    biasa = slopes[ha][:, None] * (col[None, :] - past - qa_i[:, None])
    slope_rep = np.repeat(slopes[ha][:, None], page, axis=1)
    maska = (col[None, :] <= qa_i[:, None]).astype(np.float32)
    rb = np.arange(rows_b)
    maskb = (col[None, :] <= (rb // HB)[:, None]).astype(np.float32)
    triu = (np.arange(page)[:, None] <= np.arange(page)[None, :]).astype(np.float32)
    cb = np.arange(WB)
    dmaskb = (cb[None, :] // DB == (rb % HB)[:, None]).astype(np.float32)
    consts = [jnp.asarray(biasa, F32), jnp.asarray(slope_rep, F32), jnp.asarray(maska),
              jnp.asarray(maskb), jnp.asarray(triu, BF16), jnp.asarray(dmaskb), lam]

    def page_spec(shape, k):
        return pl.BlockSpec((None,) + shape,
                            lambda b, j, pt: (pt[b * n_pages + j * pp + k], 0, 0))

    in_specs, args = [], []
    for c in caches:
        for k in range(pp):
            in_specs.append(page_spec(c.shape[1:], k))
            args.append(c)
    for a in new_pages + [wqa, wqb]:
        in_specs.append(pl.BlockSpec((None,) + a.shape[1:], lambda b, j, pt: (b, 0, 0)))
        args.append(a)
    for a in consts:
        in_specs.append(pl.BlockSpec(a.shape, lambda b, j, pt: (0, 0)))
        args.append(a)
    grid_spec = pltpu.PrefetchScalarGridSpec(
        num_scalar_prefetch=1,
        grid=(dec_batch, n_steps),
        in_specs=in_specs,
        out_specs=[pl.BlockSpec((None, rows_a // 2, LANES), lambda b, j, pt: (b, 0, 0)),
                   pl.BlockSpec((None, rows_b, LANES), lambda b, j, pt: (b, 0, 0))],
        scratch_shapes=[pltpu.VMEM((rows_a, LANES), F32), pltpu.VMEM((rows_a, LANES), F32),
                        pltpu.VMEM((rows_a, DVA), F32),
                        pltpu.VMEM((rows_b, LANES), F32), pltpu.VMEM((rows_b, LANES), F32),
                        pltpu.VMEM((rows_b, WB), F32),
                        pltpu.VMEM((HB, page), F32)],
    )
    return pl.pallas_call(
        functools.partial(_decode_kernel, pages_per_step=pp, n_steps=n_steps, page=page),
        grid_spec=grid_spec,
        out_shape=[jax.ShapeDtypeStruct((dec_batch, rows_a // 2, LANES), F32),
                   jax.ShapeDtypeStruct((dec_batch, rows_b, LANES), F32)],
        compiler_params=_cparams(("parallel", "arbitrary")),
        name="decode_attn",
    )(page_table.reshape(-1), *args)


def _merge_kernel(oa_ref, ob_ref, x_ref, g1_ref, sh2_ref, sc2_ref, sub_ref, wo_ref, n2_ref,
                  xp_ref, h2_ref, *, out_scale):
    oa = oa_ref[...]
    parts = []
    for h in range(HA):
        o = oa[:, h * DVA:(h + 1) * DVA]
        ms = jnp.mean(o * o, axis=-1, keepdims=True)
        parts.append(o * lax.rsqrt(ms + EPS) * sub_ref[...] * out_scale)
    o = jnp.concatenate(parts + [ob_ref[...]], axis=1).astype(BF16)
    y = jnp.dot(o, wo_ref[...], preferred_element_type=F32)
    xp = x_ref[...] + g1_ref[...] * y
    xp_ref[...] = xp
    ms = jnp.mean(xp * xp, axis=-1, keepdims=True)
    h2 = xp * lax.rsqrt(ms + EPS) * n2_ref[...]
    h2_ref[...] = h2 * (1.0 + sc2_ref[...]) + sh2_ref[...]


def _merge(oa, ob, x2d, g1, sh2, sc2, sub_g, w_out16, n2_g, per_row, rows_per_mod, tm, out_scale):
    n = x2d.shape[0]
    mod = _mod_spec(per_row, tm, rows_per_mod)
    row = lambda w: pl.BlockSpec((tm, w), lambda i: (i, 0))
    const = lambda a: pl.BlockSpec(a.shape, lambda i: (0,) * a.ndim)
    return pl.pallas_call(
        functools.partial(_merge_kernel, out_scale=out_scale),
        grid=(n // tm,),
        in_specs=[row(WA), row(WB), row(D_MODEL), mod, mod, mod, const(sub_g), const(w_out16),
                  const(n2_g)],
        out_specs=[row(D_MODEL), row(D_MODEL)],
        out_shape=[jax.ShapeDtypeStruct((n, D_MODEL), F32)] * 2,
        compiler_params=_cparams(("parallel",)),
        name="merge",
    )(oa, ob, x2d, g1, sh2, sc2, sub_g, w_out16, n2_g)


def _topk_kernel(h_ref, wq_ref, sk_ref, idx_ref, gate_ref, s_ref, ts_ref, ti_ref, be_ref, bg_ref, *, tb):
    hb = h_ref[...].astype(BF16)
    qt = lax.dot_general(wq_ref[...], hb, NT_DIMS, preferred_element_type=F32)
    for hp in range(2 * PEER_HEADS):
        sub = sk_ref[hp % 2]
        s_ref[hp] = jnp.dot(sub, qt[hp * N_KEYS:(hp + 1) * N_KEYS, :].astype(BF16),
                            preferred_element_type=F32)
    row_k = lax.broadcasted_iota(I32, (N_KEYS, tb), 0).astype(F32)

    def stage1(hp, carry):
        s = s_ref[hp]
        for r in range(PEER_TOPK):
            m = jnp.max(s, axis=0, keepdims=True)
            pick = jnp.min(jnp.where(s == m, row_k, float(N_KEYS)), axis=0, keepdims=True)
            ts_ref[hp, r:r + 1, :] = m
            ti_ref[hp, r:r + 1, :] = pick
            s = jnp.where(row_k == pick, NEG_INF, s)
        return carry

    lax.fori_loop(0, 2 * PEER_HEADS, stage1, 0)
    half = PEER_TOPK // 2
    n_cand = PEER_TOPK + (half - 1) * half + half
    row_c = lax.broadcasted_iota(I32, (n_cand, tb), 0).astype(F32)

    def candidates(x, y, combine):
        rows = [combine(x[0:1, :], y)]
        rows += [combine(x[i:i + 1, :], y[0:half, :]) for i in range(1, half)]
        rows.append(combine(x[half:PEER_TOPK, :], y[0:1, :]))
        return jnp.concatenate(rows, axis=0)

    def stage2(h, carry):
        a, b = ts_ref[2 * h], ts_ref[2 * h + 1]
        ia, ib = ti_ref[2 * h], ti_ref[2 * h + 1]
        cs = candidates(a, b, lambda x, y: x + y)
        ce = candidates(ia, ib, lambda x, y: x * float(N_KEYS) + y)
        best, experts = [], []
        for r in range(PEER_TOPK):
            m = jnp.max(cs, axis=0, keepdims=True)
            pick = jnp.min(jnp.where(cs == m, row_c, float(n_cand)), axis=0, keepdims=True)
            sel = row_c == pick
            experts.append(jnp.max(jnp.where(sel, ce, -1.0), axis=0, keepdims=True))
            best.append(m)
            cs = jnp.where(sel, NEG_INF, cs)
        bs = jnp.concatenate(best, axis=0)
        ex = jnp.exp(bs - bs[0:1, :])
        off = pl.multiple_of(h * PEER_TOPK, PEER_TOPK)
        bg_ref[pl.ds(off, PEER_TOPK), :] = ex / jnp.sum(ex, axis=0, keepdims=True)
        be_ref[pl.ds(off, PEER_TOPK), :] = jnp.concatenate(experts, axis=0)
        return carry

    lax.fori_loop(0, PEER_HEADS, stage2, 0)
    idx_ref[...] = (be_ref[...].T * float(ROW_TILE)).astype(I32)
    gate_ref[...] = bg_ref[...].T


def _topk(h2, wq_t16, subkeys16, tb):
    n = h2.shape[0]
    const = lambda a: pl.BlockSpec(a.shape, lambda i: (0,) * a.ndim)
    return pl.pallas_call(
        functools.partial(_topk_kernel, tb=tb),
        grid=(n // tb,),
        in_specs=[pl.BlockSpec((tb, D_MODEL), lambda i: (i, 0)), const(wq_t16), const(subkeys16)],
        out_specs=[pl.BlockSpec((tb, N_SLOTS), lambda i: (i, 0))] * 2,
        out_shape=[jax.ShapeDtypeStruct((n, N_SLOTS), I32), jax.ShapeDtypeStruct((n, N_SLOTS), F32)],
        scratch_shapes=[pltpu.VMEM((2 * PEER_HEADS, N_KEYS, tb), F32),
                        pltpu.VMEM((2 * PEER_HEADS, PEER_TOPK, tb), F32),
                        pltpu.VMEM((2 * PEER_HEADS, PEER_TOPK, tb), F32),
                        pltpu.VMEM((N_SLOTS, tb), F32), pltpu.VMEM((N_SLOTS, tb), F32)],
        compiler_params=_cparams(("parallel",)),
        name="peer_topk",
    )(h2, wq_t16, subkeys16)


def _pack_table(t):
    n_e, d = t.shape
    bits = lax.bitcast_convert_type(t.astype(BF16), jnp.uint16).astype(jnp.uint32)
    word = bits[:, :d // 2] | (bits[:, d // 2:] << 16)
    return lax.bitcast_convert_type(word, I32).reshape(n_e * ROW_TILE, LANES)


def _peer_u_kernel(idx_ref, h_ref, gate_ref, tbl_ref, mask_ref, sel_ref, w_ref, pair_ref, *stage_refs, tb):
    buf_a, buf_b, buf_c, buf_d = stage_refs

    def gather(t, stage):
        for e in range(N_SLOTS):
            row = pl.multiple_of(idx_ref[t, e], ROW_TILE)
            stage[e * ROW_TILE:(e + 1) * ROW_TILE, :] = tbl_ref[pl.ds(row, ROW_TILE), :]

    def contract(t, stage):
        h3 = jnp.concatenate(_split3(h_ref[t]), axis=0)
        acc = jnp.zeros((3 * SUBLANES, 2 * N_SLOTS), F32)
        for c in range(ROW_TILE):
            word = stage[pl.ds(c, N_SLOTS, stride=ROW_TILE), :]
            rhs = pltpu.bitcast(word, BF16)
            acc = acc + mask_ref[c] * lax.dot_general(h3, rhs, NT_DIMS, preferred_element_type=F32)
        pair_ref[pl.ds(t, 1), :] = jnp.sum(acc, axis=0, keepdims=True)

    for buf in (buf_c, buf_d):
        buf[...] = jnp.zeros(buf.shape, I32)

    def pair(t, fill, drain):
        gather(t, fill[0])
        gather(t + 1, fill[1])
        contract(jnp.maximum(t - 2, 0), drain[0])
        contract(jnp.maximum(t - 1, 0), drain[1])

    def tokens(i, carry):
        for k in range(PAIRS_PER_TRIP):
            t = 2 * (PAIRS_PER_TRIP * i + k)
            if k % 2 == 0:
                pair(t, (buf_a, buf_b), (buf_c, buf_d))
            else:
                pair(t, (buf_c, buf_d), (buf_a, buf_b))
        return carry

    lax.fori_loop(0, tb // (2 * PAIRS_PER_TRIP), tokens, 0)
    contract(tb - 2, buf_c)
    contract(tb - 1, buf_d)
    a = sum(jnp.dot(x, sel_ref[...], preferred_element_type=F32) for x in _split3(pair_ref[...]))
    w_ref[...] = gate_ref[...] * (0.5 * a * (1.0 + lax.erf(a * (2.0 ** -0.5))))


def _chunk_masks():
    m = np.zeros((ROW_TILE, 3 * SUBLANES, 2 * N_SLOTS), np.float32)
    for c in range(ROW_TILE):
        for term in range(3):
            m[c, term * SUBLANES + c, 0::2] = 1.0
            m[c, term * SUBLANES + ROW_TILE + c, 1::2] = 1.0
    sel = np.zeros((2 * N_SLOTS, N_SLOTS), np.float32)
    sel[np.arange(2 * N_SLOTS), np.arange(2 * N_SLOTS) // 2] = 1.0
    return jnp.asarray(m), jnp.asarray(sel, BF16)


def _peer_u(idx, h2r, gate, tbl, tb):
    n = idx.shape[0]
    masks, sel = _chunk_masks()
    return pl.pallas_call(
        functools.partial(_peer_u_kernel, tb=tb),
        grid=(n // tb,),
        in_specs=[pl.BlockSpec((tb, N_SLOTS), lambda i: (i, 0), memory_space=pltpu.SMEM),
                  pl.BlockSpec((tb, SUBLANES, LANES), lambda i: (i, 0, 0)),
                  pl.BlockSpec((tb, N_SLOTS), lambda i: (i, 0)),
                  pl.BlockSpec(memory_space=pltpu.VMEM),
                  pl.BlockSpec(masks.shape, lambda i: (0, 0, 0)),
                  pl.BlockSpec(sel.shape, lambda i: (0, 0))],
        out_specs=pl.BlockSpec((tb, N_SLOTS), lambda i: (i, 0)),
        out_shape=jax.ShapeDtypeStruct((n, N_SLOTS), F32),
        scratch_shapes=[pltpu.VMEM((tb, 2 * N_SLOTS), F32)]
        + [pltpu.VMEM((N_SLOTS * ROW_TILE, LANES), I32)] * 4,
        compiler_params=_cparams(("arbitrary",)),
        name="peer_u",
    )(idx, h2r, gate, tbl, masks, sel)


def _peer_v_kernel(idx_ref, w_ref, tbl_ref, elo_ref, ehi_ref, o_ref, *scratch_refs, tb):
    wexp_refs = scratch_refs[0:2]
    buf_a, buf_b, buf_c, buf_d = scratch_refs[2:6]
    n_terms = 3

    for half_ref in wexp_refs:
        half_ref[...] = jnp.zeros(half_ref.shape, F32)
    for k, term in enumerate(_split3(w_ref[...])):
        for j, spread_ref in enumerate((elo_ref, ehi_ref)):
            spread = jnp.dot(term, spread_ref[...], preferred_element_type=F32)
            for half, half_ref in enumerate(wexp_refs):
                half_ref[pl.ds(j * n_terms + k, tb, stride=SUBLANES), :] = (
                    spread[:, half * LANES:(half + 1) * LANES])

    def gather(t, stage):
        for e in range(N_SLOTS):
            row = pl.multiple_of(idx_ref[t, e], ROW_TILE)
            stage[e * ROW_TILE:(e + 1) * ROW_TILE, :] = tbl_ref[pl.ds(row, ROW_TILE), :]

    def contract(t, stage):
        rows = pl.ds(pl.multiple_of(t * SUBLANES, SUBLANES), SUBLANES)
        lhs = jnp.concatenate([half_ref[rows, :] for half_ref in wexp_refs], axis=1).astype(BF16)
        for c in range(ROW_TILE):
            rhs = pltpu.bitcast(stage[pl.ds(c, N_SLOTS, stride=ROW_TILE), :], BF16)
            r = jnp.dot(lhs, rhs, preferred_element_type=F32)
            o_ref[t, c:c + 1, :] = r[0:1] + r[1:2] + r[2:3]
            o_ref[t, ROW_TILE + c:ROW_TILE + c + 1, :] = r[3:4] + r[4:5] + r[5:6]

    for buf in (buf_c, buf_d):
        buf[...] = jnp.zeros(buf.shape, I32)

    def pair(t, fill, drain):
        gather(t, fill[0])
        gather(t + 1, fill[1])
        contract(jnp.maximum(t - 2, 0), drain[0])
        contract(jnp.maximum(t - 1, 0), drain[1])

    def tokens(i, carry):
        for k in range(PAIRS_PER_TRIP):
            t = 2 * (PAIRS_PER_TRIP * i + k)
            if k % 2 == 0:
                pair(t, (buf_a, buf_b), (buf_c, buf_d))
            else:
                pair(t, (buf_c, buf_d), (buf_a, buf_b))
        return carry

    lax.fori_loop(0, tb // (2 * PAIRS_PER_TRIP), tokens, 0)
    contract(tb - 2, buf_c)
    contract(tb - 1, buf_d)


def _half_spreads():
    lo = np.zeros((N_SLOTS, 2 * N_SLOTS), np.float32)
    hi = np.zeros((N_SLOTS, 2 * N_SLOTS), np.float32)
    lo[np.arange(N_SLOTS), 2 * np.arange(N_SLOTS)] = 1.0
    hi[np.arange(N_SLOTS), 2 * np.arange(N_SLOTS) + 1] = 1.0
    return jnp.asarray(lo, BF16), jnp.asarray(hi, BF16)


def _peer_v(idx, w, tbl, tb):
    n = idx.shape[0]
    elo, ehi = _half_spreads()
    blk = pl.BlockSpec((tb, N_SLOTS), lambda i: (i, 0))
    return pl.pallas_call(
        functools.partial(_peer_v_kernel, tb=tb),
        grid=(n // tb,),
        in_specs=[pl.BlockSpec((tb, N_SLOTS), lambda i: (i, 0), memory_space=pltpu.SMEM), blk,
                  pl.BlockSpec(memory_space=pltpu.VMEM),
                  pl.BlockSpec(elo.shape, lambda i: (0, 0)), pl.BlockSpec(ehi.shape, lambda i: (0, 0))],
        out_specs=pl.BlockSpec((tb, SUBLANES, LANES), lambda i: (i, 0, 0)),
        out_shape=jax.ShapeDtypeStruct((n, SUBLANES, LANES), F32),
        scratch_shapes=[pltpu.VMEM((tb * SUBLANES, LANES), F32)] * 2
        + [pltpu.VMEM((N_SLOTS * ROW_TILE, LANES), I32)] * 4,
        compiler_params=_cparams(("arbitrary",)),
        name="peer_v",
    )(idx, w, tbl, elo, ehi)


def _final_kernel(xp_ref, po_ref, g2_ref, fg_ref, y_ref):
    x = xp_ref[...] + g2_ref[...] * po_ref[...]
    ms = jnp.mean(x * x, axis=-1, keepdims=True)
    y_ref[...] = x * lax.rsqrt(ms + EPS) * fg_ref[...]


def _final(xp, po, g2, fg, per_row, rows_per_mod, tm):
    n = xp.shape[0]
    row = pl.BlockSpec((tm, D_MODEL), lambda i: (i, 0))
    return pl.pallas_call(
        _final_kernel,
        grid=(n // tm,),
        in_specs=[row, row, _mod_spec(per_row, tm, rows_per_mod),
                  pl.BlockSpec((1, D_MODEL), lambda i: (0, 0))],
        out_specs=row,
        out_shape=jax.ShapeDtypeStruct((n, D_MODEL), F32),
        compiler_params=_cparams(("parallel",)),
        name="final_norm",
    )(xp, po, g2, fg)


def _peer(h2, xp, g2, fg, wq_t16, subkeys16, u_tbl, v_tbl, per_row, rows_per_mod, tm, tb_topk, tb_gather):
    n = h2.shape[0]
    idx, gate = _topk(h2, wq_t16, subkeys16, tb_topk)
    w = _peer_u(idx, h2.reshape(n, SUBLANES, LANES), gate, u_tbl, tb_gather)
    po = _peer_v(idx, w, v_tbl, tb_gather).reshape(n, D_MODEL)
    return _final(xp, po, g2, fg, per_row, rows_per_mod, tm)


def kernel(x_prompt, x_sample, c_prompt, c_sample, cache_a_k, cache_a_v, cache_b_k, cache_b_v, cache_b_logf, page_table, w_ada, b_ada, norm1_g, w_in, b_f, lambda_q1, lambda_k1, lambda_q2, lambda_k2, subln_g, w_out, norm2_g, peer_wq, peer_subkeys, peer_u, peer_v, final_g):
    batch, seq_len, d = x_prompt.shape
    dec_batch, new_tokens, _ = x_sample.shape
    depth = w_ada.shape[0]
    assert depth == 1 and d == D_MODEL and new_tokens == 4
    n_p, n_s = batch * seq_len, dec_batch * new_tokens
    n_pool, page = cache_a_k.shape[1], cache_a_k.shape[2]
    l = 0
    lam_init = 0.8 - 0.6 * math.exp(-0.3 * l)

    n_c = batch + dec_batch
    pad = (-n_c) % SUBLANES
    c_all = jnp.concatenate([c_prompt, c_sample, jnp.zeros((pad, d), F32)], axis=0)
    ada = _ada(c_all, w_ada[l], b_ada[l])
    mods_p = [m.reshape(batch, 1, d) for m in jnp.split(ada[:batch], 6, axis=-1)]
    mods_s = [jnp.repeat(m, new_tokens, axis=0) for m in jnp.split(ada[batch:n_c], 6, axis=-1)]
    lam = _lam(lambda_q1[l], lambda_k1[l], lambda_q2[l], lambda_k2[l], lam_init)

    n_main = 3 * WA + 3 * WB
    w_main = w_in[l][:, :n_main].astype(BF16)
    w_f = jnp.pad(w_in[l][:, n_main:], ((0, 0), (0, LANES - HB))).astype(BF16)
    bf = jnp.pad(b_f[l], (0, LANES - HB)).reshape(1, LANES)
    g1n = norm1_g[l].reshape(1, d)
    g2n = norm2_g[l].reshape(1, d)
    sub_g = subln_g[l].reshape(1, DVA)
    w_out16 = w_out[l].astype(BF16)
    wq_t16 = peer_wq[l].T.astype(BF16)
    subkeys16 = peer_subkeys[l].astype(BF16)
    u_tbl = _pack_table(peer_u[l])
    v_tbl = _pack_table(peer_v[l])
    fg = final_g.reshape(1, d)

    tm_p = 256
    xp2d = x_prompt.reshape(n_p, d)
    sh1, sc1, g1, sh2, sc2, g2 = mods_p
    (ka, va, kb, vb, lf, qa16, ka16, va16, qb16, kb16, vb16, qbias, kbias) = _proj(
        xp2d, sh1, sc1, g1n, w_main, w_f, bf, False, seq_len, seq_len, tm_p)
    qbias_a, kbias_a = _alibi_bias(seq_len)
    tq = 1024
    oa = _attention(qa16, qbias_a, ka16, kbias_a, va16, lam, batch, seq_len, tq, fox=False)
    ob = _attention(qb16, qbias, kb16, kbias, vb16, lam, batch, seq_len, tq, fox=True)
    xp_mid, h2 = _merge(oa, ob, xp2d, g1, sh2, sc2, sub_g, w_out16, g2n, False, seq_len, tm_p,
                        1.0 - lam_init)
    y_prompt = _peer(h2, xp_mid, g2, fg, wq_t16, subkeys16, u_tbl, v_tbl, False, seq_len, tm_p,
                     256, 64).reshape(batch, seq_len, d)

    tm_s = n_s
    xs2d = x_sample.reshape(n_s, d)
    sh1, sc1, g1, sh2, sc2, g2 = mods_s
    (ka_s, va_s, kb_s, vb_s, lf_s, qa16_s, _, _, qb16_s, _, _, _, _) = _proj(
        xs2d, sh1, sc1, g1n, w_main, w_f, bf, True, n_s, n_s, tm_s)
    assert page == LANES
    qa5 = qa16_s.reshape(dec_batch, new_tokens, HA, 2, DA)
    zeros = jnp.zeros_like(qa5[:, :, :, 0, :])
    wqa = jnp.stack([jnp.concatenate([qa5[:, :, :, 0, :], zeros], axis=-1),
                     jnp.concatenate([zeros, qa5[:, :, :, 1, :]], axis=-1)], axis=3)
    wqa = jnp.pad(wqa, ((0, 0), (0, QPAD - new_tokens), (0, 0), (0, 0), (0, 0)))
    wqa = wqa.transpose(0, 2, 3, 1, 4).reshape(dec_batch, HA * 2 * QPAD, 2 * DA)
    qb4 = qb16_s.reshape(dec_batch, new_tokens, HB, DB)
    wqb = (qb4[:, :, :, None, :] * jnp.eye(HB, dtype=BF16)[None, None, :, :, None])
    wqb = wqb.reshape(dec_batch, new_tokens * HB, WB)

    def pos_major(c):
        return c.reshape(n_pool, page * HA, c.shape[-1])

    def pos_minor(c):
        return c.transpose(0, 2, 3, 1).reshape(n_pool, HB * DB, page)

    caches = [pos_major(cache_a_k[l]), pos_major(cache_a_v[l]), pos_minor(cache_b_k[l]),
              pos_minor(cache_b_v[l]), cache_b_logf[l].transpose(0, 2, 1)]

    def new_pos_major(a):
        a = a.reshape(dec_batch, new_tokens * HA, a.shape[1] // HA)
        return jnp.pad(a, ((0, 0), (0, (page - new_tokens) * HA), (0, 0)))

    def new_pos_minor(a, heads):
        w = a.shape[1] // heads
        a = a.reshape(dec_batch, new_tokens, heads, w).transpose(0, 2, 3, 1)
        return jnp.pad(a, ((0, 0), (0, 0), (0, 0), (0, page - new_tokens))).reshape(dec_batch, heads * w, page)

    new_pages = [new_pos_major(ka_s), new_pos_major(va_s), new_pos_minor(kb_s, HB),
                 new_pos_minor(vb_s, HB), new_pos_minor(lf_s, HB)]
    oa_s, ob_s = _decode_attention(page_table, caches, new_pages, wqa, wqb, lam, new_tokens, 8)
    oa_s = oa_s.reshape(dec_batch, HA, QPAD, DVA)[:, :, :new_tokens].transpose(0, 2, 1, 3).reshape(n_s, WA)
    ob_s = ob_s.reshape(dec_batch, new_tokens, HB, 2, DB)
    ob_s = jnp.where((jnp.arange(HB) % 2 == 0)[None, None, :, None], ob_s[:, :, :, 0, :], ob_s[:, :, :, 1, :])
    ob_s = ob_s.reshape(n_s, WB)
    xs_mid, h2_s = _merge(oa_s, ob_s, xs2d, g1, sh2, sc2, sub_g, w_out16, g2n, True, n_s, tm_s,
                          1.0 - lam_init)
    y_sample = _peer(h2_s, xs_mid, g2, fg, wq_t16, subkeys16, u_tbl, v_tbl, True, n_s, tm_s,
                     n_s, 64).reshape(dec_batch, new_tokens, d)

    def kv(a, b, t, h, w):
        return a.reshape(1, b, t, h, w)

    return (y_prompt, y_sample,
            kv(ka, batch, seq_len, HA, 2 * DA), kv(va, batch, seq_len, HA, DVA),
            kv(kb, batch, seq_len, HB, DB), kv(vb, batch, seq_len, HB, DB),
            lf.reshape(1, batch, seq_len, HB),
            kv(ka_s, dec_batch, new_tokens, HA, 2 * DA), kv(va_s, dec_batch, new_tokens, HA, DVA),
            kv(kb_s, dec_batch, new_tokens, HB, DB), kv(vb_s, dec_batch, new_tokens, HB, DB),
            lf_s.reshape(1, dec_batch, new_tokens, HB))
```

```python
import functools
import math

import jax
import jax.numpy as jnp
import numpy as np
from jax import lax
from jax.experimental import pallas as pl
from jax.experimental.pallas import tpu as pltpu

F32, BF16, I32 = jnp.float32, jnp.bfloat16, jnp.int32
EPS = 1e-6
D_MODEL = 1024
HA, DA, DVA = 4, 64, 128
HB, DB = 8, 64
WA, WB = HA * DVA, HB * DB
N_KEYS = 128
PEER_HEADS = 8
PEER_TOPK = 16
N_SLOTS = PEER_HEADS * PEER_TOPK
N_EXPERTS = N_KEYS * N_KEYS
LANES = 128
SUBLANES = 8
ROW_TILE = D_MODEL // (2 * LANES)
QPAD = SUBLANES
PAIRS_PER_TRIP = 16
NEG_INF = float("-inf")
VMEM_LIMIT = 56 * 1024 * 1024
NT_DIMS = (((1,), (1,)), ((), ()))


def _cparams(sem):
    return pltpu.CompilerParams(dimension_semantics=sem, vmem_limit_bytes=VMEM_LIMIT)


def _split3(x):
    hi = x.astype(BF16)
    r = x - hi.astype(F32)
    mid = r.astype(BF16)
    lo = (r - mid.astype(F32)).astype(BF16)
    return hi, mid, lo


def _ada_kernel(c_ref, w_ref, b_ref, o_ref):
    c = c_ref[...]
    s = c * (1.0 / (1.0 + jnp.exp(-c)))
    o_ref[...] = jnp.dot(s.astype(BF16), w_ref[...].astype(BF16),
                         preferred_element_type=F32) + b_ref[...]


def _ada(c_all, w, b):
    r = c_all.shape[0]
    n_out = w.shape[1]
    return pl.pallas_call(
        _ada_kernel,
        grid=(n_out // D_MODEL,),
        in_specs=[pl.BlockSpec((r, D_MODEL), lambda j: (0, 0)),
                  pl.BlockSpec((D_MODEL, D_MODEL), lambda j: (0, j)),
                  pl.BlockSpec((1, D_MODEL), lambda j: (0, j))],
        out_specs=pl.BlockSpec((r, D_MODEL), lambda j: (0, j)),
        out_shape=jax.ShapeDtypeStruct((r, n_out), F32),
        compiler_params=_cparams(("parallel",)),
        name="ada",
    )(c_all, w, b.reshape(1, n_out))


def _lam_kernel(v_ref, o_ref, *, lam_init):
    v = v_ref[...]
    a = jnp.sum(v[0:1] * v[1:2], axis=1, keepdims=True)
    b = jnp.sum(v[2:3] * v[3:4], axis=1, keepdims=True)
    lam = jnp.exp(a) - jnp.exp(b) + lam_init
    o_ref[...] = jnp.broadcast_to(lam, o_ref.shape)


def _lam(lq1, lk1, lq2, lk2, lam_init):
    v = jnp.zeros((SUBLANES, LANES), F32)
    v = v.at[0:4, 0:DA].set(jnp.stack([lq1, lk1, lq2, lk2]))
    return pl.pallas_call(
        functools.partial(_lam_kernel, lam_init=lam_init),
        out_shape=jax.ShapeDtypeStruct((SUBLANES, LANES), F32),
        name="lam",
    )(v)


def _mod_spec(per_row, tm, rows_per_mod):
    if per_row:
        return pl.BlockSpec((tm, D_MODEL), lambda i: (i, 0))
    blocks = rows_per_mod // tm
    return pl.BlockSpec((None, 1, D_MODEL), lambda i: (i // blocks, 0, 0))


def _proj_kernel(x_ref, sh_ref, sc_ref, g_ref, w_ref, wf_ref, bf_ref, tri_ref, pq_ref, pk_ref,
                 qc_ref, kc_ref,
                 ka_ref, va_ref, kb_ref, vb_ref, lf_ref,
                 qa16_ref, ka16_ref, va16_ref, qb16_ref, kb16_ref, vb16_ref, qbias_ref, kbias_ref,
                 carry_ref, *, blocks_per_seq):
    i = pl.program_id(0)

    @pl.when(i % blocks_per_seq == 0)
    def _():
        carry_ref[...] = jnp.zeros_like(carry_ref)

    x = x_ref[...]
    ms = jnp.mean(x * x, axis=-1, keepdims=True)
    h = x * lax.rsqrt(ms + EPS) * g_ref[...]
    h = h * (1.0 + sc_ref[...]) + sh_ref[...]
    hb = h.astype(BF16)
    p = jnp.dot(hb, w_ref[...], preferred_element_type=F32)
    qa, ka, va = p[:, 0:WA], p[:, WA:2 * WA], p[:, 2 * WA:3 * WA]
    o = 3 * WA
    qb, kb, vb = p[:, o:o + WB], p[:, o + WB:o + 2 * WB], p[:, o + 2 * WB:o + 3 * WB]
    ka_ref[...] = ka
    va_ref[...] = va
    kb_ref[...] = kb
    vb_ref[...] = vb
    qa16_ref[...] = (qa * (DA ** -0.5)).astype(BF16)
    ka16_ref[...] = ka.astype(BF16)
    va16_ref[...] = va.astype(BF16)
    qb16_ref[...] = (qb * (DB ** -0.5)).astype(BF16)
    kb16_ref[...] = kb.astype(BF16)
    vb16_ref[...] = vb.astype(BF16)

    z = jnp.dot(hb, wf_ref[...], preferred_element_type=F32) + bf_ref[...]
    lf = jnp.minimum(z, 0.0) - jnp.log1p(jnp.exp(-jnp.abs(z)))
    lane = lax.broadcasted_iota(I32, lf.shape, 1)
    lf = jnp.where(lane < HB, lf, 0.0)
    lf_ref[...] = lf[:, 0:HB]

    tri = tri_ref[...]
    hi, mid, lo = _split3(lf)
    cum = (jnp.dot(tri, hi, preferred_element_type=F32)
           + jnp.dot(tri, mid, preferred_element_type=F32)
           + jnp.dot(tri, lo, preferred_element_type=F32)) + carry_ref[...]
    tm = cum.shape[0]
    carry_ref[...] = cum[tm - 1:tm, :]
    ccat = jnp.concatenate(_split3(cum), axis=1)
    qbias_ref[...] = (jnp.dot(ccat, pq_ref[...], preferred_element_type=F32) + qc_ref[...]).astype(BF16)
    kbias_ref[...] = (jnp.dot(ccat, pk_ref[...], preferred_element_type=F32) + kc_ref[...]).astype(BF16)


def _decay_placement():
    pq = np.zeros((3 * LANES, WB), np.float32)
    pk = np.zeros((3 * LANES, WB), np.float32)
    qc = np.zeros((1, WB), np.float32)
    kc = np.zeros((1, WB), np.float32)
    for h in range(HB):
        pair, which = divmod(h, 2)
        base = pair * LANES + which * 6
        for t in range(3):
            pq[t * LANES + h, base + t] = 1.0
            qc[0, base + 3 + t] = 1.0
            kc[0, base + t] = 1.0
            pk[t * LANES + h, base + 3 + t] = -1.0
    return (jnp.asarray(pq, BF16), jnp.asarray(pk, BF16), jnp.asarray(qc), jnp.asarray(kc))


def _proj(x2d, sh, sc, g, w_main, w_f, b_f, per_row, rows_per_mod, seq_len, tm):
    n = x2d.shape[0]
    nw = w_main.shape[1]
    tri = (np.arange(tm)[None, :] <= np.arange(tm)[:, None]).astype(np.float32)
    pq, pk, qc, kc = _decay_placement()
    mod = _mod_spec(per_row, tm, rows_per_mod)
    row = lambda w: pl.BlockSpec((tm, w), lambda i: (i, 0))
    const = lambda a: pl.BlockSpec(a.shape, lambda i: (0,) * a.ndim)
    tri = jnp.asarray(tri, BF16)
    outs = ([jax.ShapeDtypeStruct((n, WA), F32)] * 4 + [jax.ShapeDtypeStruct((n, HB), F32)]
            + [jax.ShapeDtypeStruct((n, WA), BF16)] * 8)
    return pl.pallas_call(
        functools.partial(_proj_kernel, blocks_per_seq=seq_len // tm),
        grid=(n // tm,),
        in_specs=[row(D_MODEL), mod, mod, const(g), const(w_main), const(w_f), const(b_f),
                  const(tri), const(pq), const(pk), const(qc), const(kc)],
        out_specs=[row(WA)] * 4 + [row(HB)] + [row(WA)] * 8,
        out_shape=outs,
        scratch_shapes=[pltpu.VMEM((1, LANES), F32)],
        compiler_params=_cparams(("arbitrary",)),
        name="proj",
    )(x2d, sh, sc, g, w_main, w_f, b_f, tri, pq, pk, qc, kc)


def _attn_kernel(q_ref, qb_ref, k_ref, kb_ref, v_ref, lam_ref, o_ref, m_ref, l_ref, acc_ref,
                 *, tq, fox):
    qi = pl.program_id(2)
    q = q_ref[...].astype(F32)
    qb = qb_ref[...].astype(F32)
    lane = lax.broadcasted_iota(I32, (tq, LANES), 1)
    q_lo = jnp.where(lane < DA, q, 0.0)
    q_hi = jnp.where(lane >= DA, q, 0.0)
    if fox:
        qb_lo = jnp.where(lane < 6, qb, 0.0)
        qb_hi = jnp.where((lane >= 6) & (lane < 12), qb, 0.0)
    else:
        qb_lo = qb_hi = qb
    qq = jnp.concatenate([jnp.concatenate([q_lo, qb_lo], axis=1),
                          jnp.concatenate([q_hi, qb_hi], axis=1)], axis=0).astype(BF16)
    m_ref[...] = jnp.full(m_ref.shape, NEG_INF, F32)
    l_ref[...] = jnp.zeros(l_ref.shape, F32)
    acc_ref[...] = jnp.zeros(acc_ref.shape, F32)

    n_chunks = tq // LANES

    def step(j, masked):
        off = pl.multiple_of(j * tq, tq)
        kk = jnp.concatenate([k_ref[pl.ds(off, tq), :], kb_ref[pl.ds(off, tq), :]], axis=1)
        s = lax.dot_general(qq, kk, NT_DIMS, preferred_element_type=F32)
        if masked:
            r = lax.broadcasted_iota(I32, s.shape, 0)
            c = lax.broadcasted_iota(I32, s.shape, 1)
            r = jnp.where(r >= tq, r - tq, r)
            s = jnp.where(c <= r, s, NEG_INF)
        chunks = [s[:, c * LANES:(c + 1) * LANES] for c in range(n_chunks)]
        m_prev = m_ref[...]
        m_cur = functools.reduce(jnp.maximum, chunks)
        m_new = jnp.maximum(m_prev, jnp.max(m_cur, axis=1, keepdims=True))
        alpha = jnp.exp(m_prev - m_new)
        ps = [jnp.exp(ch - m_new) for ch in chunks]
        l_ref[...] = alpha * l_ref[...] + functools.reduce(jnp.add, ps)
        p = jnp.concatenate([x.astype(BF16) for x in ps], axis=1)
        acc_ref[...] = alpha * acc_ref[...] + jnp.dot(p, v_ref[pl.ds(off, tq), :],
                                                      preferred_element_type=F32)
        m_ref[...] = m_new

    def body(j, carry):
        step(j, False)
        return carry

    lax.fori_loop(0, qi, body, 0)
    step(qi, True)
    o = acc_ref[...] / jnp.sum(l_ref[...], axis=1, keepdims=True)
    if fox:
        o_ref[...] = jnp.where(lane < DB, o[0:tq], o[tq:2 * tq])
    else:
        o_ref[...] = o[0:tq] - lam_ref[0:1, :] * o[tq:2 * tq]


def _attention(q16, qbias, k16, kbias, v16, lam, batch, seq_len, tq, fox):
    n = q16.shape[0]
    groups = q16.shape[1] // LANES
    nq = seq_len // tq
    q_spec = pl.BlockSpec((tq, LANES), lambda b, g, i: (b * nq + i, g))
    kv_spec = pl.BlockSpec((seq_len, LANES), lambda b, g, i: (b, g))
    if fox:
        qb_spec, kb_spec = q_spec, kv_spec
    else:
        qb_spec = pl.BlockSpec((None, tq, LANES), lambda b, g, i: (g, i, 0))
        kb_spec = pl.BlockSpec((None, seq_len, LANES), lambda b, g, i: (g, 0, 0))
    return pl.pallas_call(
        functools.partial(_attn_kernel, tq=tq, fox=fox),
        grid=(batch, groups, nq),
        in_specs=[q_spec, qb_spec, kv_spec, kb_spec, kv_spec,
                  pl.BlockSpec((SUBLANES, LANES), lambda b, g, i: (0, 0))],
        out_specs=q_spec,
        out_shape=jax.ShapeDtypeStruct((n, groups * LANES), F32),
        scratch_shapes=[pltpu.VMEM((2 * tq, LANES), F32)] * 3,
        compiler_params=_cparams(("parallel", "parallel", "arbitrary")),
        name="attn_fox" if fox else "attn_diff",
    )(q16, qbias, k16, kbias, v16, lam)


def _alibi_bias(seq_len):
    slopes = 2.0 ** (-8.0 * np.arange(1, HA + 1, dtype=np.float64) / HA)
    pos = np.arange(seq_len)
    hi, lo = (pos // LANES) * float(LANES), (pos % LANES).astype(np.float64)
    qb = np.zeros((HA, seq_len, LANES), np.float32)
    kb = np.zeros((HA, seq_len, LANES), np.float32)
    for h in range(HA):
        qb[h, :, 0], qb[h, :, 1], qb[h, :, 2], qb[h, :, 3] = -slopes[h] * hi, -slopes[h] * lo, 1.0, 1.0
        kb[h, :, 0], kb[h, :, 1], kb[h, :, 2], kb[h, :, 3] = 1.0, 1.0, slopes[h] * hi, slopes[h] * lo
    return jnp.asarray(qb, BF16), jnp.asarray(kb, BF16)


def _decode_kernel(pt_ref, *refs, pages_per_step, n_steps, page):
    del pt_ref
    pp = pages_per_step
    paged = [refs[c * pp:(c + 1) * pp] for c in range(5)]
    (nka_ref, nva_ref, nkb_ref, nvb_ref, nlf_ref, wqa_ref, wqb_ref, biasa_ref, slope_ref,
     maska_ref, maskb_ref, triu_ref, dmaskb_ref, lam_ref) = refs[5 * pp:5 * pp + 14]
    oa_ref, ob_ref = refs[5 * pp + 14:5 * pp + 16]
    (ma_ref, la_ref, acca_ref, mb_ref, lb_ref, accb_ref, carry_ref) = refs[5 * pp + 16:]
    j = pl.program_id(1)
    half_rows = HA * QPAD

    @pl.when(j == 0)
    def _():
        ma_ref[...] = jnp.full(ma_ref.shape, NEG_INF, F32)
        mb_ref[...] = jnp.full(mb_ref.shape, NEG_INF, F32)
        la_ref[...] = jnp.zeros(la_ref.shape, F32)
        lb_ref[...] = jnp.zeros(lb_ref.shape, F32)
        acca_ref[...] = jnp.zeros(acca_ref.shape, F32)
        accb_ref[...] = jnp.zeros(accb_ref.shape, F32)
        carry_ref[...] = jnp.zeros(carry_ref.shape, F32)

    def softmax_step(chunks, m_ref, l_ref):
        m_prev = m_ref[...]
        m_cur = functools.reduce(jnp.maximum, chunks)
        m_new = jnp.maximum(m_prev, jnp.max(m_cur, axis=1, keepdims=True))
        alpha = jnp.exp(m_prev - m_new)
        ps = [jnp.exp(s - m_new) for s in chunks]
        l_ref[...] = alpha * l_ref[...] + functools.reduce(jnp.add, ps)
        m_ref[...] = m_new
        return alpha, ps

    def process(pages):
        chunks = []
        for ka_r, _, _, _, _, page_idx, masked in pages:
            parts = []
            for h in range(HA):
                k_h = ka_r[pl.ds(h, page, stride=HA), :].astype(BF16)
                parts.append(lax.dot_general(wqa_ref[h * 2 * QPAD:(h + 1) * 2 * QPAD, :], k_h, NT_DIMS,
                                             preferred_element_type=F32))
            sa = jnp.concatenate([x[0:QPAD] for x in parts] + [x[QPAD:2 * QPAD] for x in parts], axis=0)
            sa = sa + biasa_ref[...] + slope_ref[...] * (page_idx * float(page))
            if masked:
                sa = jnp.where(maska_ref[...] > 0.0, sa, NEG_INF)
            chunks.append(sa)
        alpha, ps = softmax_step(chunks, ma_ref, la_ref)
        acc = alpha * acca_ref[...]
        for (_, va_r, _, _, _, _, _), p in zip(pages, ps):
            pv = []
            for h in range(HA):
                v_h = va_r[pl.ds(h, page, stride=HA), :].astype(BF16)
                p_h = jnp.concatenate([p[h * QPAD:(h + 1) * QPAD],
                                       p[half_rows + h * QPAD:half_rows + (h + 1) * QPAD]], axis=0)
                pv.append(jnp.dot(p_h.astype(BF16), v_h, preferred_element_type=F32))
            acc = acc + jnp.concatenate([x[0:QPAD] for x in pv] + [x[QPAD:2 * QPAD] for x in pv], axis=0)
        acca_ref[...] = acc
        triu = triu_ref[...]
        carry = carry_ref[...]
        chunks = []
        for _, _, kb_r, _, lf_r, _, masked in pages:
            hi, mid, lo = _split3(lf_r[...])
            cum = (jnp.dot(hi, triu, preferred_element_type=F32)
                   + jnp.dot(mid, triu, preferred_element_type=F32)
                   + jnp.dot(lo, triu, preferred_element_type=F32)) + carry
            carry = jnp.broadcast_to(cum[:, page - 1:page], carry.shape)
            sb = jnp.dot(wqb_ref[...], kb_r[...].astype(BF16), preferred_element_type=F32)
            sb = sb - jnp.concatenate([cum] * 4, axis=0)
            if masked:
                sb = jnp.where(maskb_ref[...] > 0.0, sb, NEG_INF)
            chunks.append(sb)
        carry_ref[...] = carry
        alpha, ps = softmax_step(chunks, mb_ref, lb_ref)
        acc = jnp.concatenate([alpha] * (WB // LANES), axis=1) * accb_ref[...]
        for (_, _, _, vb_r, _, _, _), p in zip(pages, ps):
            acc = acc + lax.dot_general(p.astype(BF16), vb_r[...].astype(BF16), NT_DIMS,
                                        preferred_element_type=F32)
        accb_ref[...] = acc

    cached = [(paged[0][k], paged[1][k], paged[2][k], paged[3][k], paged[4][k],
               (j * pp + k).astype(F32), False) for k in range(pp)]

    @pl.when(j < n_steps - 1)
    def _():
        process(cached)

    @pl.when(j == n_steps - 1)
    def _():
        process(cached + [(nka_ref, nva_ref, nkb_ref, nvb_ref, nlf_ref, float(n_steps * pp), True)])
        oa = acca_ref[...] / jnp.sum(la_ref[...], axis=1, keepdims=True)
        oa_ref[...] = oa[0:half_rows] - lam_ref[0:1, :] * oa[half_rows:2 * half_rows]
        ob = accb_ref[...] / jnp.sum(lb_ref[...], axis=1, keepdims=True) * dmaskb_ref[...]
        ob_ref[...] = ob[:, 0:128] + ob[:, 128:256] + ob[:, 256:384] + ob[:, 384:512]


def _decode_attention(page_table, caches, new_pages, wqa, wqb, lam, new_tokens, pages_per_step):
    dec_batch, n_pages = page_table.shape
    page = caches[4].shape[2]
    pp = pages_per_step
    n_steps = n_pages // pp
    past = n_pages * page
    rows_a = 2 * HA * QPAD
    rows_b = new_tokens * HB
    slopes = 2.0 ** (-8.0 * np.arange(1, HA + 1, dtype=np.float64) / HA)
    ra = np.arange(rows_a)
    ha, qa_i = (ra // QPAD) % HA, np.minimum(ra % QPAD, new_tokens - 1)
    col = np.arange(page)
    biasa = slopes[ha][:, None] * (col[None, :] - past - qa_i[:, None])
    slope_rep = np.repeat(slopes[ha][:, None], page, axis=1)
    maska = (col[None, :] <= qa_i[:, None]).astype(np.float32)
    rb = np.arange(rows_b)
    maskb = (col[None, :] <= (rb // HB)[:, None]).astype(np.float32)
    triu = (np.arange(page)[:, None] <= np.arange(page)[None, :]).astype(np.float32)
    cb = np.arange(WB)
    dmaskb = (cb[None, :] // DB == (rb % HB)[:, None]).astype(np.float32)
    consts = [jnp.asarray(biasa, F32), jnp.asarray(slope_rep, F32), jnp.asarray(maska),
              jnp.asarray(maskb), jnp.asarray(triu, BF16), jnp.asarray(dmaskb), lam]

    def page_spec(shape, k):
        return pl.BlockSpec((None,) + shape,
                            lambda b, j, pt: (pt[b * n_pages + j * pp + k], 0, 0))

    in_specs, args = [], []
    for c in caches:
        for k in range(pp):
            in_specs.append(page_spec(c.shape[1:], k))
            args.append(c)
    for a in new_pages + [wqa, wqb]:
        in_specs.append(pl.BlockSpec((None,) + a.shape[1:], lambda b, j, pt: (b, 0, 0)))
        args.append(a)
    for a in consts:
        in_specs.append(pl.BlockSpec(a.shape, lambda b, j, pt: (0, 0)))
        args.append(a)
    grid_spec = pltpu.PrefetchScalarGridSpec(
        num_scalar_prefetch=1,
        grid=(dec_batch, n_steps),
        in_specs=in_specs,
        out_specs=[pl.BlockSpec((None, rows_a // 2, LANES), lambda b, j, pt: (b, 0, 0)),
                   pl.BlockSpec((None, rows_b, LANES), lambda b, j, pt: (b, 0, 0))],
        scratch_shapes=[pltpu.VMEM((rows_a, LANES), F32), pltpu.VMEM((rows_a, LANES), F32),
                        pltpu.VMEM((rows_a, DVA), F32),
                        pltpu.VMEM((rows_b, LANES), F32), pltpu.VMEM((rows_b, LANES), F32),
                        pltpu.VMEM((rows_b, WB), F32),
                        pltpu.VMEM((HB, page), F32)],
    )
    return pl.pallas_call(
        functools.partial(_decode_kernel, pages_per_step=pp, n_steps=n_steps, page=page),
        grid_spec=grid_spec,
        out_shape=[jax.ShapeDtypeStruct((dec_batch, rows_a // 2, LANES), F32),
                   jax.ShapeDtypeStruct((dec_batch, rows_b, LANES), F32)],
        compiler_params=_cparams(("parallel", "arbitrary")),
        name="decode_attn",
    )(page_table.reshape(-1), *args)


def _merge_kernel(oa_ref, ob_ref, x_ref, g1_ref, sh2_ref, sc2_ref, sub_ref, wo_ref, n2_ref,
                  xp_ref, h2_ref, *, out_scale):
    oa = oa_ref[...]
    parts = []
    for h in range(HA):
        o = oa[:, h * DVA:(h + 1) * DVA]
        ms = jnp.mean(o * o, axis=-1, keepdims=True)
        parts.append(o * lax.rsqrt(ms + EPS) * sub_ref[...] * out_scale)
    o = jnp.concatenate(parts + [ob_ref[...]], axis=1).astype(BF16)
    y = jnp.dot(o, wo_ref[...], preferred_element_type=F32)
    xp = x_ref[...] + g1_ref[...] * y
    xp_ref[...] = xp
    ms = jnp.mean(xp * xp, axis=-1, keepdims=True)
    h2 = xp * lax.rsqrt(ms + EPS) * n2_ref[...]
    h2_ref[...] = h2 * (1.0 + sc2_ref[...]) + sh2_ref[...]


def _merge(oa, ob, x2d, g1, sh2, sc2, sub_g, w_out16, n2_g, per_row, rows_per_mod, tm, out_scale):
    n = x2d.shape[0]
    mod = _mod_spec(per_row, tm, rows_per_mod)
    row = lambda w: pl.BlockSpec((tm, w), lambda i: (i, 0))
    const = lambda a: pl.BlockSpec(a.shape, lambda i: (0,) * a.ndim)
    return pl.pallas_call(
        functools.partial(_merge_kernel, out_scale=out_scale),
        grid=(n // tm,),
        in_specs=[row(WA), row(WB), row(D_MODEL), mod, mod, mod, const(sub_g), const(w_out16),
                  const(n2_g)],
        out_specs=[row(D_MODEL), row(D_MODEL)],
        out_shape=[jax.ShapeDtypeStruct((n, D_MODEL), F32)] * 2,
        compiler_params=_cparams(("parallel",)),
        name="merge",
    )(oa, ob, x2d, g1, sh2, sc2, sub_g, w_out16, n2_g)


def _topk_kernel(h_ref, wq_ref, sk_ref, idx_ref, gate_ref, s_ref, ts_ref, ti_ref, be_ref, bg_ref, *, tb):
    hb = h_ref[...].astype(BF16)
    qt = lax.dot_general(wq_ref[...], hb, NT_DIMS, preferred_element_type=F32)
    for hp in range(2 * PEER_HEADS):
        sub = sk_ref[hp % 2]
        s_ref[hp] = jnp.dot(sub, qt[hp * N_KEYS:(hp + 1) * N_KEYS, :].astype(BF16),
                            preferred_element_type=F32)
    row_k = lax.broadcasted_iota(I32, (N_KEYS, tb), 0).astype(F32)

    def stage1(hp, carry):
        s = s_ref[hp]
        for r in range(PEER_TOPK):
            m = jnp.max(s, axis=0, keepdims=True)
            pick = jnp.min(jnp.where(s == m, row_k, float(N_KEYS)), axis=0, keepdims=True)
            ts_ref[hp, r:r + 1, :] = m
            ti_ref[hp, r:r + 1, :] = pick
            s = jnp.where(row_k == pick, NEG_INF, s)
        return carry

    lax.fori_loop(0, 2 * PEER_HEADS, stage1, 0)
    half = PEER_TOPK // 2
    n_cand = PEER_TOPK + (half - 1) * half + half
    row_c = lax.broadcasted_iota(I32, (n_cand, tb), 0).astype(F32)

    def candidates(x, y, combine):
        rows = [combine(x[0:1, :], y)]
        rows += [combine(x[i:i + 1, :], y[0:half, :]) for i in range(1, half)]
        rows.append(combine(x[half:PEER_TOPK, :], y[0:1, :]))
        return jnp.concatenate(rows, axis=0)

    def stage2(h, carry):
        a, b = ts_ref[2 * h], ts_ref[2 * h + 1]
        ia, ib = ti_ref[2 * h], ti_ref[2 * h + 1]
        cs = candidates(a, b, lambda x, y: x + y)
        ce = candidates(ia, ib, lambda x, y: x * float(N_KEYS) + y)
        best, experts = [], []
        for r in range(PEER_TOPK):
            m = jnp.max(cs, axis=0, keepdims=True)
            pick = jnp.min(jnp.where(cs == m, row_c, float(n_cand)), axis=0, keepdims=True)
            sel = row_c == pick
            experts.append(jnp.max(jnp.where(sel, ce, -1.0), axis=0, keepdims=True))
            best.append(m)
            cs = jnp.where(sel, NEG_INF, cs)
        bs = jnp.concatenate(best, axis=0)
        ex = jnp.exp(bs - bs[0:1, :])
        off = pl.multiple_of(h * PEER_TOPK, PEER_TOPK)
        bg_ref[pl.ds(off, PEER_TOPK), :] = ex / jnp.sum(ex, axis=0, keepdims=True)
        be_ref[pl.ds(off, PEER_TOPK), :] = jnp.concatenate(experts, axis=0)
        return carry

    lax.fori_loop(0, PEER_HEADS, stage2, 0)
    idx_ref[...] = (be_ref[...].T * float(ROW_TILE)).astype(I32)
    gate_ref[...] = bg_ref[...].T


def _topk(h2, wq_t16, subkeys16, tb):
    n = h2.shape[0]
    const = lambda a: pl.BlockSpec(a.shape, lambda i: (0,) * a.ndim)
    return pl.pallas_call(
        functools.partial(_topk_kernel, tb=tb),
        grid=(n // tb,),
        in_specs=[pl.BlockSpec((tb, D_MODEL), lambda i: (i, 0)), const(wq_t16), const(subkeys16)],
        out_specs=[pl.BlockSpec((tb, N_SLOTS), lambda i: (i, 0))] * 2,
        out_shape=[jax.ShapeDtypeStruct((n, N_SLOTS), I32), jax.ShapeDtypeStruct((n, N_SLOTS), F32)],
        scratch_shapes=[pltpu.VMEM((2 * PEER_HEADS, N_KEYS, tb), F32),
                        pltpu.VMEM((2 * PEER_HEADS, PEER_TOPK, tb), F32),
                        pltpu.VMEM((2 * PEER_HEADS, PEER_TOPK, tb), F32),
                        pltpu.VMEM((N_SLOTS, tb), F32), pltpu.VMEM((N_SLOTS, tb), F32)],
        compiler_params=_cparams(("parallel",)),
        name="peer_topk",
    )(h2, wq_t16, subkeys16)


def _pack_kernel(t_ref, o_ref):
    x = t_ref[...]
    half = x.shape[1] // 2
    rows = x.shape[0]
    bits = lax.bitcast_convert_type(x.astype(BF16).astype(F32), I32)
    word = lax.shift_right_logical(bits[:, :half], 16) | bits[:, half:]
    for c in range(ROW_TILE):
        o_ref[pl.ds(c, rows, stride=ROW_TILE), :] = word[:, c * LANES:(c + 1) * LANES]


def _pack_table(t, rows=512):
    n_e, d = t.shape
    return pl.pallas_call(
        _pack_kernel,
        grid=(n_e // rows,),
        in_specs=[pl.BlockSpec((rows, d), lambda i: (i, 0))],
        out_specs=pl.BlockSpec((rows * ROW_TILE, LANES), lambda i: (i, 0)),
        out_shape=jax.ShapeDtypeStruct((n_e * ROW_TILE, LANES), I32),
        compiler_params=_cparams(("parallel",)),
        name="pack_table",
    )(t)


def _peer_u_kernel(idx_ref, h_ref, gate_ref, tbl_ref, mask_ref, sel_ref, w_ref, pair_ref, *stage_refs, tb):
    buf_a, buf_b, buf_c, buf_d = stage_refs

    def gather(t, stage):
        for e in range(N_SLOTS):
            row = pl.multiple_of(idx_ref[t, e], ROW_TILE)
            stage[e * ROW_TILE:(e + 1) * ROW_TILE, :] = tbl_ref[pl.ds(row, ROW_TILE), :]

    def contract(t, stage):
        h3 = jnp.concatenate(_split3(h_ref[t]), axis=0)
        acc = jnp.zeros((3 * SUBLANES, 2 * N_SLOTS), F32)
        for c in range(ROW_TILE):
            word = stage[pl.ds(c, N_SLOTS, stride=ROW_TILE), :]
            rhs = pltpu.bitcast(word, BF16)
            acc = acc + mask_ref[c] * lax.dot_general(h3, rhs, NT_DIMS, preferred_element_type=F32)
        pair_ref[pl.ds(t, 1), :] = jnp.sum(acc, axis=0, keepdims=True)

    for buf in (buf_c, buf_d):
        buf[...] = jnp.zeros(buf.shape, I32)

    def pair(t, fill, drain):
        gather(t, fill[0])
        gather(t + 1, fill[1])
        contract(jnp.maximum(t - 2, 0), drain[0])
        contract(jnp.maximum(t - 1, 0), drain[1])

    def tokens(i, carry):
        for k in range(PAIRS_PER_TRIP):
            t = 2 * (PAIRS_PER_TRIP * i + k)
            if k % 2 == 0:
                pair(t, (buf_a, buf_b), (buf_c, buf_d))
            else:
                pair(t, (buf_c, buf_d), (buf_a, buf_b))
        return carry

    lax.fori_loop(0, tb // (2 * PAIRS_PER_TRIP), tokens, 0)
    contract(tb - 2, buf_c)
    contract(tb - 1, buf_d)
    a = sum(jnp.dot(x, sel_ref[...], preferred_element_type=F32) for x in _split3(pair_ref[...]))
    w_ref[...] = gate_ref[...] * (0.5 * a * (1.0 + lax.erf(a * (2.0 ** -0.5))))


def _chunk_masks():
    m = np.zeros((ROW_TILE, 3 * SUBLANES, 2 * N_SLOTS), np.float32)
    for c in range(ROW_TILE):
        for term in range(3):
            m[c, term * SUBLANES + c, 0::2] = 1.0
            m[c, term * SUBLANES + ROW_TILE + c, 1::2] = 1.0
    sel = np.zeros((2 * N_SLOTS, N_SLOTS), np.float32)
    sel[np.arange(2 * N_SLOTS), np.arange(2 * N_SLOTS) // 2] = 1.0
    return jnp.asarray(m), jnp.asarray(sel, BF16)


def _peer_u(idx, h2r, gate, tbl, tb):
    n = idx.shape[0]
    masks, sel = _chunk_masks()
    return pl.pallas_call(
        functools.partial(_peer_u_kernel, tb=tb),
        grid=(n // tb,),
        in_specs=[pl.BlockSpec((tb, N_SLOTS), lambda i: (i, 0), memory_space=pltpu.SMEM),
                  pl.BlockSpec((tb, SUBLANES, LANES), lambda i: (i, 0, 0)),
                  pl.BlockSpec((tb, N_SLOTS), lambda i: (i, 0)),
                  pl.BlockSpec(memory_space=pltpu.VMEM),
                  pl.BlockSpec(masks.shape, lambda i: (0, 0, 0)),
                  pl.BlockSpec(sel.shape, lambda i: (0, 0))],
        out_specs=pl.BlockSpec((tb, N_SLOTS), lambda i: (i, 0)),
        out_shape=jax.ShapeDtypeStruct((n, N_SLOTS), F32),
        scratch_shapes=[pltpu.VMEM((tb, 2 * N_SLOTS), F32)]
        + [pltpu.VMEM((N_SLOTS * ROW_TILE, LANES), I32)] * 4,
        compiler_params=_cparams(("arbitrary",)),
        name="peer_u",
    )(idx, h2r, gate, tbl, masks, sel)


def _peer_v_kernel(idx_ref, w_ref, tbl_ref, elo_ref, ehi_ref, o_ref, *scratch_refs, tb):
    wexp_refs = scratch_refs[0:2]
    buf_a, buf_b, buf_c, buf_d = scratch_refs[2:6]
    n_terms = 3

    for half_ref in wexp_refs:
        half_ref[...] = jnp.zeros(half_ref.shape, F32)
    for k, term in enumerate(_split3(w_ref[...])):
        for j, spread_ref in enumerate((elo_ref, ehi_ref)):
            spread = jnp.dot(term, spread_ref[...], preferred_element_type=F32)
            for half, half_ref in enumerate(wexp_refs):
                half_ref[pl.ds(j * n_terms + k, tb, stride=SUBLANES), :] = (
                    spread[:, half * LANES:(half + 1) * LANES])

    def gather(t, stage):
        for e in range(N_SLOTS):
            row = pl.multiple_of(idx_ref[t, e], ROW_TILE)
            stage[e * ROW_TILE:(e + 1) * ROW_TILE, :] = tbl_ref[pl.ds(row, ROW_TILE), :]

    def contract(t, stage):
        rows = pl.ds(pl.multiple_of(t * SUBLANES, SUBLANES), SUBLANES)
        lhs = jnp.concatenate([half_ref[rows, :] for half_ref in wexp_refs], axis=1).astype(BF16)
        lo, hi = [], []
        for c in range(ROW_TILE):
            rhs = pltpu.bitcast(stage[pl.ds(c, N_SLOTS, stride=ROW_TILE), :], BF16)
            r = jnp.dot(lhs, rhs, preferred_element_type=F32)
            lo.append(r[0:1] + r[1:2] + r[2:3])
            hi.append(r[3:4] + r[4:5] + r[5:6])
        o_ref[pl.ds(t, 1), :] = jnp.concatenate(lo + hi, axis=1)

    for buf in (buf_c, buf_d):
        buf[...] = jnp.zeros(buf.shape, I32)

    def pair(t, fill, drain):
        gather(t, fill[0])
        gather(t + 1, fill[1])
        contract(jnp.maximum(t - 2, 0), drain[0])
        contract(jnp.maximum(t - 1, 0), drain[1])

    def tokens(i, carry):
        for k in range(PAIRS_PER_TRIP):
            t = 2 * (PAIRS_PER_TRIP * i + k)
            if k % 2 == 0:
                pair(t, (buf_a, buf_b), (buf_c, buf_d))
            else:
                pair(t, (buf_c, buf_d), (buf_a, buf_b))
        return carry

    lax.fori_loop(0, tb // (2 * PAIRS_PER_TRIP), tokens, 0)
    contract(tb - 2, buf_c)
    contract(tb - 1, buf_d)


def _half_spreads():
    lo = np.zeros((N_SLOTS, 2 * N_SLOTS), np.float32)
    hi = np.zeros((N_SLOTS, 2 * N_SLOTS), np.float32)
    lo[np.arange(N_SLOTS), 2 * np.arange(N_SLOTS)] = 1.0
    hi[np.arange(N_SLOTS), 2 * np.arange(N_SLOTS) + 1] = 1.0
    return jnp.asarray(lo, BF16), jnp.asarray(hi, BF16)


def _peer_v(idx, w, tbl, tb):
    n = idx.shape[0]
    elo, ehi = _half_spreads()
    blk = pl.BlockSpec((tb, N_SLOTS), lambda i: (i, 0))
    return pl.pallas_call(
        functools.partial(_peer_v_kernel, tb=tb),
        grid=(n // tb,),
        in_specs=[pl.BlockSpec((tb, N_SLOTS), lambda i: (i, 0), memory_space=pltpu.SMEM), blk,
                  pl.BlockSpec(memory_space=pltpu.VMEM),
                  pl.BlockSpec(elo.shape, lambda i: (0, 0)), pl.BlockSpec(ehi.shape, lambda i: (0, 0))],
        out_specs=pl.BlockSpec((tb, D_MODEL), lambda i: (i, 0)),
        out_shape=jax.ShapeDtypeStruct((n, D_MODEL), F32),
        scratch_shapes=[pltpu.VMEM((tb * SUBLANES, LANES), F32)] * 2
        + [pltpu.VMEM((N_SLOTS * ROW_TILE, LANES), I32)] * 4,
        compiler_params=_cparams(("arbitrary",)),
        name="peer_v",
    )(idx, w, tbl, elo, ehi)


def _final_kernel(xp_ref, po_ref, g2_ref, fg_ref, y_ref):
    x = xp_ref[...] + g2_ref[...] * po_ref[...]
    ms = jnp.mean(x * x, axis=-1, keepdims=True)
    y_ref[...] = x * lax.rsqrt(ms + EPS) * fg_ref[...]


def _final(xp, po, g2, fg, per_row, rows_per_mod, tm):
    n = xp.shape[0]
    row = pl.BlockSpec((tm, D_MODEL), lambda i: (i, 0))
    return pl.pallas_call(
        _final_kernel,
        grid=(n // tm,),
        in_specs=[row, row, _mod_spec(per_row, tm, rows_per_mod),
                  pl.BlockSpec((1, D_MODEL), lambda i: (0, 0))],
        out_specs=row,
        out_shape=jax.ShapeDtypeStruct((n, D_MODEL), F32),
        compiler_params=_cparams(("parallel",)),
        name="final_norm",
    )(xp, po, g2, fg)


def _peer(h2, xp, g2, fg, wq_t16, subkeys16, u_tbl, v_tbl, per_row, rows_per_mod, tm, tb_topk, tb_gather):
    n = h2.shape[0]
    idx, gate = _topk(h2, wq_t16, subkeys16, tb_topk)
    w = _peer_u(idx, h2.reshape(n, SUBLANES, LANES), gate, u_tbl, tb_gather)
    po = _peer_v(idx, w, v_tbl, tb_gather)
    return _final(xp, po, g2, fg, per_row, rows_per_mod, tm)


def kernel(x_prompt, x_sample, c_prompt, c_sample, cache_a_k, cache_a_v, cache_b_k, cache_b_v, cache_b_logf, page_table, w_ada, b_ada, norm1_g, w_in, b_f, lambda_q1, lambda_k1, lambda_q2, lambda_k2, subln_g, w_out, norm2_g, peer_wq, peer_subkeys, peer_u, peer_v, final_g):
    batch, seq_len, d = x_prompt.shape
    dec_batch, new_tokens, _ = x_sample.shape
    depth = w_ada.shape[0]
    assert depth == 1 and d == D_MODEL and new_tokens == 4
    n_p, n_s = batch * seq_len, dec_batch * new_tokens
    n_pool, page = cache_a_k.shape[1], cache_a_k.shape[2]
    l = 0
    lam_init = 0.8 - 0.6 * math.exp(-0.3 * l)

    n_c = batch + dec_batch
    pad = (-n_c) % SUBLANES
    c_all = jnp.concatenate([c_prompt, c_sample, jnp.zeros((pad, d), F32)], axis=0)
    ada = _ada(c_all, w_ada[l], b_ada[l])
    mods_p = [m.reshape(batch, 1, d) for m in jnp.split(ada[:batch], 6, axis=-1)]
    mods_s = [jnp.repeat(m, new_tokens, axis=0) for m in jnp.split(ada[batch:n_c], 6, axis=-1)]
    lam = _lam(lambda_q1[l], lambda_k1[l], lambda_q2[l], lambda_k2[l], lam_init)

    n_main = 3 * WA + 3 * WB
    w_main = w_in[l][:, :n_main].astype(BF16)
    w_f = jnp.pad(w_in[l][:, n_main:], ((0, 0), (0, LANES - HB))).astype(BF16)
    bf = jnp.pad(b_f[l], (0, LANES - HB)).reshape(1, LANES)
    g1n = norm1_g[l].reshape(1, d)
    g2n = norm2_g[l].reshape(1, d)
    sub_g = subln_g[l].reshape(1, DVA)
    w_out16 = w_out[l].astype(BF16)
    wq_t16 = peer_wq[l].T.astype(BF16)
    subkeys16 = peer_subkeys[l].astype(BF16)
    u_tbl = _pack_table(peer_u[l])
    v_tbl = _pack_table(peer_v[l])
    fg = final_g.reshape(1, d)

    tm_p = 256
    xp2d = x_prompt.reshape(n_p, d)
    sh1, sc1, g1, sh2, sc2, g2 = mods_p
    (ka, va, kb, vb, lf, qa16, ka16, va16, qb16, kb16, vb16, qbias, kbias) = _proj(
        xp2d, sh1, sc1, g1n, w_main, w_f, bf, False, seq_len, seq_len, tm_p)
    qbias_a, kbias_a = _alibi_bias(seq_len)
    tq = 1024
    oa = _attention(qa16, qbias_a, ka16, kbias_a, va16, lam, batch, seq_len, tq, fox=False)
    ob = _attention(qb16, qbias, kb16, kbias, vb16, lam, batch, seq_len, tq, fox=True)
    xp_mid, h2 = _merge(oa, ob, xp2d, g1, sh2, sc2, sub_g, w_out16, g2n, False, seq_len, tm_p,
                        1.0 - lam_init)
    y_prompt = _peer(h2, xp_mid, g2, fg, wq_t16, subkeys16, u_tbl, v_tbl, False, seq_len, tm_p,
                     256, 128).reshape(batch, seq_len, d)

    tm_s = n_s
    xs2d = x_sample.reshape(n_s, d)
    sh1, sc1, g1, sh2, sc2, g2 = mods_s
    (ka_s, va_s, kb_s, vb_s, lf_s, qa16_s, _, _, qb16_s, _, _, _, _) = _proj(
        xs2d, sh1, sc1, g1n, w_main, w_f, bf, True, n_s, n_s, tm_s)
    assert page == LANES
    qa5 = qa16_s.reshape(dec_batch, new_tokens, HA, 2, DA)
    zeros = jnp.zeros_like(qa5[:, :, :, 0, :])
    wqa = jnp.stack([jnp.concatenate([qa5[:, :, :, 0, :], zeros], axis=-1),
                     jnp.concatenate([zeros, qa5[:, :, :, 1, :]], axis=-1)], axis=3)
    wqa = jnp.pad(wqa, ((0, 0), (0, QPAD - new_tokens), (0, 0), (0, 0), (0, 0)))
    wqa = wqa.transpose(0, 2, 3, 1, 4).reshape(dec_batch, HA * 2 * QPAD, 2 * DA)
    qb4 = qb16_s.reshape(dec_batch, new_tokens, HB, DB)
    wqb = (qb4[:, :, :, None, :] * jnp.eye(HB, dtype=BF16)[None, None, :, :, None])
    wqb = wqb.reshape(dec_batch, new_tokens * HB, WB)

    def pos_major(c):
        return c.reshape(n_pool, page * HA, c.shape[-1])

    def pos_minor(c):
        return c.transpose(0, 2, 3, 1).reshape(n_pool, HB * DB, page)

    caches = [pos_major(cache_a_k[l]), pos_major(cache_a_v[l]), pos_minor(cache_b_k[l]),
              pos_minor(cache_b_v[l]), cache_b_logf[l].transpose(0, 2, 1)]

    def new_pos_major(a):
        a = a.reshape(dec_batch, new_tokens * HA, a.shape[1] // HA)
        return jnp.pad(a, ((0, 0), (0, (page - new_tokens) * HA), (0, 0)))

    def new_pos_minor(a, heads):
        w = a.shape[1] // heads
        a = a.reshape(dec_batch, new_tokens, heads, w).transpose(0, 2, 3, 1)
        return jnp.pad(a, ((0, 0), (0, 0), (0, 0), (0, page - new_tokens))).reshape(dec_batch, heads * w, page)

    new_pages = [new_pos_major(ka_s), new_pos_major(va_s), new_pos_minor(kb_s, HB),
                 new_pos_minor(vb_s, HB), new_pos_minor(lf_s, HB)]
    oa_s, ob_s = _decode_attention(page_table, caches, new_pages, wqa, wqb, lam, new_tokens, 8)
    oa_s = oa_s.reshape(dec_batch, HA, QPAD, DVA)[:, :, :new_tokens].transpose(0, 2, 1, 3).reshape(n_s, WA)
    ob_s = ob_s.reshape(dec_batch, new_tokens, HB, 2, DB)
    ob_s = jnp.where((jnp.arange(HB) % 2 == 0)[None, None, :, None], ob_s[:, :, :, 0, :], ob_s[:, :, :, 1, :])
    ob_s = ob_s.reshape(n_s, WB)
    xs_mid, h2_s = _merge(oa_s, ob_s, xs2d, g1, sh2, sc2, sub_g, w_out16, g2n, True, n_s, tm_s,
                          1.0 - lam_init)
    y_sample = _peer(h2_s, xs_mid, g2, fg, wq_t16, subkeys16, u_tbl, v_tbl, True, n_s, tm_s,
                     n_s, 128).reshape(dec_batch, new_tokens, d)

    def kv(a, b, t, h, w):
        return a.reshape(1, b, t, h, w)

    return (y_prompt, y_sample,
            kv(ka, batch, seq_len, HA, 2 * DA), kv(va, batch, seq_len, HA, DVA),
            kv(kb, batch, seq_len, HB, DB), kv(vb, batch, seq_len, HB, DB),
            lf.reshape(1, batch, seq_len, HB),
            kv(ka_s, dec_batch, new_tokens, HA, 2 * DA), kv(va_s, dec_batch, new_tokens, HA, DVA),
            kv(kb_s, dec_batch, new_tokens, HB, DB), kv(vb_s, dec_batch, new_tokens, HB, DB),
            lf_s.reshape(1, dec_batch, new_tokens, HB))
```

```python
import functools
import math

import jax
import jax.numpy as jnp
import numpy as np
from jax import lax
from jax.experimental import pallas as pl
from jax.experimental.pallas import tpu as pltpu

F32, BF16, I32 = jnp.float32, jnp.bfloat16, jnp.int32
EPS = 1e-6
D_MODEL = 1024
HA, DA, DVA = 4, 64, 128
HB, DB = 8, 64
WA, WB = HA * DVA, HB * DB
N_KEYS = 128
PEER_HEADS = 8
PEER_TOPK = 16
N_SLOTS = PEER_HEADS * PEER_TOPK
N_EXPERTS = N_KEYS * N_KEYS
LANES = 128
SUBLANES = 8
ROW_TILE = D_MODEL // (2 * LANES)
QPAD = SUBLANES
PAIRS_PER_TRIP = 16
NEG_INF = float("-inf")
VMEM_LIMIT = 56 * 1024 * 1024
NT_DIMS = (((1,), (1,)), ((), ()))


def _cparams(sem):
    return pltpu.CompilerParams(dimension_semantics=sem, vmem_limit_bytes=VMEM_LIMIT)


def _split3(x):
    hi = x.astype(BF16)
    r = x - hi.astype(F32)
    mid = r.astype(BF16)
    lo = (r - mid.astype(F32)).astype(BF16)
    return hi, mid, lo


def _ada_kernel(c_ref, w_ref, b_ref, o_ref):
    c = c_ref[...]
    s = c * (1.0 / (1.0 + jnp.exp(-c)))
    o_ref[...] = jnp.dot(s.astype(BF16), w_ref[...].astype(BF16),
                         preferred_element_type=F32) + b_ref[...]


def _ada(c_all, w, b):
    r = c_all.shape[0]
    n_out = w.shape[1]
    return pl.pallas_call(
        _ada_kernel,
        grid=(n_out // D_MODEL,),
        in_specs=[pl.BlockSpec((r, D_MODEL), lambda j: (0, 0)),
                  pl.BlockSpec((D_MODEL, D_MODEL), lambda j: (0, j)),
                  pl.BlockSpec((1, D_MODEL), lambda j: (0, j))],
        out_specs=pl.BlockSpec((r, D_MODEL), lambda j: (0, j)),
        out_shape=jax.ShapeDtypeStruct((r, n_out), F32),
        compiler_params=_cparams(("parallel",)),
        name="ada",
    )(c_all, w, b.reshape(1, n_out))


def _lam_kernel(v_ref, o_ref, *, lam_init):
    v = v_ref[...]
    a = jnp.sum(v[0:1] * v[1:2], axis=1, keepdims=True)
    b = jnp.sum(v[2:3] * v[3:4], axis=1, keepdims=True)
    lam = jnp.exp(a) - jnp.exp(b) + lam_init
    o_ref[...] = jnp.broadcast_to(lam, o_ref.shape)


def _lam(lq1, lk1, lq2, lk2, lam_init):
    v = jnp.zeros((SUBLANES, LANES), F32)
    v = v.at[0:4, 0:DA].set(jnp.stack([lq1, lk1, lq2, lk2]))
    return pl.pallas_call(
        functools.partial(_lam_kernel, lam_init=lam_init),
        out_shape=jax.ShapeDtypeStruct((SUBLANES, LANES), F32),
        name="lam",
    )(v)


def _mod_spec(per_row, tm, rows_per_mod):
    if per_row:
        return pl.BlockSpec((tm, D_MODEL), lambda i: (i, 0))
    blocks = rows_per_mod // tm
    return pl.BlockSpec((None, 1, D_MODEL), lambda i: (i // blocks, 0, 0))


def _proj_kernel(x_ref, sh_ref, sc_ref, g_ref, w_ref, wf_ref, bf_ref, tri_ref, pq_ref, pk_ref,
                 qc_ref, kc_ref,
                 ka_ref, va_ref, kb_ref, vb_ref, lf_ref,
                 qa16_ref, ka16_ref, va16_ref, qb16_ref, kb16_ref, vb16_ref, qbias_ref, kbias_ref,
                 carry_ref, *, blocks_per_seq):
    i = pl.program_id(0)

    @pl.when(i % blocks_per_seq == 0)
    def _():
        carry_ref[...] = jnp.zeros_like(carry_ref)

    x = x_ref[...]
    ms = jnp.mean(x * x, axis=-1, keepdims=True)
    h = x * lax.rsqrt(ms + EPS) * g_ref[...]
    h = h * (1.0 + sc_ref[...]) + sh_ref[...]
    hb = h.astype(BF16)
    p = jnp.dot(hb, w_ref[...], preferred_element_type=F32)
    qa, ka, va = p[:, 0:WA], p[:, WA:2 * WA], p[:, 2 * WA:3 * WA]
    o = 3 * WA
    qb, kb, vb = p[:, o:o + WB], p[:, o + WB:o + 2 * WB], p[:, o + 2 * WB:o + 3 * WB]
    tm = x.shape[0]
    for hd in range(HA):
        ka_ref[pl.ds(hd, tm, stride=HA), :] = ka[:, hd * DVA:(hd + 1) * DVA]
        va_ref[pl.ds(hd, tm, stride=HA), :] = va[:, hd * DVA:(hd + 1) * DVA]
    kb_ref[...] = kb
    vb_ref[...] = vb
    qa16_ref[...] = (qa * (DA ** -0.5)).astype(BF16)
    ka16_ref[...] = ka.astype(BF16)
    va16_ref[...] = va.astype(BF16)
    qb16_ref[...] = (qb * (DB ** -0.5)).astype(BF16)
    kb16_ref[...] = kb.astype(BF16)
    vb16_ref[...] = vb.astype(BF16)

    z = jnp.dot(hb, wf_ref[...], preferred_element_type=F32) + bf_ref[...]
    lf = jnp.minimum(z, 0.0) - jnp.log1p(jnp.exp(-jnp.abs(z)))
    lane = lax.broadcasted_iota(I32, lf.shape, 1)
    lf = jnp.where(lane < HB, lf, 0.0)
    lf_ref[...] = lf[:, 0:HB]

    tri = tri_ref[...]
    hi, mid, lo = _split3(lf)
    cum = (jnp.dot(tri, hi, preferred_element_type=F32)
           + jnp.dot(tri, mid, preferred_element_type=F32)
           + jnp.dot(tri, lo, preferred_element_type=F32)) + carry_ref[...]
    tm = cum.shape[0]
    carry_ref[...] = cum[tm - 1:tm, :]
    ccat = jnp.concatenate(_split3(cum), axis=1)
    qbias_ref[...] = (jnp.dot(ccat, pq_ref[...], preferred_element_type=F32) + qc_ref[...]).astype(BF16)
    kbias_ref[...] = (jnp.dot(ccat, pk_ref[...], preferred_element_type=F32) + kc_ref[...]).astype(BF16)


def _decay_placement():
    pq = np.zeros((3 * LANES, WB), np.float32)
    pk = np.zeros((3 * LANES, WB), np.float32)
    qc = np.zeros((1, WB), np.float32)
    kc = np.zeros((1, WB), np.float32)
    for h in range(HB):
        pair, which = divmod(h, 2)
        base = pair * LANES + which * 6
        for t in range(3):
            pq[t * LANES + h, base + t] = 1.0
            qc[0, base + 3 + t] = 1.0
            kc[0, base + t] = 1.0
            pk[t * LANES + h, base + 3 + t] = -1.0
    return (jnp.asarray(pq, BF16), jnp.asarray(pk, BF16), jnp.asarray(qc), jnp.asarray(kc))


def _proj(x2d, sh, sc, g, w_main, w_f, b_f, per_row, rows_per_mod, seq_len, tm):
    n = x2d.shape[0]
    nw = w_main.shape[1]
    tri = (np.arange(tm)[None, :] <= np.arange(tm)[:, None]).astype(np.float32)
    pq, pk, qc, kc = _decay_placement()
    mod = _mod_spec(per_row, tm, rows_per_mod)
    row = lambda w: pl.BlockSpec((tm, w), lambda i: (i, 0))
    const = lambda a: pl.BlockSpec(a.shape, lambda i: (0,) * a.ndim)
    tri = jnp.asarray(tri, BF16)
    head_rows = pl.BlockSpec((tm * HA, DVA), lambda i: (i, 0))
    outs = ([jax.ShapeDtypeStruct((n * HA, DVA), F32)] * 2 + [jax.ShapeDtypeStruct((n, WB), F32)] * 2
            + [jax.ShapeDtypeStruct((n, HB), F32)] + [jax.ShapeDtypeStruct((n, WA), BF16)] * 8)
    return pl.pallas_call(
        functools.partial(_proj_kernel, blocks_per_seq=seq_len // tm),
        grid=(n // tm,),
        in_specs=[row(D_MODEL), mod, mod, const(g), const(w_main), const(w_f), const(b_f),
                  const(tri), const(pq), const(pk), const(qc), const(kc)],
        out_specs=[head_rows] * 2 + [row(WB)] * 2 + [row(HB)] + [row(WA)] * 8,
        out_shape=outs,
        scratch_shapes=[pltpu.VMEM((1, LANES), F32)],
        compiler_params=_cparams(("arbitrary",)),
        name="proj",
    )(x2d, sh, sc, g, w_main, w_f, b_f, tri, pq, pk, qc, kc)


def _attn_kernel(q_ref, qb_ref, k_ref, kb_ref, v_ref, lam_ref, o_ref, m_ref, l_ref, acc_ref,
                 *, tq, fox):
    qi = pl.program_id(2)
    q = q_ref[...].astype(F32)
    qb = qb_ref[...].astype(F32)
    lane = lax.broadcasted_iota(I32, (tq, LANES), 1)
    q_lo = jnp.where(lane < DA, q, 0.0)
    q_hi = jnp.where(lane >= DA, q, 0.0)
    if fox:
        qb_lo = jnp.where(lane < 6, qb, 0.0)
        qb_hi = jnp.where((lane >= 6) & (lane < 12), qb, 0.0)
    else:
        qb_lo = qb_hi = qb
    qq = jnp.concatenate([jnp.concatenate([q_lo, qb_lo], axis=1),
                          jnp.concatenate([q_hi, qb_hi], axis=1)], axis=0).astype(BF16)
    m_ref[...] = jnp.full(m_ref.shape, NEG_INF, F32)
    l_ref[...] = jnp.zeros(l_ref.shape, F32)
    acc_ref[...] = jnp.zeros(acc_ref.shape, F32)

    n_chunks = tq // LANES

    def step(j, masked):
        off = pl.multiple_of(j * tq, tq)
        kk = jnp.concatenate([k_ref[pl.ds(off, tq), :], kb_ref[pl.ds(off, tq), :]], axis=1)
        s = lax.dot_general(qq, kk, NT_DIMS, preferred_element_type=F32)
        if masked:
            r = lax.broadcasted_iota(I32, s.shape, 0)
            c = lax.broadcasted_iota(I32, s.shape, 1)
            r = jnp.where(r >= tq, r - tq, r)
            s = jnp.where(c <= r, s, NEG_INF)
        chunks = [s[:, c * LANES:(c + 1) * LANES] for c in range(n_chunks)]
        m_prev = m_ref[...]
        m_cur = functools.reduce(jnp.maximum, chunks)
        m_new = jnp.maximum(m_prev, jnp.max(m_cur, axis=1, keepdims=True))
        alpha = jnp.exp(m_prev - m_new)
        ps = [jnp.exp(ch - m_new) for ch in chunks]
        l_ref[...] = alpha * l_ref[...] + functools.reduce(jnp.add, ps)
        p = jnp.concatenate([x.astype(BF16) for x in ps], axis=1)
        acc_ref[...] = alpha * acc_ref[...] + jnp.dot(p, v_ref[pl.ds(off, tq), :],
                                                      preferred_element_type=F32)
        m_ref[...] = m_new

    def body(j, carry):
        step(j, False)
        return carry

    lax.fori_loop(0, qi, body, 0)
    step(qi, True)
    o = acc_ref[...] / jnp.sum(l_ref[...], axis=1, keepdims=True)
    if fox:
        o_ref[...] = jnp.where(lane < DB, o[0:tq], o[tq:2 * tq])
    else:
        o_ref[...] = o[0:tq] - lam_ref[0:1, :] * o[tq:2 * tq]


def _attention(q16, qbias, k16, kbias, v16, lam, batch, seq_len, tq, fox):
    n = q16.shape[0]
    groups = q16.shape[1] // LANES
    nq = seq_len // tq
    q_spec = pl.BlockSpec((tq, LANES), lambda b, g, i: (b * nq + i, g))
    kv_spec = pl.BlockSpec((seq_len, LANES), lambda b, g, i: (b, g))
    if fox:
        qb_spec, kb_spec = q_spec, kv_spec
    else:
        qb_spec = pl.BlockSpec((None, tq, LANES), lambda b, g, i: (g, i, 0))
        kb_spec = pl.BlockSpec((None, seq_len, LANES), lambda b, g, i: (g, 0, 0))
    return pl.pallas_call(
        functools.partial(_attn_kernel, tq=tq, fox=fox),
        grid=(batch, groups, nq),
        in_specs=[q_spec, qb_spec, kv_spec, kb_spec, kv_spec,
                  pl.BlockSpec((SUBLANES, LANES), lambda b, g, i: (0, 0))],
        out_specs=q_spec,
        out_shape=jax.ShapeDtypeStruct((n, groups * LANES), F32),
        scratch_shapes=[pltpu.VMEM((2 * tq, LANES), F32)] * 3,
        compiler_params=_cparams(("parallel", "parallel", "arbitrary")),
        name="attn_fox" if fox else "attn_diff",
    )(q16, qbias, k16, kbias, v16, lam)


def _alibi_bias(seq_len):
    slopes = 2.0 ** (-8.0 * np.arange(1, HA + 1, dtype=np.float64) / HA)
    pos = np.arange(seq_len)
    hi, lo = (pos // LANES) * float(LANES), (pos % LANES).astype(np.float64)
    qb = np.zeros((HA, seq_len, LANES), np.float32)
    kb = np.zeros((HA, seq_len, LANES), np.float32)
    for h in range(HA):
        qb[h, :, 0], qb[h, :, 1], qb[h, :, 2], qb[h, :, 3] = -slopes[h] * hi, -slopes[h] * lo, 1.0, 1.0
        kb[h, :, 0], kb[h, :, 1], kb[h, :, 2], kb[h, :, 3] = 1.0, 1.0, slopes[h] * hi, slopes[h] * lo
    return jnp.asarray(qb, BF16), jnp.asarray(kb, BF16)


def _decode_kernel(pt_ref, *refs, pages_per_step, n_steps, page):
    del pt_ref
    pp = pages_per_step
    paged = [refs[c * pp:(c + 1) * pp] for c in range(5)]
    (nka_ref, nva_ref, nkb_ref, nvb_ref, nlf_ref, wqa_ref, wqb_ref, biasa_ref, slope_ref,
     maska_ref, maskb_ref, triu_ref, dmaskb_ref, lam_ref) = refs[5 * pp:5 * pp + 14]
    oa_ref, ob_ref = refs[5 * pp + 14:5 * pp + 16]
    (ma_ref, la_ref, acca_ref, mb_ref, lb_ref, accb_ref, carry_ref) = refs[5 * pp + 16:]
    j = pl.program_id(1)
    half_rows = HA * QPAD

    @pl.when(j == 0)
    def _():
        ma_ref[...] = jnp.full(ma_ref.shape, NEG_INF, F32)
        mb_ref[...] = jnp.full(mb_ref.shape, NEG_INF, F32)
        la_ref[...] = jnp.zeros(la_ref.shape, F32)
        lb_ref[...] = jnp.zeros(lb_ref.shape, F32)
        acca_ref[...] = jnp.zeros(acca_ref.shape, F32)
        accb_ref[...] = jnp.zeros(accb_ref.shape, F32)
        carry_ref[...] = jnp.zeros(carry_ref.shape, F32)

    def softmax_step(chunks, m_ref, l_ref):
        m_prev = m_ref[...]
        m_cur = functools.reduce(jnp.maximum, chunks)
        m_new = jnp.maximum(m_prev, jnp.max(m_cur, axis=1, keepdims=True))
        alpha = jnp.exp(m_prev - m_new)
        ps = [jnp.exp(s - m_new) for s in chunks]
        l_ref[...] = alpha * l_ref[...] + functools.reduce(jnp.add, ps)
        m_ref[...] = m_new
        return alpha, ps

    def process(pages):
        chunks = []
        for ka_r, _, _, _, _, page_idx, masked in pages:
            parts = []
            for h in range(HA):
                k_h = ka_r[pl.ds(h, page, stride=HA), :].astype(BF16)
                parts.append(lax.dot_general(wqa_ref[h * 2 * QPAD:(h + 1) * 2 * QPAD, :], k_h, NT_DIMS,
                                             preferred_element_type=F32))
            sa = jnp.concatenate([x[0:QPAD] for x in parts] + [x[QPAD:2 * QPAD] for x in parts], axis=0)
            sa = sa + biasa_ref[...] + slope_ref[...] * (page_idx * float(page))
            if masked:
                sa = jnp.where(maska_ref[...] > 0.0, sa, NEG_INF)
            chunks.append(sa)
        alpha, ps = softmax_step(chunks, ma_ref, la_ref)
        acc = alpha * acca_ref[...]
        for (_, va_r, _, _, _, _, _), p in zip(pages, ps):
            pv = []
            for h in range(HA):
                v_h = va_r[pl.ds(h, page, stride=HA), :].astype(BF16)
                p_h = jnp.concatenate([p[h * QPAD:(h + 1) * QPAD],
                                       p[half_rows + h * QPAD:half_rows + (h + 1) * QPAD]], axis=0)
                pv.append(jnp.dot(p_h.astype(BF16), v_h, preferred_element_type=F32))
            acc = acc + jnp.concatenate([x[0:QPAD] for x in pv] + [x[QPAD:2 * QPAD] for x in pv], axis=0)
        acca_ref[...] = acc
        triu = triu_ref[...]
        carry = carry_ref[...]
        chunks = []
        for _, _, kb_r, _, lf_r, _, masked in pages:
            hi, mid, lo = _split3(lf_r[...])
            cum = (jnp.dot(hi, triu, preferred_element_type=F32)
                   + jnp.dot(mid, triu, preferred_element_type=F32)
                   + jnp.dot(lo, triu, preferred_element_type=F32)) + carry
            carry = jnp.broadcast_to(cum[:, page - 1:page], carry.shape)
            sb = jnp.dot(wqb_ref[...], kb_r[...].astype(BF16), preferred_element_type=F32)
            sb = sb - jnp.concatenate([cum] * 4, axis=0)
            if masked:
                sb = jnp.where(maskb_ref[...] > 0.0, sb, NEG_INF)
            chunks.append(sb)
        carry_ref[...] = carry
        alpha, ps = softmax_step(chunks, mb_ref, lb_ref)
        acc = jnp.concatenate([alpha] * (WB // LANES), axis=1) * accb_ref[...]
        for (_, _, _, vb_r, _, _, _), p in zip(pages, ps):
            acc = acc + lax.dot_general(p.astype(BF16), vb_r[...].astype(BF16), NT_DIMS,
                                        preferred_element_type=F32)
        accb_ref[...] = acc

    cached = [(paged[0][k], paged[1][k], paged[2][k], paged[3][k], paged[4][k],
               (j * pp + k).astype(F32), False) for k in range(pp)]

    @pl.when(j < n_steps - 1)
    def _():
        process(cached)

    @pl.when(j == n_steps - 1)
    def _():
        process(cached + [(nka_ref, nva_ref, nkb_ref, nvb_ref, nlf_ref, float(n_steps * pp), True)])
        oa = acca_ref[...] / jnp.sum(la_ref[...], axis=1, keepdims=True)
        oa_ref[...] = oa[0:half_rows] - lam_ref[0:1, :] * oa[half_rows:2 * half_rows]
        ob = accb_ref[...] / jnp.sum(lb_ref[...], axis=1, keepdims=True) * dmaskb_ref[...]
        ob_ref[...] = ob[:, 0:128] + ob[:, 128:256] + ob[:, 256:384] + ob[:, 384:512]


def _decode_attention(page_table, caches, new_pages, wqa, wqb, lam, new_tokens, pages_per_step):
    dec_batch, n_pages = page_table.shape
    page = caches[4].shape[2]
    pp = pages_per_step
    n_steps = n_pages // pp
    past = n_pages * page
    rows_a = 2 * HA * QPAD
    rows_b = new_tokens * HB
    slopes = 2.0 ** (-8.0 * np.arange(1, HA + 1, dtype=np.float64) / HA)
    ra = np.arange(rows_a)
    ha, qa_i = (ra // QPAD) % HA, np.minimum(ra % QPAD, new_tokens - 1)
    col = np.arange(page)
    biasa = slopes[ha][:, None] * (col[None, :] - past - qa_i[:, None])
    slope_rep = np.repeat(slopes[ha][:, None], page, axis=1)
    maska = (col[None, :] <= qa_i[:, None]).astype(np.float32)
    rb = np.arange(rows_b)
    maskb = (col[None, :] <= (rb // HB)[:, None]).astype(np.float32)
    triu = (np.arange(page)[:, None] <= np.arange(page)[None, :]).astype(np.float32)
    cb = np.arange(WB)
    dmaskb = (cb[None, :] // DB == (rb % HB)[:, None]).astype(np.float32)
    consts = [jnp.asarray(biasa, F32), jnp.asarray(slope_rep, F32), jnp.asarray(maska),
              jnp.asarray(maskb), jnp.asarray(triu, BF16), jnp.asarray(dmaskb), lam]

    def page_spec(shape, k):
        return pl.BlockSpec((None,) + shape,
                            lambda b, j, pt: (pt[b * n_pages + j * pp + k], 0, 0))

    in_specs, args = [], []
    for c in caches:
        for k in range(pp):
            in_specs.append(page_spec(c.shape[1:], k))
            args.append(c)
    for a in new_pages + [wqa, wqb]:
        in_specs.append(pl.BlockSpec((None,) + a.shape[1:], lambda b, j, pt: (b, 0, 0)))
        args.append(a)
    for a in consts:
        in_specs.append(pl.BlockSpec(a.shape, lambda b, j, pt: (0, 0)))
        args.append(a)
    grid_spec = pltpu.PrefetchScalarGridSpec(
        num_scalar_prefetch=1,
        grid=(dec_batch, n_steps),
        in_specs=in_specs,
        out_specs=[pl.BlockSpec((None, rows_a // 2, LANES), lambda b, j, pt: (b, 0, 0)),
                   pl.BlockSpec((None, rows_b, LANES), lambda b, j, pt: (b, 0, 0))],
        scratch_shapes=[pltpu.VMEM((rows_a, LANES), F32), pltpu.VMEM((rows_a, LANES), F32),
                        pltpu.VMEM((rows_a, DVA), F32),
                        pltpu.VMEM((rows_b, LANES), F32), pltpu.VMEM((rows_b, LANES), F32),
                        pltpu.VMEM((rows_b, WB), F32),
                        pltpu.VMEM((HB, page), F32)],
    )
    return pl.pallas_call(
        functools.partial(_decode_kernel, pages_per_step=pp, n_steps=n_steps, page=page),
        grid_spec=grid_spec,
        out_shape=[jax.ShapeDtypeStruct((dec_batch, rows_a // 2, LANES), F32),
                   jax.ShapeDtypeStruct((dec_batch, rows_b, LANES), F32)],
        compiler_params=_cparams(("parallel", "arbitrary")),
        name="decode_attn",
    )(page_table.reshape(-1), *args)


def _merge_kernel(oa_ref, ob_ref, x_ref, g1_ref, sh2_ref, sc2_ref, sub_ref, wo_ref, n2_ref,
                  xp_ref, h2_ref, *, out_scale):
    oa = oa_ref[...]
    parts = []
    for h in range(HA):
        o = oa[:, h * DVA:(h + 1) * DVA]
        ms = jnp.mean(o * o, axis=-1, keepdims=True)
        parts.append(o * lax.rsqrt(ms + EPS) * sub_ref[...] * out_scale)
    o = jnp.concatenate(parts + [ob_ref[...]], axis=1).astype(BF16)
    y = jnp.dot(o, wo_ref[...], preferred_element_type=F32)
    xp = x_ref[...] + g1_ref[...] * y
    xp_ref[...] = xp
    ms = jnp.mean(xp * xp, axis=-1, keepdims=True)
    h2 = xp * lax.rsqrt(ms + EPS) * n2_ref[...]
    h2_ref[...] = h2 * (1.0 + sc2_ref[...]) + sh2_ref[...]


def _merge(oa, ob, x2d, g1, sh2, sc2, sub_g, w_out16, n2_g, per_row, rows_per_mod, tm, out_scale):
    n = x2d.shape[0]
    mod = _mod_spec(per_row, tm, rows_per_mod)
    row = lambda w: pl.BlockSpec((tm, w), lambda i: (i, 0))
    const = lambda a: pl.BlockSpec(a.shape, lambda i: (0,) * a.ndim)
    return pl.pallas_call(
        functools.partial(_merge_kernel, out_scale=out_scale),
        grid=(n // tm,),
        in_specs=[row(WA), row(WB), row(D_MODEL), mod, mod, mod, const(sub_g), const(w_out16),
                  const(n2_g)],
        out_specs=[row(D_MODEL), row(D_MODEL)],
        out_shape=[jax.ShapeDtypeStruct((n, D_MODEL), F32)] * 2,
        compiler_params=_cparams(("parallel",)),
        name="merge",
    )(oa, ob, x2d, g1, sh2, sc2, sub_g, w_out16, n2_g)


def _topk_kernel(h_ref, wq_ref, sk_ref, idx_ref, gate_ref, s_ref, ts_ref, ti_ref, be_ref, bg_ref, *, tb):
    hb = h_ref[...].astype(BF16)
    qt = lax.dot_general(wq_ref[...], hb, NT_DIMS, preferred_element_type=F32)
    for hp in range(2 * PEER_HEADS):
        sub = sk_ref[hp % 2]
        s_ref[hp] = jnp.dot(sub, qt[hp * N_KEYS:(hp + 1) * N_KEYS, :].astype(BF16),
                            preferred_element_type=F32)
    row_k = lax.broadcasted_iota(I32, (N_KEYS, tb), 0).astype(F32)

    def stage1(hp, carry):
        s = s_ref[hp]
        for r in range(PEER_TOPK):
            m = jnp.max(s, axis=0, keepdims=True)
            pick = jnp.min(jnp.where(s == m, row_k, float(N_KEYS)), axis=0, keepdims=True)
            ts_ref[hp, r:r + 1, :] = m
            ti_ref[hp, r:r + 1, :] = pick
            s = jnp.where(row_k == pick, NEG_INF, s)
        return carry

    lax.fori_loop(0, 2 * PEER_HEADS, stage1, 0)
    half = PEER_TOPK // 2
    n_cand = PEER_TOPK + (half - 1) * half + half
    row_c = lax.broadcasted_iota(I32, (n_cand, tb), 0).astype(F32)

    def candidates(x, y, combine):
        rows = [combine(x[0:1, :], y)]
        rows += [combine(x[i:i + 1, :], y[0:half, :]) for i in range(1, half)]
        rows.append(combine(x[half:PEER_TOPK, :], y[0:1, :]))
        return jnp.concatenate(rows, axis=0)

    def stage2(h, carry):
        a, b = ts_ref[2 * h], ts_ref[2 * h + 1]
        ia, ib = ti_ref[2 * h], ti_ref[2 * h + 1]
        cs = candidates(a, b, lambda x, y: x + y)
        ce = candidates(ia, ib, lambda x, y: x * float(N_KEYS) + y)
        best, experts = [], []
        for r in range(PEER_TOPK):
            m = jnp.max(cs, axis=0, keepdims=True)
            pick = jnp.min(jnp.where(cs == m, row_c, float(n_cand)), axis=0, keepdims=True)
            sel = row_c == pick
            experts.append(jnp.max(jnp.where(sel, ce, -1.0), axis=0, keepdims=True))
            best.append(m)
            cs = jnp.where(sel, NEG_INF, cs)
        bs = jnp.concatenate(best, axis=0)
        ex = jnp.exp(bs - bs[0:1, :])
        off = pl.multiple_of(h * PEER_TOPK, PEER_TOPK)
        bg_ref[pl.ds(off, PEER_TOPK), :] = ex / jnp.sum(ex, axis=0, keepdims=True)
        be_ref[pl.ds(off, PEER_TOPK), :] = jnp.concatenate(experts, axis=0)
        return carry

    lax.fori_loop(0, PEER_HEADS, stage2, 0)
    idx_ref[...] = (be_ref[...].T * float(ROW_TILE)).astype(I32)
    gate_ref[...] = bg_ref[...].T


def _topk(h2, wq_t16, subkeys16, tb):
    n = h2.shape[0]
    const = lambda a: pl.BlockSpec(a.shape, lambda i: (0,) * a.ndim)
    return pl.pallas_call(
        functools.partial(_topk_kernel, tb=tb),
        grid=(n // tb,),
        in_specs=[pl.BlockSpec((tb, D_MODEL), lambda i: (i, 0)), const(wq_t16), const(subkeys16)],
        out_specs=[pl.BlockSpec((tb, N_SLOTS), lambda i: (i, 0))] * 2,
        out_shape=[jax.ShapeDtypeStruct((n, N_SLOTS), I32), jax.ShapeDtypeStruct((n, N_SLOTS), F32)],
        scratch_shapes=[pltpu.VMEM((2 * PEER_HEADS, N_KEYS, tb), F32),
                        pltpu.VMEM((2 * PEER_HEADS, PEER_TOPK, tb), F32),
                        pltpu.VMEM((2 * PEER_HEADS, PEER_TOPK, tb), F32),
                        pltpu.VMEM((N_SLOTS, tb), F32), pltpu.VMEM((N_SLOTS, tb), F32)],
        compiler_params=_cparams(("parallel",)),
        name="peer_topk",
    )(h2, wq_t16, subkeys16)


def _pack_kernel(t_ref, o_ref):
    x = t_ref[...]
    half = x.shape[1] // 2
    rows = x.shape[0]
    bits = lax.bitcast_convert_type(x.astype(BF16).astype(F32), I32)
    word = lax.shift_right_logical(bits[:, :half], 16) | bits[:, half:]
    for c in range(ROW_TILE):
        o_ref[pl.ds(c, rows, stride=ROW_TILE), :] = word[:, c * LANES:(c + 1) * LANES]


def _pack_table(t, rows=512):
    n_e, d = t.shape
    return pl.pallas_call(
        _pack_kernel,
        grid=(n_e // rows,),
        in_specs=[pl.BlockSpec((rows, d), lambda i: (i, 0))],
        out_specs=pl.BlockSpec((rows * ROW_TILE, LANES), lambda i: (i, 0)),
        out_shape=jax.ShapeDtypeStruct((n_e * ROW_TILE, LANES), I32),
        compiler_params=_cparams(("parallel",)),
        name="pack_table",
    )(t)


def _peer_u_kernel(idx_ref, h_ref, gate_ref, tbl_ref, mask_ref, sel_ref, w_ref, pair_ref, *stage_refs, tb):
    buf_a, buf_b, buf_c, buf_d = stage_refs

    def gather(t, stage):
        for e in range(N_SLOTS):
            row = pl.multiple_of(idx_ref[t, e], ROW_TILE)
            stage[e * ROW_TILE:(e + 1) * ROW_TILE, :] = tbl_ref[pl.ds(row, ROW_TILE), :]

    def contract(t, stage):
        h3 = jnp.concatenate(_split3(h_ref[t]), axis=0)
        acc = jnp.zeros((3 * SUBLANES, 2 * N_SLOTS), F32)
        for c in range(ROW_TILE):
            word = stage[pl.ds(c, N_SLOTS, stride=ROW_TILE), :]
            rhs = pltpu.bitcast(word, BF16)
            acc = acc + mask_ref[c] * lax.dot_general(h3, rhs, NT_DIMS, preferred_element_type=F32)
        pair_ref[pl.ds(t, 1), :] = jnp.sum(acc, axis=0, keepdims=True)

    for buf in (buf_c, buf_d):
        buf[...] = jnp.zeros(buf.shape, I32)

    def pair(t, fill, drain):
        gather(t, fill[0])
        gather(t + 1, fill[1])
        contract(jnp.maximum(t - 2, 0), drain[0])
        contract(jnp.maximum(t - 1, 0), drain[1])

    def tokens(i, carry):
        for k in range(PAIRS_PER_TRIP):
            t = 2 * (PAIRS_PER_TRIP * i + k)
            if k % 2 == 0:
                pair(t, (buf_a, buf_b), (buf_c, buf_d))
            else:
                pair(t, (buf_c, buf_d), (buf_a, buf_b))
        return carry

    lax.fori_loop(0, tb // (2 * PAIRS_PER_TRIP), tokens, 0)
    contract(tb - 2, buf_c)
    contract(tb - 1, buf_d)
    a = sum(jnp.dot(x, sel_ref[...], preferred_element_type=F32) for x in _split3(pair_ref[...]))
    w_ref[...] = gate_ref[...] * (0.5 * a * (1.0 + lax.erf(a * (2.0 ** -0.5))))


def _chunk_masks():
    m = np.zeros((ROW_TILE, 3 * SUBLANES, 2 * N_SLOTS), np.float32)
    for c in range(ROW_TILE):
        for term in range(3):
            m[c, term * SUBLANES + c, 0::2] = 1.0
            m[c, term * SUBLANES + ROW_TILE + c, 1::2] = 1.0
    sel = np.zeros((2 * N_SLOTS, N_SLOTS), np.float32)
    sel[np.arange(2 * N_SLOTS), np.arange(2 * N_SLOTS) // 2] = 1.0
    return jnp.asarray(m), jnp.asarray(sel, BF16)


def _peer_u(idx, h2r, gate, tbl, tb):
    n = idx.shape[0]
    masks, sel = _chunk_masks()
    return pl.pallas_call(
        functools.partial(_peer_u_kernel, tb=tb),
        grid=(n // tb,),
        in_specs=[pl.BlockSpec((tb, N_SLOTS), lambda i: (i, 0), memory_space=pltpu.SMEM),
                  pl.BlockSpec((tb, SUBLANES, LANES), lambda i: (i, 0, 0)),
                  pl.BlockSpec((tb, N_SLOTS), lambda i: (i, 0)),
                  pl.BlockSpec(memory_space=pltpu.VMEM),
                  pl.BlockSpec(masks.shape, lambda i: (0, 0, 0)),
                  pl.BlockSpec(sel.shape, lambda i: (0, 0))],
        out_specs=pl.BlockSpec((tb, N_SLOTS), lambda i: (i, 0)),
        out_shape=jax.ShapeDtypeStruct((n, N_SLOTS), F32),
        scratch_shapes=[pltpu.VMEM((tb, 2 * N_SLOTS), F32)]
        + [pltpu.VMEM((N_SLOTS * ROW_TILE, LANES), I32)] * 4,
        compiler_params=_cparams(("arbitrary",)),
        name="peer_u",
    )(idx, h2r, gate, tbl, masks, sel)


def _peer_v_kernel(idx_ref, w_ref, tbl_ref, elo_ref, ehi_ref, xp_ref, g2_ref, fg_ref, y_ref,
                   *scratch_refs, tb):
    o_ref = scratch_refs[0]
    wexp_refs = scratch_refs[1:3]
    buf_a, buf_b, buf_c, buf_d = scratch_refs[3:7]
    n_terms = 3

    for half_ref in wexp_refs:
        half_ref[...] = jnp.zeros(half_ref.shape, F32)
    for k, term in enumerate(_split3(w_ref[...])):
        for j, spread_ref in enumerate((elo_ref, ehi_ref)):
            spread = jnp.dot(term, spread_ref[...], preferred_element_type=F32)
            for half, half_ref in enumerate(wexp_refs):
                half_ref[pl.ds(j * n_terms + k, tb, stride=SUBLANES), :] = (
                    spread[:, half * LANES:(half + 1) * LANES])

    def gather(t, stage):
        for e in range(N_SLOTS):
            row = pl.multiple_of(idx_ref[t, e], ROW_TILE)
            stage[e * ROW_TILE:(e + 1) * ROW_TILE, :] = tbl_ref[pl.ds(row, ROW_TILE), :]

    def contract(t, stage):
        rows = pl.ds(pl.multiple_of(t * SUBLANES, SUBLANES), SUBLANES)
        lhs = jnp.concatenate([half_ref[rows, :] for half_ref in wexp_refs], axis=1).astype(BF16)
        lo, hi = [], []
        for c in range(ROW_TILE):
            rhs = pltpu.bitcast(stage[pl.ds(c, N_SLOTS, stride=ROW_TILE), :], BF16)
            r = jnp.dot(lhs, rhs, preferred_element_type=F32)
            lo.append(r[0:1] + r[1:2] + r[2:3])
            hi.append(r[3:4] + r[4:5] + r[5:6])
        o_ref[pl.ds(t, 1), :] = jnp.concatenate(lo + hi, axis=1)

    for buf in (buf_c, buf_d):
        buf[...] = jnp.zeros(buf.shape, I32)

    def pair(t, fill, drain):
        gather(t, fill[0])
        gather(t + 1, fill[1])
        contract(jnp.maximum(t - 2, 0), drain[0])
        contract(jnp.maximum(t - 1, 0), drain[1])

    def tokens(i, carry):
        for k in range(PAIRS_PER_TRIP):
            t = 2 * (PAIRS_PER_TRIP * i + k)
            if k % 2 == 0:
                pair(t, (buf_a, buf_b), (buf_c, buf_d))
            else:
                pair(t, (buf_c, buf_d), (buf_a, buf_b))
        return carry

    lax.fori_loop(0, tb // (2 * PAIRS_PER_TRIP), tokens, 0)
    contract(tb - 2, buf_c)
    contract(tb - 1, buf_d)
    x = xp_ref[...] + g2_ref[...] * o_ref[...]
    ms = jnp.mean(x * x, axis=-1, keepdims=True)
    y_ref[...] = x * lax.rsqrt(ms + EPS) * fg_ref[...]


def _half_spreads():
    lo = np.zeros((N_SLOTS, 2 * N_SLOTS), np.float32)
    hi = np.zeros((N_SLOTS, 2 * N_SLOTS), np.float32)
    lo[np.arange(N_SLOTS), 2 * np.arange(N_SLOTS)] = 1.0
    hi[np.arange(N_SLOTS), 2 * np.arange(N_SLOTS) + 1] = 1.0
    return jnp.asarray(lo, BF16), jnp.asarray(hi, BF16)


def _peer_v(idx, w, tbl, xp, g2, fg, per_row, rows_per_mod, tb):
    n = idx.shape[0]
    elo, ehi = _half_spreads()
    blk = pl.BlockSpec((tb, N_SLOTS), lambda i: (i, 0))
    row = pl.BlockSpec((tb, D_MODEL), lambda i: (i, 0))
    return pl.pallas_call(
        functools.partial(_peer_v_kernel, tb=tb),
        grid=(n // tb,),
        in_specs=[pl.BlockSpec((tb, N_SLOTS), lambda i: (i, 0), memory_space=pltpu.SMEM), blk,
                  pl.BlockSpec(memory_space=pltpu.VMEM),
                  pl.BlockSpec(elo.shape, lambda i: (0, 0)), pl.BlockSpec(ehi.shape, lambda i: (0, 0)),
                  row, _mod_spec(per_row, tb, rows_per_mod), pl.BlockSpec((1, D_MODEL), lambda i: (0, 0))],
        out_specs=row,
        out_shape=jax.ShapeDtypeStruct((n, D_MODEL), F32),
        scratch_shapes=[pltpu.VMEM((tb, D_MODEL), F32)]
        + [pltpu.VMEM((tb * SUBLANES, LANES), F32)] * 2
        + [pltpu.VMEM((N_SLOTS * ROW_TILE, LANES), I32)] * 4,
        compiler_params=_cparams(("arbitrary",)),
        name="peer_v",
    )(idx, w, tbl, elo, ehi, xp, g2, fg)


def _peer(h2, xp, g2, fg, wq_t16, subkeys16, u_tbl, v_tbl, per_row, rows_per_mod, tb_topk, tb_gather):
    n = h2.shape[0]
    idx, gate = _topk(h2, wq_t16, subkeys16, tb_topk)
    w = _peer_u(idx, h2.reshape(n, SUBLANES, LANES), gate, u_tbl, tb_gather)
    return _peer_v(idx, w, v_tbl, xp, g2, fg, per_row, rows_per_mod, tb_gather)


def kernel(x_prompt, x_sample, c_prompt, c_sample, cache_a_k, cache_a_v, cache_b_k, cache_b_v, cache_b_logf, page_table, w_ada, b_ada, norm1_g, w_in, b_f, lambda_q1, lambda_k1, lambda_q2, lambda_k2, subln_g, w_out, norm2_g, peer_wq, peer_subkeys, peer_u, peer_v, final_g):
    batch, seq_len, d = x_prompt.shape
    dec_batch, new_tokens, _ = x_sample.shape
    depth = w_ada.shape[0]
    assert depth == 1 and d == D_MODEL and new_tokens == 4
    n_p, n_s = batch * seq_len, dec_batch * new_tokens
    n_pool, page = cache_a_k.shape[1], cache_a_k.shape[2]
    l = 0
    lam_init = 0.8 - 0.6 * math.exp(-0.3 * l)

    n_c = batch + dec_batch
    pad = (-n_c) % SUBLANES
    c_all = jnp.concatenate([c_prompt, c_sample, jnp.zeros((pad, d), F32)], axis=0)
    ada = _ada(c_all, w_ada[l], b_ada[l])
    mods_p = [m.reshape(batch, 1, d) for m in jnp.split(ada[:batch], 6, axis=-1)]
    mods_s = [jnp.repeat(m, new_tokens, axis=0) for m in jnp.split(ada[batch:n_c], 6, axis=-1)]
    lam = _lam(lambda_q1[l], lambda_k1[l], lambda_q2[l], lambda_k2[l], lam_init)

    n_main = 3 * WA + 3 * WB
    w_main = w_in[l][:, :n_main].astype(BF16)
    w_f = jnp.pad(w_in[l][:, n_main:], ((0, 0), (0, LANES - HB))).astype(BF16)
    bf = jnp.pad(b_f[l], (0, LANES - HB)).reshape(1, LANES)
    g1n = norm1_g[l].reshape(1, d)
    g2n = norm2_g[l].reshape(1, d)
    sub_g = subln_g[l].reshape(1, DVA)
    w_out16 = w_out[l].astype(BF16)
    wq_t16 = peer_wq[l].T.astype(BF16)
    subkeys16 = peer_subkeys[l].astype(BF16)
    u_tbl = _pack_table(peer_u[l])
    v_tbl = _pack_table(peer_v[l])
    fg = final_g.reshape(1, d)

    tm_p = 256
    xp2d = x_prompt.reshape(n_p, d)
    sh1, sc1, g1, sh2, sc2, g2 = mods_p
    (ka, va, kb, vb, lf, qa16, ka16, va16, qb16, kb16, vb16, qbias, kbias) = _proj(
        xp2d, sh1, sc1, g1n, w_main, w_f, bf, False, seq_len, seq_len, tm_p)
    qbias_a, kbias_a = _alibi_bias(seq_len)
    tq = 1024
    oa = _attention(qa16, qbias_a, ka16, kbias_a, va16, lam, batch, seq_len, tq, fox=False)
    ob = _attention(qb16, qbias, kb16, kbias, vb16, lam, batch, seq_len, tq, fox=True)
    xp_mid, h2 = _merge(oa, ob, xp2d, g1, sh2, sc2, sub_g, w_out16, g2n, False, seq_len, tm_p,
                        1.0 - lam_init)
    y_prompt = _peer(h2, xp_mid, g2, fg, wq_t16, subkeys16, u_tbl, v_tbl, False, seq_len,
                     256, 128).reshape(batch, seq_len, d)

    tm_s = n_s
    xs2d = x_sample.reshape(n_s, d)
    sh1, sc1, g1, sh2, sc2, g2 = mods_s
    (ka_s, va_s, kb_s, vb_s, lf_s, qa16_s, _, _, qb16_s, _, _, _, _) = _proj(
        xs2d, sh1, sc1, g1n, w_main, w_f, bf, True, n_s, n_s, tm_s)
    assert page == LANES
    qa5 = qa16_s.reshape(dec_batch, new_tokens, HA, 2, DA)
    zeros = jnp.zeros_like(qa5[:, :, :, 0, :])
    wqa = jnp.stack([jnp.concatenate([qa5[:, :, :, 0, :], zeros], axis=-1),
                     jnp.concatenate([zeros, qa5[:, :, :, 1, :]], axis=-1)], axis=3)
    wqa = jnp.pad(wqa, ((0, 0), (0, QPAD - new_tokens), (0, 0), (0, 0), (0, 0)))
    wqa = wqa.transpose(0, 2, 3, 1, 4).reshape(dec_batch, HA * 2 * QPAD, 2 * DA)
    qb4 = qb16_s.reshape(dec_batch, new_tokens, HB, DB)
    wqb = (qb4[:, :, :, None, :] * jnp.eye(HB, dtype=BF16)[None, None, :, :, None])
    wqb = wqb.reshape(dec_batch, new_tokens * HB, WB)

    def pos_major(c):
        return c.reshape(n_pool, page * HA, c.shape[-1])

    def pos_minor(c):
        return c.transpose(0, 2, 3, 1).reshape(n_pool, HB * DB, page)

    caches = [pos_major(cache_a_k[l]), pos_major(cache_a_v[l]), pos_minor(cache_b_k[l]),
              pos_minor(cache_b_v[l]), cache_b_logf[l].transpose(0, 2, 1)]

    def new_pos_major(a):
        a = a.reshape(dec_batch, new_tokens * HA, DVA)
        return jnp.pad(a, ((0, 0), (0, (page - new_tokens) * HA), (0, 0)))

    def new_pos_minor(a, heads):
        w = a.shape[1] // heads
        a = a.reshape(dec_batch, new_tokens, heads, w).transpose(0, 2, 3, 1)
        return jnp.pad(a, ((0, 0), (0, 0), (0, 0), (0, page - new_tokens))).reshape(dec_batch, heads * w, page)

    new_pages = [new_pos_major(ka_s), new_pos_major(va_s), new_pos_minor(kb_s, HB),
                 new_pos_minor(vb_s, HB), new_pos_minor(lf_s, HB)]
    oa_s, ob_s = _decode_attention(page_table, caches, new_pages, wqa, wqb, lam, new_tokens, 8)
    oa_s = oa_s.reshape(dec_batch, HA, QPAD, DVA)[:, :, :new_tokens].transpose(0, 2, 1, 3).reshape(n_s, WA)
    ob_s = ob_s.reshape(dec_batch, new_tokens, HB, 2, DB)
    ob_s = jnp.where((jnp.arange(HB) % 2 == 0)[None, None, :, None], ob_s[:, :, :, 0, :], ob_s[:, :, :, 1, :])
    ob_s = ob_s.reshape(n_s, WB)
    xs_mid, h2_s = _merge(oa_s, ob_s, xs2d, g1, sh2, sc2, sub_g, w_out16, g2n, True, n_s, tm_s,
                          1.0 - lam_init)
    y_sample = _peer(h2_s, xs_mid, g2, fg, wq_t16, subkeys16, u_tbl, v_tbl, True, n_s,
                     n_s, 128).reshape(dec_batch, new_tokens, d)

    def kv(a, b, t, h, w):
        return a.reshape(1, b, t, h, w)

    return (y_prompt, y_sample,
            kv(ka, batch, seq_len, HA, 2 * DA), kv(va, batch, seq_len, HA, DVA),
            kv(kb, batch, seq_len, HB, DB), kv(vb, batch, seq_len, HB, DB),
            lf.reshape(1, batch, seq_len, HB),
            kv(ka_s, dec_batch, new_tokens, HA, 2 * DA), kv(va_s, dec_batch, new_tokens, HA, DVA),
            kv(kb_s, dec_batch, new_tokens, HB, DB), kv(vb_s, dec_batch, new_tokens, HB, DB),
            lf_s.reshape(1, dec_batch, new_tokens, HB))
```

```python
import functools
import math

import jax
import jax.numpy as jnp
import numpy as np
from jax import lax
from jax.experimental import pallas as pl
from jax.experimental.pallas import tpu as pltpu

F32, BF16, I32 = jnp.float32, jnp.bfloat16, jnp.int32
EPS = 1e-6
D_MODEL = 1024
HA, DA, DVA = 4, 64, 128
HB, DB = 8, 64
WA, WB = HA * DVA, HB * DB
N_KEYS = 128
PEER_HEADS = 8
PEER_TOPK = 16
N_SLOTS = PEER_HEADS * PEER_TOPK
N_EXPERTS = N_KEYS * N_KEYS
LANES = 128
SUBLANES = 8
ROW_TILE = D_MODEL // (2 * LANES)
QPAD = SUBLANES
PAIRS_PER_TRIP = 16
LISTS_PER_TRIP = 2
NEG_INF = float("-inf")
VMEM_LIMIT = 56 * 1024 * 1024
NT_DIMS = (((1,), (1,)), ((), ()))


def _cparams(sem):
    return pltpu.CompilerParams(dimension_semantics=sem, vmem_limit_bytes=VMEM_LIMIT)


def _split3(x):
    hi = x.astype(BF16)
    r = x - hi.astype(F32)
    mid = r.astype(BF16)
    lo = (r - mid.astype(F32)).astype(BF16)
    return hi, mid, lo


def _ada_kernel(c_ref, w_ref, b_ref, o_ref):
    c = c_ref[...]
    s = c * (1.0 / (1.0 + jnp.exp(-c)))
    o_ref[...] = jnp.dot(s.astype(BF16), w_ref[...].astype(BF16),
                         preferred_element_type=F32) + b_ref[...]


def _ada(c_all, w, b):
    r = c_all.shape[0]
    n_out = w.shape[1]
    return pl.pallas_call(
        _ada_kernel,
        grid=(n_out // D_MODEL,),
        in_specs=[pl.BlockSpec((r, D_MODEL), lambda j: (0, 0)),
                  pl.BlockSpec((D_MODEL, D_MODEL), lambda j: (0, j)),
                  pl.BlockSpec((1, D_MODEL), lambda j: (0, j))],
        out_specs=pl.BlockSpec((r, D_MODEL), lambda j: (0, j)),
        out_shape=jax.ShapeDtypeStruct((r, n_out), F32),
        compiler_params=_cparams(("parallel",)),
        name="ada",
    )(c_all, w, b.reshape(1, n_out))


def _lam_kernel(v_ref, o_ref, *, lam_init):
    v = v_ref[...]
    a = jnp.sum(v[0:1] * v[1:2], axis=1, keepdims=True)
    b = jnp.sum(v[2:3] * v[3:4], axis=1, keepdims=True)
    lam = jnp.exp(a) - jnp.exp(b) + lam_init
    o_ref[...] = jnp.broadcast_to(lam, o_ref.shape)


def _lam(lq1, lk1, lq2, lk2, lam_init):
    v = jnp.zeros((SUBLANES, LANES), F32)
    v = v.at[0:4, 0:DA].set(jnp.stack([lq1, lk1, lq2, lk2]))
    return pl.pallas_call(
        functools.partial(_lam_kernel, lam_init=lam_init),
        out_shape=jax.ShapeDtypeStruct((SUBLANES, LANES), F32),
        name="lam",
    )(v)


def _mod_spec(per_row, tm, rows_per_mod):
    if per_row:
        return pl.BlockSpec((tm, D_MODEL), lambda i: (i, 0))
    blocks = rows_per_mod // tm
    return pl.BlockSpec((None, 1, D_MODEL), lambda i: (i // blocks, 0, 0))


def _proj_kernel(x_ref, sh_ref, sc_ref, g_ref, w_ref, wf_ref, bf_ref, tri_ref, pq_ref, pk_ref,
                 qc_ref, kc_ref,
                 ka_ref, va_ref, kb_ref, vb_ref, lf_ref,
                 qa16_ref, ka16_ref, va16_ref, qb16_ref, kb16_ref, vb16_ref, qbias_ref, kbias_ref,
                 carry_ref, *, blocks_per_seq):
    i = pl.program_id(0)

    @pl.when(i % blocks_per_seq == 0)
    def _():
        carry_ref[...] = jnp.zeros_like(carry_ref)

    x = x_ref[...]
    ms = jnp.mean(x * x, axis=-1, keepdims=True)
    h = x * lax.rsqrt(ms + EPS) * g_ref[...]
    h = h * (1.0 + sc_ref[...]) + sh_ref[...]
    hb = h.astype(BF16)
    p = jnp.dot(hb, w_ref[...], preferred_element_type=F32)
    qa, ka, va = p[:, 0:WA], p[:, WA:2 * WA], p[:, 2 * WA:3 * WA]
    o = 3 * WA
    qb, kb, vb = p[:, o:o + WB], p[:, o + WB:o + 2 * WB], p[:, o + 2 * WB:o + 3 * WB]
    tm = x.shape[0]
    for hd in range(HA):
        ka_ref[pl.ds(hd, tm, stride=HA), :] = ka[:, hd * DVA:(hd + 1) * DVA]
        va_ref[pl.ds(hd, tm, stride=HA), :] = va[:, hd * DVA:(hd + 1) * DVA]
    kb_ref[...] = kb
    vb_ref[...] = vb
    qa16_ref[...] = (qa * (DA ** -0.5)).astype(BF16)
    ka16_ref[...] = ka.astype(BF16)
    va16_ref[...] = va.astype(BF16)
    qb16_ref[...] = (qb * (DB ** -0.5)).astype(BF16)
    kb16_ref[...] = kb.astype(BF16)
    vb16_ref[...] = vb.astype(BF16)

    z = jnp.dot(hb, wf_ref[...], preferred_element_type=F32) + bf_ref[...]
    lf = jnp.minimum(z, 0.0) - jnp.log1p(jnp.exp(-jnp.abs(z)))
    lane = lax.broadcasted_iota(I32, lf.shape, 1)
    lf = jnp.where(lane < HB, lf, 0.0)
    lf_ref[...] = lf[:, 0:HB]

    tri = tri_ref[...]
    hi, mid, lo = _split3(lf)
    cum = (jnp.dot(tri, hi, preferred_element_type=F32)
           + jnp.dot(tri, mid, preferred_element_type=F32)
           + jnp.dot(tri, lo, preferred_element_type=F32)) + carry_ref[...]
    tm = cum.shape[0]
    carry_ref[...] = cum[tm - 1:tm, :]
    ccat = jnp.concatenate(_split3(cum), axis=1)
    qbias_ref[...] = (jnp.dot(ccat, pq_ref[...], preferred_element_type=F32) + qc_ref[...]).astype(BF16)
    kbias_ref[...] = (jnp.dot(ccat, pk_ref[...], preferred_element_type=F32) + kc_ref[...]).astype(BF16)


def _decay_placement():
    pq = np.zeros((3 * LANES, WB), np.float32)
    pk = np.zeros((3 * LANES, WB), np.float32)
    qc = np.zeros((1, WB), np.float32)
    kc = np.zeros((1, WB), np.float32)
    for h in range(HB):
        pair, which = divmod(h, 2)
        base = pair * LANES + which * 6
        for t in range(3):
            pq[t * LANES + h, base + t] = 1.0
            qc[0, base + 3 + t] = 1.0
            kc[0, base + t] = 1.0
            pk[t * LANES + h, base + 3 + t] = -1.0
    return (jnp.asarray(pq, BF16), jnp.asarray(pk, BF16), jnp.asarray(qc), jnp.asarray(kc))


def _proj(x2d, sh, sc, g, w_main, w_f, b_f, per_row, rows_per_mod, seq_len, tm):
    n = x2d.shape[0]
    nw = w_main.shape[1]
    tri = (np.arange(tm)[None, :] <= np.arange(tm)[:, None]).astype(np.float32)
    pq, pk, qc, kc = _decay_placement()
    mod = _mod_spec(per_row, tm, rows_per_mod)
    row = lambda w: pl.BlockSpec((tm, w), lambda i: (i, 0))
    const = lambda a: pl.BlockSpec(a.shape, lambda i: (0,) * a.ndim)
    tri = jnp.asarray(tri, BF16)
    head_rows = pl.BlockSpec((tm * HA, DVA), lambda i: (i, 0))
    outs = ([jax.ShapeDtypeStruct((n * HA, DVA), F32)] * 2 + [jax.ShapeDtypeStruct((n, WB), F32)] * 2
            + [jax.ShapeDtypeStruct((n, HB), F32)] + [jax.ShapeDtypeStruct((n, WA), BF16)] * 8)
    return pl.pallas_call(
        functools.partial(_proj_kernel, blocks_per_seq=seq_len // tm),
        grid=(n // tm,),
        in_specs=[row(D_MODEL), mod, mod, const(g), const(w_main), const(w_f), const(b_f),
                  const(tri), const(pq), const(pk), const(qc), const(kc)],
        out_specs=[head_rows] * 2 + [row(WB)] * 2 + [row(HB)] + [row(WA)] * 8,
        out_shape=outs,
        scratch_shapes=[pltpu.VMEM((1, LANES), F32)],
        compiler_params=_cparams(("arbitrary",)),
        name="proj",
    )(x2d, sh, sc, g, w_main, w_f, b_f, tri, pq, pk, qc, kc)


def _attn_kernel(q_ref, qb_ref, k_ref, kb_ref, v_ref, lam_ref, o_ref, m_ref, l_ref, acc_ref,
                 *, tq, fox):
    qi = pl.program_id(2)
    q = q_ref[...].astype(F32)
    qb = qb_ref[...].astype(F32)
    lane = lax.broadcasted_iota(I32, (tq, LANES), 1)
    q_lo = jnp.where(lane < DA, q, 0.0)
    q_hi = jnp.where(lane >= DA, q, 0.0)
    if fox:
        qb_lo = jnp.where(lane < 6, qb, 0.0)
        qb_hi = jnp.where((lane >= 6) & (lane < 12), qb, 0.0)
    else:
        qb_lo = qb_hi = qb
    qq = jnp.concatenate([jnp.concatenate([q_lo, qb_lo], axis=1),
                          jnp.concatenate([q_hi, qb_hi], axis=1)], axis=0).astype(BF16)
    m_ref[...] = jnp.full(m_ref.shape, NEG_INF, F32)
    l_ref[...] = jnp.zeros(l_ref.shape, F32)
    acc_ref[...] = jnp.zeros(acc_ref.shape, F32)

    n_chunks = tq // LANES

    def step(j, masked):
        off = pl.multiple_of(j * tq, tq)
        kk = jnp.concatenate([k_ref[pl.ds(off, tq), :], kb_ref[pl.ds(off, tq), :]], axis=1)
        s = lax.dot_general(qq, kk, NT_DIMS, preferred_element_type=F32)
        if masked:
            r = lax.broadcasted_iota(I32, s.shape, 0)
            c = lax.broadcasted_iota(I32, s.shape, 1)
            r = jnp.where(r >= tq, r - tq, r)
            s = jnp.where(c <= r, s, NEG_INF)
        chunks = [s[:, c * LANES:(c + 1) * LANES] for c in range(n_chunks)]
        m_prev = m_ref[...]
        m_cur = functools.reduce(jnp.maximum, chunks)
        m_new = jnp.maximum(m_prev, jnp.max(m_cur, axis=1, keepdims=True))
        alpha = jnp.exp(m_prev - m_new)
        ps = [jnp.exp(ch - m_new) for ch in chunks]
        l_ref[...] = alpha * l_ref[...] + functools.reduce(jnp.add, ps)
        p = jnp.concatenate([x.astype(BF16) for x in ps], axis=1)
        acc_ref[...] = alpha * acc_ref[...] + jnp.dot(p, v_ref[pl.ds(off, tq), :],
                                                      preferred_element_type=F32)
        m_ref[...] = m_new

    def body(j, carry):
        step(j, False)
        return carry

    lax.fori_loop(0, qi, body, 0)
    step(qi, True)
    o = acc_ref[...] / jnp.sum(l_ref[...], axis=1, keepdims=True)
    if fox:
        o_ref[...] = jnp.where(lane < DB, o[0:tq], o[tq:2 * tq])
    else:
        o_ref[...] = o[0:tq] - lam_ref[0:1, :] * o[tq:2 * tq]


def _attention(q16, qbias, k16, kbias, v16, lam, batch, seq_len, tq, fox):
    n = q16.shape[0]
    groups = q16.shape[1] // LANES
    nq = seq_len // tq
    q_spec = pl.BlockSpec((tq, LANES), lambda b, g, i: (b * nq + i, g))
    kv_spec = pl.BlockSpec((seq_len, LANES), lambda b, g, i: (b, g))
    if fox:
        qb_spec, kb_spec = q_spec, kv_spec
    else:
        qb_spec = pl.BlockSpec((None, tq, LANES), lambda b, g, i: (g, i, 0))
        kb_spec = pl.BlockSpec((None, seq_len, LANES), lambda b, g, i: (g, 0, 0))
    return pl.pallas_call(
        functools.partial(_attn_kernel, tq=tq, fox=fox),
        grid=(batch, groups, nq),
        in_specs=[q_spec, qb_spec, kv_spec, kb_spec, kv_spec,
                  pl.BlockSpec((SUBLANES, LANES), lambda b, g, i: (0, 0))],
        out_specs=q_spec,
        out_shape=jax.ShapeDtypeStruct((n, groups * LANES), F32),
        scratch_shapes=[pltpu.VMEM((2 * tq, LANES), F32)] * 3,
        compiler_params=_cparams(("parallel", "parallel", "arbitrary")),
        name="attn_fox" if fox else "attn_diff",
    )(q16, qbias, k16, kbias, v16, lam)


def _alibi_bias(seq_len):
    slopes = 2.0 ** (-8.0 * np.arange(1, HA + 1, dtype=np.float64) / HA)
    pos = np.arange(seq_len)
    hi, lo = (pos // LANES) * float(LANES), (pos % LANES).astype(np.float64)
    qb = np.zeros((HA, seq_len, LANES), np.float32)
    kb = np.zeros((HA, seq_len, LANES), np.float32)
    for h in range(HA):
        qb[h, :, 0], qb[h, :, 1], qb[h, :, 2], qb[h, :, 3] = -slopes[h] * hi, -slopes[h] * lo, 1.0, 1.0
        kb[h, :, 0], kb[h, :, 1], kb[h, :, 2], kb[h, :, 3] = 1.0, 1.0, slopes[h] * hi, slopes[h] * lo
    return jnp.asarray(qb, BF16), jnp.asarray(kb, BF16)


def _decode_kernel(pt_ref, *refs, pages_per_step, n_steps, page):
    del pt_ref
    pp = pages_per_step
    paged = [refs[c * pp:(c + 1) * pp] for c in range(5)]
    (nka_ref, nva_ref, nkb_ref, nvb_ref, nlf_ref, wqa_ref, wqb_ref, biasa_ref, slope_ref,
     maska_ref, maskb_ref, triu_ref, dmaskb_ref, lam_ref) = refs[5 * pp:5 * pp + 14]
    oa_ref, ob_ref = refs[5 * pp + 14:5 * pp + 16]
    (ma_ref, la_ref, acca_ref, mb_ref, lb_ref, accb_ref, carry_ref) = refs[5 * pp + 16:]
    j = pl.program_id(1)
    half_rows = HA * QPAD

    @pl.when(j == 0)
    def _():
        ma_ref[...] = jnp.full(ma_ref.shape, NEG_INF, F32)
        mb_ref[...] = jnp.full(mb_ref.shape, NEG_INF, F32)
        la_ref[...] = jnp.zeros(la_ref.shape, F32)
        lb_ref[...] = jnp.zeros(lb_ref.shape, F32)
        acca_ref[...] = jnp.zeros(acca_ref.shape, F32)
        accb_ref[...] = jnp.zeros(accb_ref.shape, F32)
        carry_ref[...] = jnp.zeros(carry_ref.shape, F32)

    def softmax_step(chunks, m_ref, l_ref):
        m_prev = m_ref[...]
        m_cur = functools.reduce(jnp.maximum, chunks)
        m_new = jnp.maximum(m_prev, jnp.max(m_cur, axis=1, keepdims=True))
        alpha = jnp.exp(m_prev - m_new)
        ps = [jnp.exp(s - m_new) for s in chunks]
        l_ref[...] = alpha * l_ref[...] + functools.reduce(jnp.add, ps)
        m_ref[...] = m_new
        return alpha, ps

    def process(pages):
        chunks = []
        for ka_r, _, _, _, _, page_idx, masked in pages:
            parts = []
            for h in range(HA):
                k_h = ka_r[pl.ds(h, page, stride=HA), :].astype(BF16)
                parts.append(lax.dot_general(wqa_ref[h * 2 * QPAD:(h + 1) * 2 * QPAD, :], k_h, NT_DIMS,
                                             preferred_element_type=F32))
            sa = jnp.concatenate([x[0:QPAD] for x in parts] + [x[QPAD:2 * QPAD] for x in parts], axis=0)
            sa = sa + biasa_ref[...] + slope_ref[...] * (page_idx * float(page))
            if masked:
                sa = jnp.where(maska_ref[...] > 0.0, sa, NEG_INF)
            chunks.append(sa)
        alpha, ps = softmax_step(chunks, ma_ref, la_ref)
        acc = alpha * acca_ref[...]
        for (_, va_r, _, _, _, _, _), p in zip(pages, ps):
            pv = []
            for h in range(HA):
                v_h = va_r[pl.ds(h, page, stride=HA), :].astype(BF16)
                p_h = jnp.concatenate([p[h * QPAD:(h + 1) * QPAD],
                                       p[half_rows + h * QPAD:half_rows + (h + 1) * QPAD]], axis=0)
                pv.append(jnp.dot(p_h.astype(BF16), v_h, preferred_element_type=F32))
            acc = acc + jnp.concatenate([x[0:QPAD] for x in pv] + [x[QPAD:2 * QPAD] for x in pv], axis=0)
        acca_ref[...] = acc
        triu = triu_ref[...]
        carry = carry_ref[...]
        chunks = []
        for _, _, kb_r, _, lf_r, _, masked in pages:
            hi, mid, lo = _split3(lf_r[...])
            cum = (jnp.dot(hi, triu, preferred_element_type=F32)
                   + jnp.dot(mid, triu, preferred_element_type=F32)
                   + jnp.dot(lo, triu, preferred_element_type=F32)) + carry
            carry = jnp.broadcast_to(cum[:, page - 1:page], carry.shape)
            sb = jnp.dot(wqb_ref[...], kb_r[...].astype(BF16), preferred_element_type=F32)
            sb = sb - jnp.concatenate([cum] * 4, axis=0)
            if masked:
                sb = jnp.where(maskb_ref[...] > 0.0, sb, NEG_INF)
            chunks.append(sb)
        carry_ref[...] = carry
        alpha, ps = softmax_step(chunks, mb_ref, lb_ref)
        acc = jnp.concatenate([alpha] * (WB // LANES), axis=1) * accb_ref[...]
        for (_, _, _, vb_r, _, _, _), p in zip(pages, ps):
            acc = acc + lax.dot_general(p.astype(BF16), vb_r[...].astype(BF16), NT_DIMS,
                                        preferred_element_type=F32)
        accb_ref[...] = acc

    cached = [(paged[0][k], paged[1][k], paged[2][k], paged[3][k], paged[4][k],
               (j * pp + k).astype(F32), False) for k in range(pp)]

    @pl.when(j < n_steps - 1)
    def _():
        process(cached)

    @pl.when(j == n_steps - 1)
    def _():
        process(cached + [(nka_ref, nva_ref, nkb_ref, nvb_ref, nlf_ref, float(n_steps * pp), True)])
        oa = acca_ref[...] / jnp.sum(la_ref[...], axis=1, keepdims=True)
        oa_ref[...] = oa[0:half_rows] - lam_ref[0:1, :] * oa[half_rows:2 * half_rows]
        ob = accb_ref[...] / jnp.sum(lb_ref[...], axis=1, keepdims=True) * dmaskb_ref[...]
        ob_ref[...] = ob[:, 0:128] + ob[:, 128:256] + ob[:, 256:384] + ob[:, 384:512]


def _decode_attention(page_table, caches, new_pages, wqa, wqb, lam, new_tokens, pages_per_step):
    dec_batch, n_pages = page_table.shape
    page = caches[4].shape[2]
    pp = pages_per_step
    n_steps = n_pages // pp
    past = n_pages * page
    rows_a = 2 * HA * QPAD
    rows_b = new_tokens * HB
    slopes = 2.0 ** (-8.0 * np.arange(1, HA + 1, dtype=np.float64) / HA)
    ra = np.arange(rows_a)
    ha, qa_i = (ra // QPAD) % HA, np.minimum(ra % QPAD, new_tokens - 1)
    col = np.arange(page)
    biasa = slopes[ha][:, None] * (col[None, :] - past - qa_i[:, None])
    slope_rep = np.repeat(slopes[ha][:, None], page, axis=1)
    maska = (col[None, :] <= qa_i[:, None]).astype(np.float32)
    rb = np.arange(rows_b)
    maskb = (col[None, :] <= (rb // HB)[:, None]).astype(np.float32)
    triu = (np.arange(page)[:, None] <= np.arange(page)[None, :]).astype(np.float32)
    cb = np.arange(WB)
    dmaskb = (cb[None, :] // DB == (rb % HB)[:, None]).astype(np.float32)
    consts = [jnp.asarray(biasa, F32), jnp.asarray(slope_rep, F32), jnp.asarray(maska),
              jnp.asarray(maskb), jnp.asarray(triu, BF16), jnp.asarray(dmaskb), lam]

    def page_spec(shape, k):
        return pl.BlockSpec((None,) + shape,
                            lambda b, j, pt: (pt[b * n_pages + j * pp + k], 0, 0))

    in_specs, args = [], []
    for c in caches:
        for k in range(pp):
            in_specs.append(page_spec(c.shape[1:], k))
            args.append(c)
    for a in new_pages + [wqa, wqb]:
        in_specs.append(pl.BlockSpec((None,) + a.shape[1:], lambda b, j, pt: (b, 0, 0)))
        args.append(a)
    for a in consts:
        in_specs.append(pl.BlockSpec(a.shape, lambda b, j, pt: (0, 0)))
        args.append(a)
    grid_spec = pltpu.PrefetchScalarGridSpec(
        num_scalar_prefetch=1,
        grid=(dec_batch, n_steps),
        in_specs=in_specs,
        out_specs=[pl.BlockSpec((None, rows_a // 2, LANES), lambda b, j, pt: (b, 0, 0)),
                   pl.BlockSpec((None, rows_b, LANES), lambda b, j, pt: (b, 0, 0))],
        scratch_shapes=[pltpu.VMEM((rows_a, LANES), F32), pltpu.VMEM((rows_a, LANES), F32),
                        pltpu.VMEM((rows_a, DVA), F32),
                        pltpu.VMEM((rows_b, LANES), F32), pltpu.VMEM((rows_b, LANES), F32),
                        pltpu.VMEM((rows_b, WB), F32),
                        pltpu.VMEM((HB, page), F32)],
    )
    return pl.pallas_call(
        functools.partial(_decode_kernel, pages_per_step=pp, n_steps=n_steps, page=page),
        grid_spec=grid_spec,
        out_shape=[jax.ShapeDtypeStruct((dec_batch, rows_a // 2, LANES), F32),
                   jax.ShapeDtypeStruct((dec_batch, rows_b, LANES), F32)],
        compiler_params=_cparams(("parallel", "arbitrary")),
        name="decode_attn",
    )(page_table.reshape(-1), *args)


def _merge_kernel(oa_ref, ob_ref, x_ref, g1_ref, sh2_ref, sc2_ref, sub_ref, wo_ref, n2_ref,
                  xp_ref, h2_ref, *, out_scale):
    oa = oa_ref[...]
    parts = []
    for h in range(HA):
        o = oa[:, h * DVA:(h + 1) * DVA]
        ms = jnp.mean(o * o, axis=-1, keepdims=True)
        parts.append(o * lax.rsqrt(ms + EPS) * sub_ref[...] * out_scale)
    o = jnp.concatenate(parts + [ob_ref[...]], axis=1).astype(BF16)
    y = jnp.dot(o, wo_ref[...], preferred_element_type=F32)
    xp = x_ref[...] + g1_ref[...] * y
    xp_ref[...] = xp
    ms = jnp.mean(xp * xp, axis=-1, keepdims=True)
    h2 = xp * lax.rsqrt(ms + EPS) * n2_ref[...]
    h2_ref[...] = h2 * (1.0 + sc2_ref[...]) + sh2_ref[...]


def _merge(oa, ob, x2d, g1, sh2, sc2, sub_g, w_out16, n2_g, per_row, rows_per_mod, tm, out_scale):
    n = x2d.shape[0]
    mod = _mod_spec(per_row, tm, rows_per_mod)
    row = lambda w: pl.BlockSpec((tm, w), lambda i: (i, 0))
    const = lambda a: pl.BlockSpec(a.shape, lambda i: (0,) * a.ndim)
    return pl.pallas_call(
        functools.partial(_merge_kernel, out_scale=out_scale),
        grid=(n // tm,),
        in_specs=[row(WA), row(WB), row(D_MODEL), mod, mod, mod, const(sub_g), const(w_out16),
                  const(n2_g)],
        out_specs=[row(D_MODEL), row(D_MODEL)],
        out_shape=[jax.ShapeDtypeStruct((n, D_MODEL), F32)] * 2,
        compiler_params=_cparams(("parallel",)),
        name="merge",
    )(oa, ob, x2d, g1, sh2, sc2, sub_g, w_out16, n2_g)


def _topk_kernel(h_ref, wq_ref, sk_ref, idx_ref, gate_ref, s_ref, ts_ref, ti_ref, be_ref, bg_ref, *, tb):
    hb = h_ref[...].astype(BF16)
    qt = lax.dot_general(wq_ref[...], hb, NT_DIMS, preferred_element_type=F32)
    for hp in range(2 * PEER_HEADS):
        sub = sk_ref[hp % 2]
        s_ref[hp] = jnp.dot(sub, qt[hp * N_KEYS:(hp + 1) * N_KEYS, :].astype(BF16),
                            preferred_element_type=F32)
    row_k = lax.broadcasted_iota(I32, (N_KEYS, tb), 0).astype(F32)

    def stage1(i, carry):
        hps = [LISTS_PER_TRIP * i + k for k in range(LISTS_PER_TRIP)]
        ss = [s_ref[hp] for hp in hps]
        for r in range(PEER_TOPK):
            for k, hp in enumerate(hps):
                s = ss[k]
                m = jnp.max(s, axis=0, keepdims=True)
                pick = jnp.min(jnp.where(s == m, row_k, float(N_KEYS)), axis=0, keepdims=True)
                ts_ref[hp, r:r + 1, :] = m
                ti_ref[hp, r:r + 1, :] = pick
                ss[k] = jnp.where(row_k == pick, NEG_INF, s)
        return carry

    lax.fori_loop(0, 2 * PEER_HEADS // LISTS_PER_TRIP, stage1, 0)
    half = PEER_TOPK // 2
    n_cand = PEER_TOPK + (half - 1) * half + half
    row_c = lax.broadcasted_iota(I32, (n_cand, tb), 0).astype(F32)

    def candidates(x, y, combine):
        rows = [combine(x[0:1, :], y)]
        rows += [combine(x[i:i + 1, :], y[0:half, :]) for i in range(1, half)]
        rows.append(combine(x[half:PEER_TOPK, :], y[0:1, :]))
        return jnp.concatenate(rows, axis=0)

    def stage2(i, carry):
        heads = [LISTS_PER_TRIP * i + k for k in range(LISTS_PER_TRIP)]
        css, ces = [], []
        for h in heads:
            a, b = ts_ref[2 * h], ts_ref[2 * h + 1]
            ia, ib = ti_ref[2 * h], ti_ref[2 * h + 1]
            css.append(candidates(a, b, lambda x, y: x + y))
            ces.append(candidates(ia, ib, lambda x, y: x * float(N_KEYS) + y))
        best = [[] for _ in heads]
        experts = [[] for _ in heads]
        for r in range(PEER_TOPK):
            for k in range(LISTS_PER_TRIP):
                cs = css[k]
                m = jnp.max(cs, axis=0, keepdims=True)
                pick = jnp.min(jnp.where(cs == m, row_c, float(n_cand)), axis=0, keepdims=True)
                sel = row_c == pick
                experts[k].append(jnp.max(jnp.where(sel, ces[k], -1.0), axis=0, keepdims=True))
                best[k].append(m)
                css[k] = jnp.where(sel, NEG_INF, cs)
        for k, h in enumerate(heads):
            bs = jnp.concatenate(best[k], axis=0)
            ex = jnp.exp(bs - bs[0:1, :])
            off = pl.multiple_of(h * PEER_TOPK, PEER_TOPK)
            bg_ref[pl.ds(off, PEER_TOPK), :] = ex / jnp.sum(ex, axis=0, keepdims=True)
            be_ref[pl.ds(off, PEER_TOPK), :] = jnp.concatenate(experts[k], axis=0)
        return carry

    lax.fori_loop(0, PEER_HEADS // LISTS_PER_TRIP, stage2, 0)
    idx_ref[...] = (be_ref[...].T * float(ROW_TILE)).astype(I32)
    gate_ref[...] = bg_ref[...].T


def _topk(h2, wq_t16, subkeys16, tb):
    n = h2.shape[0]
    const = lambda a: pl.BlockSpec(a.shape, lambda i: (0,) * a.ndim)
    return pl.pallas_call(
        functools.partial(_topk_kernel, tb=tb),
        grid=(n // tb,),
        in_specs=[pl.BlockSpec((tb, D_MODEL), lambda i: (i, 0)), const(wq_t16), const(subkeys16)],
        out_specs=[pl.BlockSpec((tb, N_SLOTS), lambda i: (i, 0))] * 2,
        out_shape=[jax.ShapeDtypeStruct((n, N_SLOTS), I32), jax.ShapeDtypeStruct((n, N_SLOTS), F32)],
        scratch_shapes=[pltpu.VMEM((2 * PEER_HEADS, N_KEYS, tb), F32),
                        pltpu.VMEM((2 * PEER_HEADS, PEER_TOPK, tb), F32),
                        pltpu.VMEM((2 * PEER_HEADS, PEER_TOPK, tb), F32),
                        pltpu.VMEM((N_SLOTS, tb), F32), pltpu.VMEM((N_SLOTS, tb), F32)],
        compiler_params=_cparams(("parallel",)),
        name="peer_topk",
    )(h2, wq_t16, subkeys16)


def _pack_kernel(t_ref, o_ref):
    x = t_ref[...]
    half = x.shape[1] // 2
    rows = x.shape[0]
    bits = lax.bitcast_convert_type(x.astype(BF16).astype(F32), I32)
    word = lax.shift_right_logical(bits[:, :half], 16) | bits[:, half:]
    for c in range(ROW_TILE):
        o_ref[pl.ds(c, rows, stride=ROW_TILE), :] = word[:, c * LANES:(c + 1) * LANES]


def _pack_table(t, rows=512):
    n_e, d = t.shape
    return pl.pallas_call(
        _pack_kernel,
        grid=(n_e // rows,),
        in_specs=[pl.BlockSpec((rows, d), lambda i: (i, 0))],
        out_specs=pl.BlockSpec((rows * ROW_TILE, LANES), lambda i: (i, 0)),
        out_shape=jax.ShapeDtypeStruct((n_e * ROW_TILE, LANES), I32),
        compiler_params=_cparams(("parallel",)),
        name="pack_table",
    )(t)


def _peer_u_kernel(idx_ref, h_ref, gate_ref, tbl_ref, mask_ref, sel_ref, w_ref, pair_ref, *stage_refs, tb):
    buf_a, buf_b, buf_c, buf_d = stage_refs

    def gather(t, stage):
        for e in range(N_SLOTS):
            row = pl.multiple_of(idx_ref[t, e], ROW_TILE)
            stage[e * ROW_TILE:(e + 1) * ROW_TILE, :] = tbl_ref[pl.ds(row, ROW_TILE), :]

    def contract(t, stage):
        h3 = jnp.concatenate(_split3(h_ref[t]), axis=0)
        acc = jnp.zeros((3 * SUBLANES, 2 * N_SLOTS), F32)
        for c in range(ROW_TILE):
            word = stage[pl.ds(c, N_SLOTS, stride=ROW_TILE), :]
            rhs = pltpu.bitcast(word, BF16)
            acc = acc + mask_ref[c] * lax.dot_general(h3, rhs, NT_DIMS, preferred_element_type=F32)
        pair_ref[pl.ds(t, 1), :] = jnp.sum(acc, axis=0, keepdims=True)

    for buf in (buf_c, buf_d):
        buf[...] = jnp.zeros(buf.shape, I32)

    def pair(t, fill, drain):
        gather(t, fill[0])
        gather(t + 1, fill[1])
        contract(jnp.maximum(t - 2, 0), drain[0])
        contract(jnp.maximum(t - 1, 0), drain[1])

    def tokens(i, carry):
        for k in range(PAIRS_PER_TRIP):
            t = 2 * (PAIRS_PER_TRIP * i + k)
            if k % 2 == 0:
                pair(t, (buf_a, buf_b), (buf_c, buf_d))
            else:
                pair(t, (buf_c, buf_d), (buf_a, buf_b))
        return carry

    lax.fori_loop(0, tb // (2 * PAIRS_PER_TRIP), tokens, 0)
    contract(tb - 2, buf_c)
    contract(tb - 1, buf_d)
    a = sum(jnp.dot(x, sel_ref[...], preferred_element_type=F32) for x in _split3(pair_ref[...]))
    w_ref[...] = gate_ref[...] * (0.5 * a * (1.0 + lax.erf(a * (2.0 ** -0.5))))


def _chunk_masks():
    m = np.zeros((ROW_TILE, 3 * SUBLANES, 2 * N_SLOTS), np.float32)
    for c in range(ROW_TILE):
        for term in range(3):
            m[c, term * SUBLANES + c, 0::2] = 1.0
            m[c, term * SUBLANES + ROW_TILE + c, 1::2] = 1.0
    sel = np.zeros((2 * N_SLOTS, N_SLOTS), np.float32)
    sel[np.arange(2 * N_SLOTS), np.arange(2 * N_SLOTS) // 2] = 1.0
    return jnp.asarray(m), jnp.asarray(sel, BF16)


def _peer_u(idx, h2r, gate, tbl, tb):
    n = idx.shape[0]
    masks, sel = _chunk_masks()
    return pl.pallas_call(
        functools.partial(_peer_u_kernel, tb=tb),
        grid=(n // tb,),
        in_specs=[pl.BlockSpec((tb, N_SLOTS), lambda i: (i, 0), memory_space=pltpu.SMEM),
                  pl.BlockSpec((tb, SUBLANES, LANES), lambda i: (i, 0, 0)),
                  pl.BlockSpec((tb, N_SLOTS), lambda i: (i, 0)),
                  pl.BlockSpec(memory_space=pltpu.VMEM),
                  pl.BlockSpec(masks.shape, lambda i: (0, 0, 0)),
                  pl.BlockSpec(sel.shape, lambda i: (0, 0))],
        out_specs=pl.BlockSpec((tb, N_SLOTS), lambda i: (i, 0)),
        out_shape=jax.ShapeDtypeStruct((n, N_SLOTS), F32),
        scratch_shapes=[pltpu.VMEM((tb, 2 * N_SLOTS), F32)]
        + [pltpu.VMEM((N_SLOTS * ROW_TILE, LANES), I32)] * 4,
        compiler_params=_cparams(("arbitrary",)),
        name="peer_u",
    )(idx, h2r, gate, tbl, masks, sel)


def _peer_v_kernel(idx_ref, w_ref, tbl_ref, elo_ref, ehi_ref, xp_ref, g2_ref, fg_ref, y_ref,
                   *scratch_refs, tb):
    o_ref = scratch_refs[0]
    wexp_refs = scratch_refs[1:3]
    buf_a, buf_b, buf_c, buf_d = scratch_refs[3:7]
    n_terms = 3

    for half_ref in wexp_refs:
        half_ref[...] = jnp.zeros(half_ref.shape, F32)
    for k, term in enumerate(_split3(w_ref[...])):
        for j, spread_ref in enumerate((elo_ref, ehi_ref)):
            spread = jnp.dot(term, spread_ref[...], preferred_element_type=F32)
            for half, half_ref in enumerate(wexp_refs):
                half_ref[pl.ds(j * n_terms + k, tb, stride=SUBLANES), :] = (
                    spread[:, half * LANES:(half + 1) * LANES])

    def gather(t, stage):
        for e in range(N_SLOTS):
            row = pl.multiple_of(idx_ref[t, e], ROW_TILE)
            stage[e * ROW_TILE:(e + 1) * ROW_TILE, :] = tbl_ref[pl.ds(row, ROW_TILE), :]

    def contract(t, stage):
        rows = pl.ds(pl.multiple_of(t * SUBLANES, SUBLANES), SUBLANES)
        lhs = jnp.concatenate([half_ref[rows, :] for half_ref in wexp_refs], axis=1).astype(BF16)
        lo, hi = [], []
        for c in range(ROW_TILE):
            rhs = pltpu.bitcast(stage[pl.ds(c, N_SLOTS, stride=ROW_TILE), :], BF16)
            r = jnp.dot(lhs, rhs, preferred_element_type=F32)
            lo.append(r[0:1] + r[1:2] + r[2:3])
            hi.append(r[3:4] + r[4:5] + r[5:6])
        o_ref[pl.ds(t, 1), :] = jnp.concatenate(lo + hi, axis=1)

    for buf in (buf_c, buf_d):
        buf[...] = jnp.zeros(buf.shape, I32)

    def pair(t, fill, drain):
        gather(t, fill[0])
        gather(t + 1, fill[1])
        contract(jnp.maximum(t - 2, 0), drain[0])
        contract(jnp.maximum(t - 1, 0), drain[1])

    def tokens(i, carry):
        for k in range(PAIRS_PER_TRIP):
            t = 2 * (PAIRS_PER_TRIP * i + k)
            if k % 2 == 0:
                pair(t, (buf_a, buf_b), (buf_c, buf_d))
            else:
                pair(t, (buf_c, buf_d), (buf_a, buf_b))
        return carry

    lax.fori_loop(0, tb // (2 * PAIRS_PER_TRIP), tokens, 0)
    contract(tb - 2, buf_c)
    contract(tb - 1, buf_d)
    x = xp_ref[...] + g2_ref[...] * o_ref[...]
    ms = jnp.mean(x * x, axis=-1, keepdims=True)
    y_ref[...] = x * lax.rsqrt(ms + EPS) * fg_ref[...]


def _half_spreads():
    lo = np.zeros((N_SLOTS, 2 * N_SLOTS), np.float32)
    hi = np.zeros((N_SLOTS, 2 * N_SLOTS), np.float32)
    lo[np.arange(N_SLOTS), 2 * np.arange(N_SLOTS)] = 1.0
    hi[np.arange(N_SLOTS), 2 * np.arange(N_SLOTS) + 1] = 1.0
    return jnp.asarray(lo, BF16), jnp.asarray(hi, BF16)


def _peer_v(idx, w, tbl, xp, g2, fg, per_row, rows_per_mod, tb):
    n = idx.shape[0]
    elo, ehi = _half_spreads()
    blk = pl.BlockSpec((tb, N_SLOTS), lambda i: (i, 0))
    row = pl.BlockSpec((tb, D_MODEL), lambda i: (i, 0))
    return pl.pallas_call(
        functools.partial(_peer_v_kernel, tb=tb),
        grid=(n // tb,),
        in_specs=[pl.BlockSpec((tb, N_SLOTS), lambda i: (i, 0), memory_space=pltpu.SMEM), blk,
                  pl.BlockSpec(memory_space=pltpu.VMEM),
                  pl.BlockSpec(elo.shape, lambda i: (0, 0)), pl.BlockSpec(ehi.shape, lambda i: (0, 0)),
                  row, _mod_spec(per_row, tb, rows_per_mod), pl.BlockSpec((1, D_MODEL), lambda i: (0, 0))],
        out_specs=row,
        out_shape=jax.ShapeDtypeStruct((n, D_MODEL), F32),
        scratch_shapes=[pltpu.VMEM((tb, D_MODEL), F32)]
        + [pltpu.VMEM((tb * SUBLANES, LANES), F32)] * 2
        + [pltpu.VMEM((N_SLOTS * ROW_TILE, LANES), I32)] * 4,
        compiler_params=_cparams(("arbitrary",)),
        name="peer_v",
    )(idx, w, tbl, elo, ehi, xp, g2, fg)


def _peer(h2, xp, g2, fg, wq_t16, subkeys16, u_tbl, v_tbl, per_row, rows_per_mod, tb_topk, tb_gather):
    n = h2.shape[0]
    idx, gate = _topk(h2, wq_t16, subkeys16, tb_topk)
    w = _peer_u(idx, h2.reshape(n, SUBLANES, LANES), gate, u_tbl, tb_gather)
    return _peer_v(idx, w, v_tbl, xp, g2, fg, per_row, rows_per_mod, tb_gather)


def kernel(x_prompt, x_sample, c_prompt, c_sample, cache_a_k, cache_a_v, cache_b_k, cache_b_v, cache_b_logf, page_table, w_ada, b_ada, norm1_g, w_in, b_f, lambda_q1, lambda_k1, lambda_q2, lambda_k2, subln_g, w_out, norm2_g, peer_wq, peer_subkeys, peer_u, peer_v, final_g):
    batch, seq_len, d = x_prompt.shape
    dec_batch, new_tokens, _ = x_sample.shape
    depth = w_ada.shape[0]
    assert depth == 1 and d == D_MODEL and new_tokens == 4
    n_p, n_s = batch * seq_len, dec_batch * new_tokens
    n_pool, page = cache_a_k.shape[1], cache_a_k.shape[2]
    l = 0
    lam_init = 0.8 - 0.6 * math.exp(-0.3 * l)

    n_c = batch + dec_batch
    pad = (-n_c) % SUBLANES
    c_all = jnp.concatenate([c_prompt, c_sample, jnp.zeros((pad, d), F32)], axis=0)
    ada = _ada(c_all, w_ada[l], b_ada[l])
    mods_p = [m.reshape(batch, 1, d) for m in jnp.split(ada[:batch], 6, axis=-1)]
    mods_s = [jnp.repeat(m, new_tokens, axis=0) for m in jnp.split(ada[batch:n_c], 6, axis=-1)]
    lam = _lam(lambda_q1[l], lambda_k1[l], lambda_q2[l], lambda_k2[l], lam_init)

    n_main = 3 * WA + 3 * WB
    w_main = w_in[l][:, :n_main].astype(BF16)
    w_f = jnp.pad(w_in[l][:, n_main:], ((0, 0), (0, LANES - HB))).astype(BF16)
    bf = jnp.pad(b_f[l], (0, LANES - HB)).reshape(1, LANES)
    g1n = norm1_g[l].reshape(1, d)
    g2n = norm2_g[l].reshape(1, d)
    sub_g = subln_g[l].reshape(1, DVA)
    w_out16 = w_out[l].astype(BF16)
    wq_t16 = peer_wq[l].T.astype(BF16)
    subkeys16 = peer_subkeys[l].astype(BF16)
    u_tbl = _pack_table(peer_u[l])
    v_tbl = _pack_table(peer_v[l])
    fg = final_g.reshape(1, d)

    tm_p = 256
    xp2d = x_prompt.reshape(n_p, d)
    sh1, sc1, g1, sh2, sc2, g2 = mods_p
    (ka, va, kb, vb, lf, qa16, ka16, va16, qb16, kb16, vb16, qbias, kbias) = _proj(
        xp2d, sh1, sc1, g1n, w_main, w_f, bf, False, seq_len, seq_len, tm_p)
    qbias_a, kbias_a = _alibi_bias(seq_len)
    tq = 1024
    oa = _attention(qa16, qbias_a, ka16, kbias_a, va16, lam, batch, seq_len, tq, fox=False)
    ob = _attention(qb16, qbias, kb16, kbias, vb16, lam, batch, seq_len, tq, fox=True)
    xp_mid, h2 = _merge(oa, ob, xp2d, g1, sh2, sc2, sub_g, w_out16, g2n, False, seq_len, tm_p,
                        1.0 - lam_init)
    y_prompt = _peer(h2, xp_mid, g2, fg, wq_t16, subkeys16, u_tbl, v_tbl, False, seq_len,
                     256, 128).reshape(batch, seq_len, d)

    tm_s = n_s
    xs2d = x_sample.reshape(n_s, d)
    sh1, sc1, g1, sh2, sc2, g2 = mods_s
    (ka_s, va_s, kb_s, vb_s, lf_s, qa16_s, _, _, qb16_s, _, _, _, _) = _proj(
        xs2d, sh1, sc1, g1n, w_main, w_f, bf, True, n_s, n_s, tm_s)
    assert page == LANES
    qa5 = qa16_s.reshape(dec_batch, new_tokens, HA, 2, DA)
    zeros = jnp.zeros_like(qa5[:, :, :, 0, :])
    wqa = jnp.stack([jnp.concatenate([qa5[:, :, :, 0, :], zeros], axis=-1),
                     jnp.concatenate([zeros, qa5[:, :, :, 1, :]], axis=-1)], axis=3)
    wqa = jnp.pad(wqa, ((0, 0), (0, QPAD - new_tokens), (0, 0), (0, 0), (0, 0)))
    wqa = wqa.transpose(0, 2, 3, 1, 4).reshape(dec_batch, HA * 2 * QPAD, 2 * DA)
    qb4 = qb16_s.reshape(dec_batch, new_tokens, HB, DB)
    wqb = (qb4[:, :, :, None, :] * jnp.eye(HB, dtype=BF16)[None, None, :, :, None])
    wqb = wqb.reshape(dec_batch, new_tokens * HB, WB)

    def pos_major(c):
        return c.reshape(n_pool, page * HA, c.shape[-1])

    def pos_minor(c):
        return c.transpose(0, 2, 3, 1).reshape(n_pool, HB * DB, page)

    caches = [pos_major(cache_a_k[l]), pos_major(cache_a_v[l]), pos_minor(cache_b_k[l]),
              pos_minor(cache_b_v[l]), cache_b_logf[l].transpose(0, 2, 1)]

    def new_pos_major(a):
        a = a.reshape(dec_batch, new_tokens * HA, DVA)
        return jnp.pad(a, ((0, 0), (0, (page - new_tokens) * HA), (0, 0)))

    def new_pos_minor(a, heads):
        w = a.shape[1] // heads
        a = a.reshape(dec_batch, new_tokens, heads, w).transpose(0, 2, 3, 1)
        return jnp.pad(a, ((0, 0), (0, 0), (0, 0), (0, page - new_tokens))).reshape(dec_batch, heads * w, page)

    new_pages = [new_pos_major(ka_s), new_pos_major(va_s), new_pos_minor(kb_s, HB),
                 new_pos_minor(vb_s, HB), new_pos_minor(lf_s, HB)]
    oa_s, ob_s = _decode_attention(page_table, caches, new_pages, wqa, wqb, lam, new_tokens, 8)
    oa_s = oa_s.reshape(dec_batch, HA, QPAD, DVA)[:, :, :new_tokens].transpose(0, 2, 1, 3).reshape(n_s, WA)
    ob_s = ob_s.reshape(dec_batch, new_tokens, HB, 2, DB)
    ob_s = jnp.where((jnp.arange(HB) % 2 == 0)[None, None, :, None], ob_s[:, :, :, 0, :], ob_s[:, :, :, 1, :])
    ob_s = ob_s.reshape(n_s, WB)
    xs_mid, h2_s = _merge(oa_s, ob_s, xs2d, g1, sh2, sc2, sub_g, w_out16, g2n, True, n_s, tm_s,
                          1.0 - lam_init)
    y_sample = _peer(h2_s, xs_mid, g2, fg, wq_t16, subkeys16, u_tbl, v_tbl, True, n_s,
                     n_s, 128).reshape(dec_batch, new_tokens, d)

    def kv(a, b, t, h, w):
        return a.reshape(1, b, t, h, w)

    return (y_prompt, y_sample,
            kv(ka, batch, seq_len, HA, 2 * DA), kv(va, batch, seq_len, HA, DVA),
            kv(kb, batch, seq_len, HB, DB), kv(vb, batch, seq_len, HB, DB),
            lf.reshape(1, batch, seq_len, HB),
            kv(ka_s, dec_batch, new_tokens, HA, 2 * DA), kv(va_s, dec_batch, new_tokens, HA, DVA),
            kv(kb_s, dec_batch, new_tokens, HB, DB), kv(vb_s, dec_batch, new_tokens, HB, DB),
            lf_s.reshape(1, dec_batch, new_tokens, HB))
```

```python
import functools
import math

import jax
import jax.numpy as jnp
import numpy as np
from jax import lax
from jax.experimental import pallas as pl
from jax.experimental.pallas import tpu as pltpu

F32, BF16, I32 = jnp.float32, jnp.bfloat16, jnp.int32
EPS = 1e-6
D_MODEL = 1024
HA, DA, DVA = 4, 64, 128
HB, DB = 8, 64
WA, WB = HA * DVA, HB * DB
N_KEYS = 128
PEER_HEADS = 8
PEER_TOPK = 16
N_SLOTS = PEER_HEADS * PEER_TOPK
N_EXPERTS = N_KEYS * N_KEYS
LANES = 128
SUBLANES = 8
ROW_TILE = D_MODEL // (2 * LANES)
QPAD = SUBLANES
PAIRS_PER_TRIP = 32
LISTS_PER_TRIP = 2
NEG_INF = float("-inf")
VMEM_LIMIT = 56 * 1024 * 1024
NT_DIMS = (((1,), (1,)), ((), ()))


def _cparams(sem):
    return pltpu.CompilerParams(dimension_semantics=sem, vmem_limit_bytes=VMEM_LIMIT)


def _split3(x):
    hi = x.astype(BF16)
    r = x - hi.astype(F32)
    mid = r.astype(BF16)
    lo = (r - mid.astype(F32)).astype(BF16)
    return hi, mid, lo


def _ada_kernel(c_ref, w_ref, b_ref, o_ref):
    c = c_ref[...]
    s = c * (1.0 / (1.0 + jnp.exp(-c)))
    o_ref[...] = jnp.dot(s.astype(BF16), w_ref[...].astype(BF16),
                         preferred_element_type=F32) + b_ref[...]


def _ada(c_all, w, b):
    r = c_all.shape[0]
    n_out = w.shape[1]
    return pl.pallas_call(
        _ada_kernel,
        grid=(n_out // D_MODEL,),
        in_specs=[pl.BlockSpec((r, D_MODEL), lambda j: (0, 0)),
                  pl.BlockSpec((D_MODEL, D_MODEL), lambda j: (0, j)),
                  pl.BlockSpec((1, D_MODEL), lambda j: (0, j))],
        out_specs=pl.BlockSpec((r, D_MODEL), lambda j: (0, j)),
        out_shape=jax.ShapeDtypeStruct((r, n_out), F32),
        compiler_params=_cparams(("parallel",)),
        name="ada",
    )(c_all, w, b.reshape(1, n_out))


def _lam_kernel(v_ref, o_ref, *, lam_init):
    v = v_ref[...]
    a = jnp.sum(v[0:1] * v[1:2], axis=1, keepdims=True)
    b = jnp.sum(v[2:3] * v[3:4], axis=1, keepdims=True)
    lam = jnp.exp(a) - jnp.exp(b) + lam_init
    o_ref[...] = jnp.broadcast_to(lam, o_ref.shape)


def _lam(lq1, lk1, lq2, lk2, lam_init):
    v = jnp.zeros((SUBLANES, LANES), F32)
    v = v.at[0:4, 0:DA].set(jnp.stack([lq1, lk1, lq2, lk2]))
    return pl.pallas_call(
        functools.partial(_lam_kernel, lam_init=lam_init),
        out_shape=jax.ShapeDtypeStruct((SUBLANES, LANES), F32),
        name="lam",
    )(v)


def _mod_spec(per_row, tm, rows_per_mod):
    if per_row:
        return pl.BlockSpec((tm, D_MODEL), lambda i: (i, 0))
    blocks = rows_per_mod // tm
    return pl.BlockSpec((None, 1, D_MODEL), lambda i: (i // blocks, 0, 0))


def _proj_kernel(x_ref, sh_ref, sc_ref, g_ref, w_ref, wf_ref, bf_ref, tri_ref, pq_ref, pk_ref,
                 qc_ref, kc_ref,
                 ka_ref, va_ref, kb_ref, vb_ref, lf_ref,
                 qa16_ref, ka16_ref, va16_ref, qb16_ref, kb16_ref, vb16_ref, qbias_ref, kbias_ref,
                 carry_ref, *, blocks_per_seq):
    i = pl.program_id(0)

    @pl.when(i % blocks_per_seq == 0)
    def _():
        carry_ref[...] = jnp.zeros_like(carry_ref)

    x = x_ref[...]
    ms = jnp.mean(x * x, axis=-1, keepdims=True)
    h = x * lax.rsqrt(ms + EPS) * g_ref[...]
    h = h * (1.0 + sc_ref[...]) + sh_ref[...]
    hb = h.astype(BF16)
    p = jnp.dot(hb, w_ref[...], preferred_element_type=F32)
    qa, ka, va = p[:, 0:WA], p[:, WA:2 * WA], p[:, 2 * WA:3 * WA]
    o = 3 * WA
    qb, kb, vb = p[:, o:o + WB], p[:, o + WB:o + 2 * WB], p[:, o + 2 * WB:o + 3 * WB]
    tm = x.shape[0]
    for hd in range(HA):
        ka_ref[pl.ds(hd, tm, stride=HA), :] = ka[:, hd * DVA:(hd + 1) * DVA]
        va_ref[pl.ds(hd, tm, stride=HA), :] = va[:, hd * DVA:(hd + 1) * DVA]
    kb_ref[...] = kb
    vb_ref[...] = vb
    qa16_ref[...] = (qa * (DA ** -0.5)).astype(BF16)
    ka16_ref[...] = ka.astype(BF16)
    va16_ref[...] = va.astype(BF16)
    qb16_ref[...] = (qb * (DB ** -0.5)).astype(BF16)
    kb16_ref[...] = kb.astype(BF16)
    vb16_ref[...] = vb.astype(BF16)

    z = jnp.dot(hb, wf_ref[...], preferred_element_type=F32) + bf_ref[...]
    lf = jnp.minimum(z, 0.0) - jnp.log1p(jnp.exp(-jnp.abs(z)))
    lane = lax.broadcasted_iota(I32, lf.shape, 1)
    lf = jnp.where(lane < HB, lf, 0.0)
    lf_ref[...] = lf[:, 0:HB]

    tri = tri_ref[...]
    hi, mid, lo = _split3(lf)
    cum = (jnp.dot(tri, hi, preferred_element_type=F32)
           + jnp.dot(tri, mid, preferred_element_type=F32)
           + jnp.dot(tri, lo, preferred_element_type=F32)) + carry_ref[...]
    tm = cum.shape[0]
    carry_ref[...] = cum[tm - 1:tm, :]
    ccat = jnp.concatenate(_split3(cum), axis=1)
    qbias_ref[...] = (jnp.dot(ccat, pq_ref[...], preferred_element_type=F32) + qc_ref[...]).astype(BF16)
    kbias_ref[...] = (jnp.dot(ccat, pk_ref[...], preferred_element_type=F32) + kc_ref[...]).astype(BF16)


def _decay_placement():
    pq = np.zeros((3 * LANES, WB), np.float32)
    pk = np.zeros((3 * LANES, WB), np.float32)
    qc = np.zeros((1, WB), np.float32)
    kc = np.zeros((1, WB), np.float32)
    for h in range(HB):
        pair, which = divmod(h, 2)
        base = pair * LANES + which * 6
        for t in range(3):
            pq[t * LANES + h, base + t] = 1.0
            qc[0, base + 3 + t] = 1.0
            kc[0, base + t] = 1.0
            pk[t * LANES + h, base + 3 + t] = -1.0
    return (jnp.asarray(pq, BF16), jnp.asarray(pk, BF16), jnp.asarray(qc), jnp.asarray(kc))


def _proj(x2d, sh, sc, g, w_main, w_f, b_f, per_row, rows_per_mod, seq_len, tm):
    n = x2d.shape[0]
    nw = w_main.shape[1]
    tri = (np.arange(tm)[None, :] <= np.arange(tm)[:, None]).astype(np.float32)
    pq, pk, qc, kc = _decay_placement()
    mod = _mod_spec(per_row, tm, rows_per_mod)
    row = lambda w: pl.BlockSpec((tm, w), lambda i: (i, 0))
    const = lambda a: pl.BlockSpec(a.shape, lambda i: (0,) * a.ndim)
    tri = jnp.asarray(tri, BF16)
    head_rows = pl.BlockSpec((tm * HA, DVA), lambda i: (i, 0))
    outs = ([jax.ShapeDtypeStruct((n * HA, DVA), F32)] * 2 + [jax.ShapeDtypeStruct((n, WB), F32)] * 2
            + [jax.ShapeDtypeStruct((n, HB), F32)] + [jax.ShapeDtypeStruct((n, WA), BF16)] * 8)
    return pl.pallas_call(
        functools.partial(_proj_kernel, blocks_per_seq=seq_len // tm),
        grid=(n // tm,),
        in_specs=[row(D_MODEL), mod, mod, const(g), const(w_main), const(w_f), const(b_f),
                  const(tri), const(pq), const(pk), const(qc), const(kc)],
        out_specs=[head_rows] * 2 + [row(WB)] * 2 + [row(HB)] + [row(WA)] * 8,
        out_shape=outs,
        scratch_shapes=[pltpu.VMEM((1, LANES), F32)],
        compiler_params=_cparams(("arbitrary",)),
        name="proj",
    )(x2d, sh, sc, g, w_main, w_f, b_f, tri, pq, pk, qc, kc)


def _attn_kernel(q_ref, qb_ref, k_ref, kb_ref, v_ref, lam_ref, o_ref, m_ref, l_ref, acc_ref,
                 *, tq, fox):
    qi = pl.program_id(2)
    q = q_ref[...].astype(F32)
    qb = qb_ref[...].astype(F32)
    lane = lax.broadcasted_iota(I32, (tq, LANES), 1)
    q_lo = jnp.where(lane < DA, q, 0.0)
    q_hi = jnp.where(lane >= DA, q, 0.0)
    if fox:
        qb_lo = jnp.where(lane < 6, qb, 0.0)
        qb_hi = jnp.where((lane >= 6) & (lane < 12), qb, 0.0)
    else:
        qb_lo = qb_hi = qb
    qq = jnp.concatenate([jnp.concatenate([q_lo, qb_lo], axis=1),
                          jnp.concatenate([q_hi, qb_hi], axis=1)], axis=0).astype(BF16)
    m_ref[...] = jnp.full(m_ref.shape, NEG_INF, F32)
    l_ref[...] = jnp.zeros(l_ref.shape, F32)
    acc_ref[...] = jnp.zeros(acc_ref.shape, F32)

    n_chunks = tq // LANES

    def step(j, masked):
        off = pl.multiple_of(j * tq, tq)
        kk = jnp.concatenate([k_ref[pl.ds(off, tq), :], kb_ref[pl.ds(off, tq), :]], axis=1)
        s = lax.dot_general(qq, kk, NT_DIMS, preferred_element_type=F32)
        if masked:
            r = lax.broadcasted_iota(I32, s.shape, 0)
            c = lax.broadcasted_iota(I32, s.shape, 1)
            r = jnp.where(r >= tq, r - tq, r)
            s = jnp.where(c <= r, s, NEG_INF)
        chunks = [s[:, c * LANES:(c + 1) * LANES] for c in range(n_chunks)]
        m_prev = m_ref[...]
        m_cur = functools.reduce(jnp.maximum, chunks)
        m_new = jnp.maximum(m_prev, jnp.max(m_cur, axis=1, keepdims=True))
        alpha = jnp.exp(m_prev - m_new)
        ps = [jnp.exp(ch - m_new) for ch in chunks]
        l_ref[...] = alpha * l_ref[...] + functools.reduce(jnp.add, ps)
        p = jnp.concatenate([x.astype(BF16) for x in ps], axis=1)
        acc_ref[...] = alpha * acc_ref[...] + jnp.dot(p, v_ref[pl.ds(off, tq), :],
                                                      preferred_element_type=F32)
        m_ref[...] = m_new

    def body(j, carry):
        step(j, False)
        return carry

    lax.fori_loop(0, qi, body, 0)
    step(qi, True)
    o = acc_ref[...] / jnp.sum(l_ref[...], axis=1, keepdims=True)
    if fox:
        o_ref[...] = jnp.where(lane < DB, o[0:tq], o[tq:2 * tq])
    else:
        o_ref[...] = o[0:tq] - lam_ref[0:1, :] * o[tq:2 * tq]


def _attention(q16, qbias, k16, kbias, v16, lam, batch, seq_len, tq, fox):
    n = q16.shape[0]
    groups = q16.shape[1] // LANES
    nq = seq_len // tq
    q_spec = pl.BlockSpec((tq, LANES), lambda b, g, i: (b * nq + i, g))
    kv_spec = pl.BlockSpec((seq_len, LANES), lambda b, g, i: (b, g))
    if fox:
        qb_spec, kb_spec = q_spec, kv_spec
    else:
        qb_spec = pl.BlockSpec((None, tq, LANES), lambda b, g, i: (g, i, 0))
        kb_spec = pl.BlockSpec((None, seq_len, LANES), lambda b, g, i: (g, 0, 0))
    return pl.pallas_call(
        functools.partial(_attn_kernel, tq=tq, fox=fox),
        grid=(batch, groups, nq),
        in_specs=[q_spec, qb_spec, kv_spec, kb_spec, kv_spec,
                  pl.BlockSpec((SUBLANES, LANES), lambda b, g, i: (0, 0))],
        out_specs=q_spec,
        out_shape=jax.ShapeDtypeStruct((n, groups * LANES), F32),
        scratch_shapes=[pltpu.VMEM((2 * tq, LANES), F32)] * 3,
        compiler_params=_cparams(("parallel", "parallel", "arbitrary")),
        name="attn_fox" if fox else "attn_diff",
    )(q16, qbias, k16, kbias, v16, lam)


def _alibi_bias(seq_len):
    slopes = 2.0 ** (-8.0 * np.arange(1, HA + 1, dtype=np.float64) / HA)
    pos = np.arange(seq_len)
    hi, lo = (pos // LANES) * float(LANES), (pos % LANES).astype(np.float64)
    qb = np.zeros((HA, seq_len, LANES), np.float32)
    kb = np.zeros((HA, seq_len, LANES), np.float32)
    for h in range(HA):
        qb[h, :, 0], qb[h, :, 1], qb[h, :, 2], qb[h, :, 3] = -slopes[h] * hi, -slopes[h] * lo, 1.0, 1.0
        kb[h, :, 0], kb[h, :, 1], kb[h, :, 2], kb[h, :, 3] = 1.0, 1.0, slopes[h] * hi, slopes[h] * lo
    return jnp.asarray(qb, BF16), jnp.asarray(kb, BF16)


def _decode_kernel(pt_ref, *refs, pages_per_step, n_steps, page):
    del pt_ref
    pp = pages_per_step
    paged = [refs[c * pp:(c + 1) * pp] for c in range(5)]
    (nka_ref, nva_ref, nkb_ref, nvb_ref, nlf_ref, wqa_ref, wqb_ref, biasa_ref, slope_ref,
     maska_ref, maskb_ref, triu_ref, dmaskb_ref, lam_ref) = refs[5 * pp:5 * pp + 14]
    oa_ref, ob_ref = refs[5 * pp + 14:5 * pp + 16]
    (ma_ref, la_ref, acca_ref, mb_ref, lb_ref, accb_ref, carry_ref) = refs[5 * pp + 16:]
    j = pl.program_id(1)
    half_rows = HA * QPAD

    @pl.when(j == 0)
    def _():
        ma_ref[...] = jnp.full(ma_ref.shape, NEG_INF, F32)
        mb_ref[...] = jnp.full(mb_ref.shape, NEG_INF, F32)
        la_ref[...] = jnp.zeros(la_ref.shape, F32)
        lb_ref[...] = jnp.zeros(lb_ref.shape, F32)
        acca_ref[...] = jnp.zeros(acca_ref.shape, F32)
        accb_ref[...] = jnp.zeros(accb_ref.shape, F32)
        carry_ref[...] = jnp.zeros(carry_ref.shape, F32)

    def softmax_step(chunks, m_ref, l_ref):
        m_prev = m_ref[...]
        m_cur = functools.reduce(jnp.maximum, chunks)
        m_new = jnp.maximum(m_prev, jnp.max(m_cur, axis=1, keepdims=True))
        alpha = jnp.exp(m_prev - m_new)
        ps = [jnp.exp(s - m_new) for s in chunks]
        l_ref[...] = alpha * l_ref[...] + functools.reduce(jnp.add, ps)
        m_ref[...] = m_new
        return alpha, ps

    def process(pages):
        chunks = []
        for ka_r, _, _, _, _, page_idx, masked in pages:
            parts = []
            for h in range(HA):
                k_h = ka_r[pl.ds(h, page, stride=HA), :].astype(BF16)
                parts.append(lax.dot_general(wqa_ref[h * 2 * QPAD:(h + 1) * 2 * QPAD, :], k_h, NT_DIMS,
                                             preferred_element_type=F32))
            sa = jnp.concatenate([x[0:QPAD] for x in parts] + [x[QPAD:2 * QPAD] for x in parts], axis=0)
            sa = sa + biasa_ref[...] + slope_ref[...] * (page_idx * float(page))
            if masked:
                sa = jnp.where(maska_ref[...] > 0.0, sa, NEG_INF)
            chunks.append(sa)
        alpha, ps = softmax_step(chunks, ma_ref, la_ref)
        acc = alpha * acca_ref[...]
        for (_, va_r, _, _, _, _, _), p in zip(pages, ps):
            pv = []
            for h in range(HA):
                v_h = va_r[pl.ds(h, page, stride=HA), :].astype(BF16)
                p_h = jnp.concatenate([p[h * QPAD:(h + 1) * QPAD],
                                       p[half_rows + h * QPAD:half_rows + (h + 1) * QPAD]], axis=0)
                pv.append(jnp.dot(p_h.astype(BF16), v_h, preferred_element_type=F32))
            acc = acc + jnp.concatenate([x[0:QPAD] for x in pv] + [x[QPAD:2 * QPAD] for x in pv], axis=0)
        acca_ref[...] = acc
        triu = triu_ref[...]
        carry = carry_ref[...]
        chunks = []
        for _, _, kb_r, _, lf_r, _, masked in pages:
            hi, mid, lo = _split3(lf_r[...])
            cum = (jnp.dot(hi, triu, preferred_element_type=F32)
                   + jnp.dot(mid, triu, preferred_element_type=F32)
                   + jnp.dot(lo, triu, preferred_element_type=F32)) + carry
            carry = jnp.broadcast_to(cum[:, page - 1:page], carry.shape)
            sb = jnp.dot(wqb_ref[...], kb_r[...].astype(BF16), preferred_element_type=F32)
            sb = sb - jnp.concatenate([cum] * 4, axis=0)
            if masked:
                sb = jnp.where(maskb_ref[...] > 0.0, sb, NEG_INF)
            chunks.append(sb)
        carry_ref[...] = carry
        alpha, ps = softmax_step(chunks, mb_ref, lb_ref)
        acc = jnp.concatenate([alpha] * (WB // LANES), axis=1) * accb_ref[...]
        for (_, _, _, vb_r, _, _, _), p in zip(pages, ps):
            acc = acc + lax.dot_general(p.astype(BF16), vb_r[...].astype(BF16), NT_DIMS,
                                        preferred_element_type=F32)
        accb_ref[...] = acc

    cached = [(paged[0][k], paged[1][k], paged[2][k], paged[3][k], paged[4][k],
               (j * pp + k).astype(F32), False) for k in range(pp)]

    @pl.when(j < n_steps - 1)
    def _():
        process(cached)

    @pl.when(j == n_steps - 1)
    def _():
        process(cached + [(nka_ref, nva_ref, nkb_ref, nvb_ref, nlf_ref, float(n_steps * pp), True)])
        oa = acca_ref[...] / jnp.sum(la_ref[...], axis=1, keepdims=True)
        oa_ref[...] = oa[0:half_rows] - lam_ref[0:1, :] * oa[half_rows:2 * half_rows]
        ob = accb_ref[...] / jnp.sum(lb_ref[...], axis=1, keepdims=True) * dmaskb_ref[...]
        ob_ref[...] = ob[:, 0:128] + ob[:, 128:256] + ob[:, 256:384] + ob[:, 384:512]


def _decode_attention(page_table, caches, new_pages, wqa, wqb, lam, new_tokens, pages_per_step):
    dec_batch, n_pages = page_table.shape
    page = caches[4].shape[2]
    pp = pages_per_step
    n_steps = n_pages // pp
    past = n_pages * page
    rows_a = 2 * HA * QPAD
    rows_b = new_tokens * HB
    slopes = 2.0 ** (-8.0 * np.arange(1, HA + 1, dtype=np.float64) / HA)
    ra = np.arange(rows_a)
    ha, qa_i = (ra // QPAD) % HA, np.minimum(ra % QPAD, new_tokens - 1)
    col = np.arange(page)
    biasa = slopes[ha][:, None] * (col[None, :] - past - qa_i[:, None])
    slope_rep = np.repeat(slopes[ha][:, None], page, axis=1)
    maska = (col[None, :] <= qa_i[:, None]).astype(np.float32)
    rb = np.arange(rows_b)
    maskb = (col[None, :] <= (rb // HB)[:, None]).astype(np.float32)
    triu = (np.arange(page)[:, None] <= np.arange(page)[None, :]).astype(np.float32)
    cb = np.arange(WB)
    dmaskb = (cb[None, :] // DB == (rb % HB)[:, None]).astype(np.float32)
    consts = [jnp.asarray(biasa, F32), jnp.asarray(slope_rep, F32), jnp.asarray(maska),
              jnp.asarray(maskb), jnp.asarray(triu, BF16), jnp.asarray(dmaskb), lam]

    def page_spec(shape, k):
        return pl.BlockSpec((None,) + shape,
                            lambda b, j, pt: (pt[b * n_pages + j * pp + k], 0, 0))

    in_specs, args = [], []
    for c in caches:
        for k in range(pp):
            in_specs.append(page_spec(c.shape[1:], k))
            args.append(c)
    for a in new_pages + [wqa, wqb]:
        in_specs.append(pl.BlockSpec((None,) + a.shape[1:], lambda b, j, pt: (b, 0, 0)))
        args.append(a)
    for a in consts:
        in_specs.append(pl.BlockSpec(a.shape, lambda b, j, pt: (0, 0)))
        args.append(a)
    grid_spec = pltpu.PrefetchScalarGridSpec(
        num_scalar_prefetch=1,
        grid=(dec_batch, n_steps),
        in_specs=in_specs,
        out_specs=[pl.BlockSpec((None, rows_a // 2, LANES), lambda b, j, pt: (b, 0, 0)),
                   pl.BlockSpec((None, rows_b, LANES), lambda b, j, pt: (b, 0, 0))],
        scratch_shapes=[pltpu.VMEM((rows_a, LANES), F32), pltpu.VMEM((rows_a, LANES), F32),
                        pltpu.VMEM((rows_a, DVA), F32),
                        pltpu.VMEM((rows_b, LANES), F32), pltpu.VMEM((rows_b, LANES), F32),
                        pltpu.VMEM((rows_b, WB), F32),
                        pltpu.VMEM((HB, page), F32)],
    )
    return pl.pallas_call(
        functools.partial(_decode_kernel, pages_per_step=pp, n_steps=n_steps, page=page),
        grid_spec=grid_spec,
        out_shape=[jax.ShapeDtypeStruct((dec_batch, rows_a // 2, LANES), F32),
                   jax.ShapeDtypeStruct((dec_batch, rows_b, LANES), F32)],
        compiler_params=_cparams(("parallel", "arbitrary")),
        name="decode_attn",
    )(page_table.reshape(-1), *args)


def _merge_kernel(oa_ref, ob_ref, x_ref, g1_ref, sh2_ref, sc2_ref, sub_ref, wo_ref, n2_ref,
                  xp_ref, h2_ref, *, out_scale):
    oa = oa_ref[...]
    parts = []
    for h in range(HA):
        o = oa[:, h * DVA:(h + 1) * DVA]
        ms = jnp.mean(o * o, axis=-1, keepdims=True)
        parts.append(o * lax.rsqrt(ms + EPS) * sub_ref[...] * out_scale)
    o = jnp.concatenate(parts + [ob_ref[...]], axis=1).astype(BF16)
    y = jnp.dot(o, wo_ref[...], preferred_element_type=F32)
    xp = x_ref[...] + g1_ref[...] * y
    xp_ref[...] = xp
    ms = jnp.mean(xp * xp, axis=-1, keepdims=True)
    h2 = xp * lax.rsqrt(ms + EPS) * n2_ref[...]
    h2_ref[...] = h2 * (1.0 + sc2_ref[...]) + sh2_ref[...]


def _merge(oa, ob, x2d, g1, sh2, sc2, sub_g, w_out16, n2_g, per_row, rows_per_mod, tm, out_scale):
    n = x2d.shape[0]
    mod = _mod_spec(per_row, tm, rows_per_mod)
    row = lambda w: pl.BlockSpec((tm, w), lambda i: (i, 0))
    const = lambda a: pl.BlockSpec(a.shape, lambda i: (0,) * a.ndim)
    return pl.pallas_call(
        functools.partial(_merge_kernel, out_scale=out_scale),
        grid=(n // tm,),
        in_specs=[row(WA), row(WB), row(D_MODEL), mod, mod, mod, const(sub_g), const(w_out16),
                  const(n2_g)],
        out_specs=[row(D_MODEL), row(D_MODEL)],
        out_shape=[jax.ShapeDtypeStruct((n, D_MODEL), F32)] * 2,
        compiler_params=_cparams(("parallel",)),
        name="merge",
    )(oa, ob, x2d, g1, sh2, sc2, sub_g, w_out16, n2_g)


def _topk_kernel(h_ref, wq_ref, sk_ref, idx_ref, gate_ref, s_ref, ts_ref, ti_ref, be_ref, bg_ref, *, tb):
    hb = h_ref[...].astype(BF16)
    qt = lax.dot_general(wq_ref[...], hb, NT_DIMS, preferred_element_type=F32)
    for hp in range(2 * PEER_HEADS):
        sub = sk_ref[hp % 2]
        s_ref[hp] = jnp.dot(sub, qt[hp * N_KEYS:(hp + 1) * N_KEYS, :].astype(BF16),
                            preferred_element_type=F32)
    row_k = lax.broadcasted_iota(I32, (N_KEYS, tb), 0).astype(F32)

    def stage1(i, carry):
        hps = [LISTS_PER_TRIP * i + k for k in range(LISTS_PER_TRIP)]
        ss = [s_ref[hp] for hp in hps]
        for r in range(PEER_TOPK):
            for k, hp in enumerate(hps):
                s = ss[k]
                m = jnp.max(s, axis=0, keepdims=True)
                pick = jnp.min(jnp.where(s == m, row_k, float(N_KEYS)), axis=0, keepdims=True)
                ts_ref[hp, r:r + 1, :] = m
                ti_ref[hp, r:r + 1, :] = pick
                ss[k] = jnp.where(row_k == pick, NEG_INF, s)
        return carry

    lax.fori_loop(0, 2 * PEER_HEADS // LISTS_PER_TRIP, stage1, 0)
    half = PEER_TOPK // 2
    n_cand = PEER_TOPK + (half - 1) * half + half
    row_c = lax.broadcasted_iota(I32, (n_cand, tb), 0).astype(F32)

    def candidates(x, y, combine):
        rows = [combine(x[0:1, :], y)]
        rows += [combine(x[i:i + 1, :], y[0:half, :]) for i in range(1, half)]
        rows.append(combine(x[half:PEER_TOPK, :], y[0:1, :]))
        return jnp.concatenate(rows, axis=0)

    def stage2(i, carry):
        heads = [LISTS_PER_TRIP * i + k for k in range(LISTS_PER_TRIP)]
        css, ces = [], []
        for h in heads:
            a, b = ts_ref[2 * h], ts_ref[2 * h + 1]
            ia, ib = ti_ref[2 * h], ti_ref[2 * h + 1]
            css.append(candidates(a, b, lambda x, y: x + y))
            ces.append(candidates(ia, ib, lambda x, y: x * float(N_KEYS) + y))
        best = [[] for _ in heads]
        experts = [[] for _ in heads]
        for r in range(PEER_TOPK):
            for k in range(LISTS_PER_TRIP):
                cs = css[k]
                m = jnp.max(cs, axis=0, keepdims=True)
                pick = jnp.min(jnp.where(cs == m, row_c, float(n_cand)), axis=0, keepdims=True)
                sel = row_c == pick
                experts[k].append(jnp.max(jnp.where(sel, ces[k], -1.0), axis=0, keepdims=True))
                best[k].append(m)
                css[k] = jnp.where(sel, NEG_INF, cs)
        for k, h in enumerate(heads):
            bs = jnp.concatenate(best[k], axis=0)
            ex = jnp.exp(bs - bs[0:1, :])
            off = pl.multiple_of(h * PEER_TOPK, PEER_TOPK)
            bg_ref[pl.ds(off, PEER_TOPK), :] = ex / jnp.sum(ex, axis=0, keepdims=True)
            be_ref[pl.ds(off, PEER_TOPK), :] = jnp.concatenate(experts[k], axis=0)
        return carry

    lax.fori_loop(0, PEER_HEADS // LISTS_PER_TRIP, stage2, 0)
    idx_ref[...] = (be_ref[...].T * float(ROW_TILE)).astype(I32)
    gate_ref[...] = bg_ref[...].T


def _topk(h2, wq_t16, subkeys16, tb):
    n = h2.shape[0]
    const = lambda a: pl.BlockSpec(a.shape, lambda i: (0,) * a.ndim)
    return pl.pallas_call(
        functools.partial(_topk_kernel, tb=tb),
        grid=(n // tb,),
        in_specs=[pl.BlockSpec((tb, D_MODEL), lambda i: (i, 0)), const(wq_t16), const(subkeys16)],
        out_specs=[pl.BlockSpec((tb, N_SLOTS), lambda i: (i, 0))] * 2,
        out_shape=[jax.ShapeDtypeStruct((n, N_SLOTS), I32), jax.ShapeDtypeStruct((n, N_SLOTS), F32)],
        scratch_shapes=[pltpu.VMEM((2 * PEER_HEADS, N_KEYS, tb), F32),
                        pltpu.VMEM((2 * PEER_HEADS, PEER_TOPK, tb), F32),
                        pltpu.VMEM((2 * PEER_HEADS, PEER_TOPK, tb), F32),
                        pltpu.VMEM((N_SLOTS, tb), F32), pltpu.VMEM((N_SLOTS, tb), F32)],
        compiler_params=_cparams(("parallel",)),
        name="peer_topk",
    )(h2, wq_t16, subkeys16)


def _pack_kernel(t_ref, o_ref):
    x = t_ref[...]
    half = x.shape[1] // 2
    rows = x.shape[0]
    bits = lax.bitcast_convert_type(x.astype(BF16).astype(F32), I32)
    word = lax.shift_right_logical(bits[:, :half], 16) | bits[:, half:]
    for c in range(ROW_TILE):
        o_ref[pl.ds(c, rows, stride=ROW_TILE), :] = word[:, c * LANES:(c + 1) * LANES]


def _pack_table(t, rows=512):
    n_e, d = t.shape
    return pl.pallas_call(
        _pack_kernel,
        grid=(n_e // rows,),
        in_specs=[pl.BlockSpec((rows, d), lambda i: (i, 0))],
        out_specs=pl.BlockSpec((rows * ROW_TILE, LANES), lambda i: (i, 0)),
        out_shape=jax.ShapeDtypeStruct((n_e * ROW_TILE, LANES), I32),
        compiler_params=_cparams(("parallel",)),
        name="pack_table",
    )(t)


def _peer_u_kernel(idx_ref, h_ref, gate_ref, tbl_ref, mask_ref, sel_ref, w_ref, pair_ref, *stage_refs, tb):
    buf_a, buf_b, buf_c, buf_d = stage_refs

    def gather(t, stage):
        for e in range(N_SLOTS):
            row = pl.multiple_of(idx_ref[t, e], ROW_TILE)
            stage[e * ROW_TILE:(e + 1) * ROW_TILE, :] = tbl_ref[pl.ds(row, ROW_TILE), :]

    def contract(t, stage):
        h3 = jnp.concatenate(_split3(h_ref[t]), axis=0)
        acc = jnp.zeros((3 * SUBLANES, 2 * N_SLOTS), F32)
        for c in range(ROW_TILE):
            word = stage[pl.ds(c, N_SLOTS, stride=ROW_TILE), :]
            rhs = pltpu.bitcast(word, BF16)
            acc = acc + mask_ref[c] * lax.dot_general(h3, rhs, NT_DIMS, preferred_element_type=F32)
        pair_ref[pl.ds(t, 1), :] = jnp.sum(acc, axis=0, keepdims=True)

    for buf in (buf_c, buf_d):
        buf[...] = jnp.zeros(buf.shape, I32)

    def pair(t, fill, drain):
        gather(t, fill[0])
        gather(t + 1, fill[1])
        contract(jnp.maximum(t - 2, 0), drain[0])
        contract(jnp.maximum(t - 1, 0), drain[1])

    def tokens(i, carry):
        for k in range(PAIRS_PER_TRIP):
            t = 2 * (PAIRS_PER_TRIP * i + k)
            if k % 2 == 0:
                pair(t, (buf_a, buf_b), (buf_c, buf_d))
            else:
                pair(t, (buf_c, buf_d), (buf_a, buf_b))
        return carry

    lax.fori_loop(0, tb // (2 * PAIRS_PER_TRIP), tokens, 0)
    contract(tb - 2, buf_c)
    contract(tb - 1, buf_d)
    a = sum(jnp.dot(x, sel_ref[...], preferred_element_type=F32) for x in _split3(pair_ref[...]))
    w_ref[...] = gate_ref[...] * (0.5 * a * (1.0 + lax.erf(a * (2.0 ** -0.5))))


def _chunk_masks():
    m = np.zeros((ROW_TILE, 3 * SUBLANES, 2 * N_SLOTS), np.float32)
    for c in range(ROW_TILE):
        for term in range(3):
            m[c, term * SUBLANES + c, 0::2] = 1.0
            m[c, term * SUBLANES + ROW_TILE + c, 1::2] = 1.0
    sel = np.zeros((2 * N_SLOTS, N_SLOTS), np.float32)
    sel[np.arange(2 * N_SLOTS), np.arange(2 * N_SLOTS) // 2] = 1.0
    return jnp.asarray(m), jnp.asarray(sel, BF16)


def _peer_u(idx, h2r, gate, tbl, tb):
    n = idx.shape[0]
    masks, sel = _chunk_masks()
    return pl.pallas_call(
        functools.partial(_peer_u_kernel, tb=tb),
        grid=(n // tb,),
        in_specs=[pl.BlockSpec((tb, N_SLOTS), lambda i: (i, 0), memory_space=pltpu.SMEM),
                  pl.BlockSpec((tb, SUBLANES, LANES), lambda i: (i, 0, 0)),
                  pl.BlockSpec((tb, N_SLOTS), lambda i: (i, 0)),
                  pl.BlockSpec(memory_space=pltpu.VMEM),
                  pl.BlockSpec(masks.shape, lambda i: (0, 0, 0)),
                  pl.BlockSpec(sel.shape, lambda i: (0, 0))],
        out_specs=pl.BlockSpec((tb, N_SLOTS), lambda i: (i, 0)),
        out_shape=jax.ShapeDtypeStruct((n, N_SLOTS), F32),
        scratch_shapes=[pltpu.VMEM((tb, 2 * N_SLOTS), F32)]
        + [pltpu.VMEM((N_SLOTS * ROW_TILE, LANES), I32)] * 4,
        compiler_params=_cparams(("arbitrary",)),
        name="peer_u",
    )(idx, h2r, gate, tbl, masks, sel)


def _peer_v_kernel(idx_ref, w_ref, tbl_ref, elo_ref, ehi_ref, xp_ref, g2_ref, fg_ref, y_ref,
                   *scratch_refs, tb):
    o_ref = scratch_refs[0]
    wexp_refs = scratch_refs[1:3]
    buf_a, buf_b, buf_c, buf_d = scratch_refs[3:7]
    n_terms = 3

    for half_ref in wexp_refs:
        half_ref[...] = jnp.zeros(half_ref.shape, F32)
    for k, term in enumerate(_split3(w_ref[...])):
        for j, spread_ref in enumerate((elo_ref, ehi_ref)):
            spread = jnp.dot(term, spread_ref[...], preferred_element_type=F32)
            for half, half_ref in enumerate(wexp_refs):
                half_ref[pl.ds(j * n_terms + k, tb, stride=SUBLANES), :] = (
                    spread[:, half * LANES:(half + 1) * LANES])

    def gather(t, stage):
        for e in range(N_SLOTS):
            row = pl.multiple_of(idx_ref[t, e], ROW_TILE)
            stage[e * ROW_TILE:(e + 1) * ROW_TILE, :] = tbl_ref[pl.ds(row, ROW_TILE), :]

    def contract(t, stage):
        rows = pl.ds(pl.multiple_of(t * SUBLANES, SUBLANES), SUBLANES)
        lhs = jnp.concatenate([half_ref[rows, :] for half_ref in wexp_refs], axis=1).astype(BF16)
        lo, hi = [], []
        for c in range(ROW_TILE):
            rhs = pltpu.bitcast(stage[pl.ds(c, N_SLOTS, stride=ROW_TILE), :], BF16)
            r = jnp.dot(lhs, rhs, preferred_element_type=F32)
            lo.append(r[0:1] + r[1:2] + r[2:3])
            hi.append(r[3:4] + r[4:5] + r[5:6])
        o_ref[pl.ds(t, 1), :] = jnp.concatenate(lo + hi, axis=1)

    for buf in (buf_c, buf_d):
        buf[...] = jnp.zeros(buf.shape, I32)

    def pair(t, fill, drain):
        gather(t, fill[0])
        gather(t + 1, fill[1])
        contract(jnp.maximum(t - 2, 0), drain[0])
        contract(jnp.maximum(t - 1, 0), drain[1])

    def tokens(i, carry):
        for k in range(PAIRS_PER_TRIP):
            t = 2 * (PAIRS_PER_TRIP * i + k)
            if k % 2 == 0:
                pair(t, (buf_a, buf_b), (buf_c, buf_d))
            else:
                pair(t, (buf_c, buf_d), (buf_a, buf_b))
        return carry

    lax.fori_loop(0, tb // (2 * PAIRS_PER_TRIP), tokens, 0)
    contract(tb - 2, buf_c)
    contract(tb - 1, buf_d)
    x = xp_ref[...] + g2_ref[...] * o_ref[...]
    ms = jnp.mean(x * x, axis=-1, keepdims=True)
    y_ref[...] = x * lax.rsqrt(ms + EPS) * fg_ref[...]


def _half_spreads():
    lo = np.zeros((N_SLOTS, 2 * N_SLOTS), np.float32)
    hi = np.zeros((N_SLOTS, 2 * N_SLOTS), np.float32)
    lo[np.arange(N_SLOTS), 2 * np.arange(N_SLOTS)] = 1.0
    hi[np.arange(N_SLOTS), 2 * np.arange(N_SLOTS) + 1] = 1.0
    return jnp.asarray(lo, BF16), jnp.asarray(hi, BF16)


def _peer_v(idx, w, tbl, xp, g2, fg, per_row, rows_per_mod, tb):
    n = idx.shape[0]
    elo, ehi = _half_spreads()
    blk = pl.BlockSpec((tb, N_SLOTS), lambda i: (i, 0))
    row = pl.BlockSpec((tb, D_MODEL), lambda i: (i, 0))
    return pl.pallas_call(
        functools.partial(_peer_v_kernel, tb=tb),
        grid=(n // tb,),
        in_specs=[pl.BlockSpec((tb, N_SLOTS), lambda i: (i, 0), memory_space=pltpu.SMEM), blk,
                  pl.BlockSpec(memory_space=pltpu.VMEM),
                  pl.BlockSpec(elo.shape, lambda i: (0, 0)), pl.BlockSpec(ehi.shape, lambda i: (0, 0)),
                  row, _mod_spec(per_row, tb, rows_per_mod), pl.BlockSpec((1, D_MODEL), lambda i: (0, 0))],
        out_specs=row,
        out_shape=jax.ShapeDtypeStruct((n, D_MODEL), F32),
        scratch_shapes=[pltpu.VMEM((tb, D_MODEL), F32)]
        + [pltpu.VMEM((tb * SUBLANES, LANES), F32)] * 2
        + [pltpu.VMEM((N_SLOTS * ROW_TILE, LANES), I32)] * 4,
        compiler_params=_cparams(("arbitrary",)),
        name="peer_v",
    )(idx, w, tbl, elo, ehi, xp, g2, fg)


def _peer(h2, xp, g2, fg, wq_t16, subkeys16, u_tbl, v_tbl, per_row, rows_per_mod, tb_topk, tb_gather):
    n = h2.shape[0]
    idx, gate = _topk(h2, wq_t16, subkeys16, tb_topk)
    w = _peer_u(idx, h2.reshape(n, SUBLANES, LANES), gate, u_tbl, tb_gather)
    return _peer_v(idx, w, v_tbl, xp, g2, fg, per_row, rows_per_mod, tb_gather)


def kernel(x_prompt, x_sample, c_prompt, c_sample, cache_a_k, cache_a_v, cache_b_k, cache_b_v, cache_b_logf, page_table, w_ada, b_ada, norm1_g, w_in, b_f, lambda_q1, lambda_k1, lambda_q2, lambda_k2, subln_g, w_out, norm2_g, peer_wq, peer_subkeys, peer_u, peer_v, final_g):
    batch, seq_len, d = x_prompt.shape
    dec_batch, new_tokens, _ = x_sample.shape
    depth = w_ada.shape[0]
    assert depth == 1 and d == D_MODEL and new_tokens == 4
    n_p, n_s = batch * seq_len, dec_batch * new_tokens
    n_pool, page = cache_a_k.shape[1], cache_a_k.shape[2]
    l = 0
    lam_init = 0.8 - 0.6 * math.exp(-0.3 * l)

    n_c = batch + dec_batch
    pad = (-n_c) % SUBLANES
    c_all = jnp.concatenate([c_prompt, c_sample, jnp.zeros((pad, d), F32)], axis=0)
    ada = _ada(c_all, w_ada[l], b_ada[l])
    mods_p = [m.reshape(batch, 1, d) for m in jnp.split(ada[:batch], 6, axis=-1)]
    mods_s = [jnp.repeat(m, new_tokens, axis=0) for m in jnp.split(ada[batch:n_c], 6, axis=-1)]
    lam = _lam(lambda_q1[l], lambda_k1[l], lambda_q2[l], lambda_k2[l], lam_init)

    n_main = 3 * WA + 3 * WB
    w_main = w_in[l][:, :n_main].astype(BF16)
    w_f = jnp.pad(w_in[l][:, n_main:], ((0, 0), (0, LANES - HB))).astype(BF16)
    bf = jnp.pad(b_f[l], (0, LANES - HB)).reshape(1, LANES)
    g1n = norm1_g[l].reshape(1, d)
    g2n = norm2_g[l].reshape(1, d)
    sub_g = subln_g[l].reshape(1, DVA)
    w_out16 = w_out[l].astype(BF16)
    wq_t16 = peer_wq[l].T.astype(BF16)
    subkeys16 = peer_subkeys[l].astype(BF16)
    u_tbl = _pack_table(peer_u[l])
    v_tbl = _pack_table(peer_v[l])
    fg = final_g.reshape(1, d)

    tm_p = 256
    xp2d = x_prompt.reshape(n_p, d)
    sh1, sc1, g1, sh2, sc2, g2 = mods_p
    (ka, va, kb, vb, lf, qa16, ka16, va16, qb16, kb16, vb16, qbias, kbias) = _proj(
        xp2d, sh1, sc1, g1n, w_main, w_f, bf, False, seq_len, seq_len, tm_p)
    qbias_a, kbias_a = _alibi_bias(seq_len)
    tq = 1024
    oa = _attention(qa16, qbias_a, ka16, kbias_a, va16, lam, batch, seq_len, tq, fox=False)
    ob = _attention(qb16, qbias, kb16, kbias, vb16, lam, batch, seq_len, tq, fox=True)
    xp_mid, h2 = _merge(oa, ob, xp2d, g1, sh2, sc2, sub_g, w_out16, g2n, False, seq_len, tm_p,
                        1.0 - lam_init)
    y_prompt = _peer(h2, xp_mid, g2, fg, wq_t16, subkeys16, u_tbl, v_tbl, False, seq_len,
                     256, 128).reshape(batch, seq_len, d)

    tm_s = n_s
    xs2d = x_sample.reshape(n_s, d)
    sh1, sc1, g1, sh2, sc2, g2 = mods_s
    (ka_s, va_s, kb_s, vb_s, lf_s, qa16_s, _, _, qb16_s, _, _, _, _) = _proj(
        xs2d, sh1, sc1, g1n, w_main, w_f, bf, True, n_s, n_s, tm_s)
    assert page == LANES
    qa5 = qa16_s.reshape(dec_batch, new_tokens, HA, 2, DA)
    zeros = jnp.zeros_like(qa5[:, :, :, 0, :])
    wqa = jnp.stack([jnp.concatenate([qa5[:, :, :, 0, :], zeros], axis=-1),
                     jnp.concatenate([zeros, qa5[:, :, :, 1, :]], axis=-1)], axis=3)
    wqa = jnp.pad(wqa, ((0, 0), (0, QPAD - new_tokens), (0, 0), (0, 0), (0, 0)))
    wqa = wqa.transpose(0, 2, 3, 1, 4).reshape(dec_batch, HA * 2 * QPAD, 2 * DA)
    qb4 = qb16_s.reshape(dec_batch, new_tokens, HB, DB)
    wqb = (qb4[:, :, :, None, :] * jnp.eye(HB, dtype=BF16)[None, None, :, :, None])
    wqb = wqb.reshape(dec_batch, new_tokens * HB, WB)

    def pos_major(c):
        return c.reshape(n_pool, page * HA, c.shape[-1])

    def pos_minor(c):
        return c.transpose(0, 2, 3, 1).reshape(n_pool, HB * DB, page)

    caches = [pos_major(cache_a_k[l]), pos_major(cache_a_v[l]), pos_minor(cache_b_k[l]),
              pos_minor(cache_b_v[l]), cache_b_logf[l].transpose(0, 2, 1)]

    def new_pos_major(a):
        a = a.reshape(dec_batch, new_tokens * HA, DVA)
        return jnp.pad(a, ((0, 0), (0, (page - new_tokens) * HA), (0, 0)))

    def new_pos_minor(a, heads):
        w = a.shape[1] // heads
        a = a.reshape(dec_batch, new_tokens, heads, w).transpose(0, 2, 3, 1)
        return jnp.pad(a, ((0, 0), (0, 0), (0, 0), (0, page - new_tokens))).reshape(dec_batch, heads * w, page)

    new_pages = [new_pos_major(ka_s), new_pos_major(va_s), new_pos_minor(kb_s, HB),
                 new_pos_minor(vb_s, HB), new_pos_minor(lf_s, HB)]
    oa_s, ob_s = _decode_attention(page_table, caches, new_pages, wqa, wqb, lam, new_tokens, 16)
    oa_s = oa_s.reshape(dec_batch, HA, QPAD, DVA)[:, :, :new_tokens].transpose(0, 2, 1, 3).reshape(n_s, WA)
    ob_s = ob_s.reshape(dec_batch, new_tokens, HB, 2, DB)
    ob_s = jnp.where((jnp.arange(HB) % 2 == 0)[None, None, :, None], ob_s[:, :, :, 0, :], ob_s[:, :, :, 1, :])
    ob_s = ob_s.reshape(n_s, WB)
    xs_mid, h2_s = _merge(oa_s, ob_s, xs2d, g1, sh2, sc2, sub_g, w_out16, g2n, True, n_s, tm_s,
                          1.0 - lam_init)
    y_sample = _peer(h2_s, xs_mid, g2, fg, wq_t16, subkeys16, u_tbl, v_tbl, True, n_s,
                     n_s, 128).reshape(dec_batch, new_tokens, d)

    def kv(a, b, t, h, w):
        return a.reshape(1, b, t, h, w)

    return (y_prompt, y_sample,
            kv(ka, batch, seq_len, HA, 2 * DA), kv(va, batch, seq_len, HA, DVA),
            kv(kb, batch, seq_len, HB, DB), kv(vb, batch, seq_len, HB, DB),
            lf.reshape(1, batch, seq_len, HB),
            kv(ka_s, dec_batch, new_tokens, HA, 2 * DA), kv(va_s, dec_batch, new_tokens, HA, DVA),
            kv(kb_s, dec_batch, new_tokens, HB, DB), kv(vb_s, dec_batch, new_tokens, HB, DB),
            lf_s.reshape(1, dec_batch, new_tokens, HB))
```

```python
import functools
import math

import jax
import jax.numpy as jnp
import numpy as np
from jax import lax
from jax.experimental import pallas as pl
from jax.experimental.pallas import tpu as pltpu

F32, BF16, I32 = jnp.float32, jnp.bfloat16, jnp.int32
EPS = 1e-6
D_MODEL = 1024
HA, DA, DVA = 4, 64, 128
HB, DB = 8, 64
WA, WB = HA * DVA, HB * DB
N_KEYS = 128
PEER_HEADS = 8
PEER_TOPK = 16
N_SLOTS = PEER_HEADS * PEER_TOPK
N_EXPERTS = N_KEYS * N_KEYS
LANES = 128
SUBLANES = 8
ROW_TILE = D_MODEL // (2 * LANES)
QPAD = SUBLANES
PAIRS_PER_TRIP = 32
LISTS_PER_TRIP = 2
NEG_INF = float("-inf")
VMEM_LIMIT = 56 * 1024 * 1024
NT_DIMS = (((1,), (1,)), ((), ()))


def _cparams(sem):
    return pltpu.CompilerParams(dimension_semantics=sem, vmem_limit_bytes=VMEM_LIMIT)


def _split3(x):
    hi = x.astype(BF16)
    r = x - hi.astype(F32)
    mid = r.astype(BF16)
    lo = (r - mid.astype(F32)).astype(BF16)
    return hi, mid, lo


def _ada_kernel(c_ref, w_ref, b_ref, o_ref):
    c = c_ref[...]
    s = c * (1.0 / (1.0 + jnp.exp(-c)))
    o_ref[...] = jnp.dot(s.astype(BF16), w_ref[...].astype(BF16),
                         preferred_element_type=F32) + b_ref[...]


def _ada(c_all, w, b):
    r = c_all.shape[0]
    n_out = w.shape[1]
    return pl.pallas_call(
        _ada_kernel,
        grid=(n_out // D_MODEL,),
        in_specs=[pl.BlockSpec((r, D_MODEL), lambda j: (0, 0)),
                  pl.BlockSpec((D_MODEL, D_MODEL), lambda j: (0, j)),
                  pl.BlockSpec((1, D_MODEL), lambda j: (0, j))],
        out_specs=pl.BlockSpec((r, D_MODEL), lambda j: (0, j)),
        out_shape=jax.ShapeDtypeStruct((r, n_out), F32),
        compiler_params=_cparams(("parallel",)),
        name="ada",
    )(c_all, w, b.reshape(1, n_out))


def _lam_kernel(v_ref, o_ref, *, lam_init):
    v = v_ref[...]
    a = jnp.sum(v[0:1] * v[1:2], axis=1, keepdims=True)
    b = jnp.sum(v[2:3] * v[3:4], axis=1, keepdims=True)
    lam = jnp.exp(a) - jnp.exp(b) + lam_init
    o_ref[...] = jnp.broadcast_to(lam, o_ref.shape)


def _lam(lq1, lk1, lq2, lk2, lam_init):
    v = jnp.zeros((SUBLANES, LANES), F32)
    v = v.at[0:4, 0:DA].set(jnp.stack([lq1, lk1, lq2, lk2]))
    return pl.pallas_call(
        functools.partial(_lam_kernel, lam_init=lam_init),
        out_shape=jax.ShapeDtypeStruct((SUBLANES, LANES), F32),
        name="lam",
    )(v)


def _mod_spec(per_row, tm, rows_per_mod):
    if per_row:
        return pl.BlockSpec((tm, D_MODEL), lambda i: (i, 0))
    blocks = rows_per_mod // tm
    return pl.BlockSpec((None, 1, D_MODEL), lambda i: (i // blocks, 0, 0))


def _proj_kernel(x_ref, sh_ref, sc_ref, g_ref, w_ref, wf_ref, bf_ref, tri_ref, pq_ref, pk_ref,
                 qc_ref, kc_ref,
                 ka_ref, va_ref, kb_ref, vb_ref, lf_ref,
                 qa16_ref, ka16_ref, va16_ref, qb16_ref, kb16_ref, vb16_ref, qbias_ref, kbias_ref,
                 carry_ref, *, blocks_per_seq):
    i = pl.program_id(0)

    @pl.when(i % blocks_per_seq == 0)
    def _():
        carry_ref[...] = jnp.zeros_like(carry_ref)

    x = x_ref[...]
    ms = jnp.mean(x * x, axis=-1, keepdims=True)
    h = x * lax.rsqrt(ms + EPS) * g_ref[...]
    h = h * (1.0 + sc_ref[...]) + sh_ref[...]
    hb = h.astype(BF16)
    p = jnp.dot(hb, w_ref[...], preferred_element_type=F32)
    qa, ka, va = p[:, 0:WA], p[:, WA:2 * WA], p[:, 2 * WA:3 * WA]
    o = 3 * WA
    qb, kb, vb = p[:, o:o + WB], p[:, o + WB:o + 2 * WB], p[:, o + 2 * WB:o + 3 * WB]
    tm = x.shape[0]
    for hd in range(HA):
        ka_ref[pl.ds(hd, tm, stride=HA), :] = ka[:, hd * DVA:(hd + 1) * DVA]
        va_ref[pl.ds(hd, tm, stride=HA), :] = va[:, hd * DVA:(hd + 1) * DVA]
    kb_ref[...] = kb
    vb_ref[...] = vb
    qa16_ref[...] = (qa * (DA ** -0.5)).astype(BF16)
    ka16_ref[...] = ka.astype(BF16)
    va16_ref[...] = va.astype(BF16)
    qb16_ref[...] = (qb * (DB ** -0.5)).astype(BF16)
    kb16_ref[...] = kb.astype(BF16)
    vb16_ref[...] = vb.astype(BF16)

    z = jnp.dot(hb, wf_ref[...], preferred_element_type=F32) + bf_ref[...]
    lf = jnp.minimum(z, 0.0) - jnp.log1p(jnp.exp(-jnp.abs(z)))
    lane = lax.broadcasted_iota(I32, lf.shape, 1)
    lf = jnp.where(lane < HB, lf, 0.0)
    lf_ref[...] = lf[:, 0:HB]

    tri = tri_ref[...]
    hi, mid, lo = _split3(lf)
    cum = (jnp.dot(tri, hi, preferred_element_type=F32)
           + jnp.dot(tri, mid, preferred_element_type=F32)
           + jnp.dot(tri, lo, preferred_element_type=F32)) + carry_ref[...]
    tm = cum.shape[0]
    carry_ref[...] = cum[tm - 1:tm, :]
    ccat = jnp.concatenate(_split3(cum), axis=1)
    qbias_ref[...] = (jnp.dot(ccat, pq_ref[...], preferred_element_type=F32) + qc_ref[...]).astype(BF16)
    kbias_ref[...] = (jnp.dot(ccat, pk_ref[...], preferred_element_type=F32) + kc_ref[...]).astype(BF16)


def _decay_placement():
    pq = np.zeros((3 * LANES, WB), np.float32)
    pk = np.zeros((3 * LANES, WB), np.float32)
    qc = np.zeros((1, WB), np.float32)
    kc = np.zeros((1, WB), np.float32)
    for h in range(HB):
        pair, which = divmod(h, 2)
        base = pair * LANES + which * 6
        for t in range(3):
            pq[t * LANES + h, base + t] = 1.0
            qc[0, base + 3 + t] = 1.0
            kc[0, base + t] = 1.0
            pk[t * LANES + h, base + 3 + t] = -1.0
    return (jnp.asarray(pq, BF16), jnp.asarray(pk, BF16), jnp.asarray(qc), jnp.asarray(kc))


def _proj(x2d, sh, sc, g, w_main, w_f, b_f, per_row, rows_per_mod, seq_len, tm):
    n = x2d.shape[0]
    nw = w_main.shape[1]
    tri = (np.arange(tm)[None, :] <= np.arange(tm)[:, None]).astype(np.float32)
    pq, pk, qc, kc = _decay_placement()
    mod = _mod_spec(per_row, tm, rows_per_mod)
    row = lambda w: pl.BlockSpec((tm, w), lambda i: (i, 0))
    const = lambda a: pl.BlockSpec(a.shape, lambda i: (0,) * a.ndim)
    tri = jnp.asarray(tri, BF16)
    head_rows = pl.BlockSpec((tm * HA, DVA), lambda i: (i, 0))
    outs = ([jax.ShapeDtypeStruct((n * HA, DVA), F32)] * 2 + [jax.ShapeDtypeStruct((n, WB), F32)] * 2
            + [jax.ShapeDtypeStruct((n, HB), F32)] + [jax.ShapeDtypeStruct((n, WA), BF16)] * 8)
    return pl.pallas_call(
        functools.partial(_proj_kernel, blocks_per_seq=seq_len // tm),
        grid=(n // tm,),
        in_specs=[row(D_MODEL), mod, mod, const(g), const(w_main), const(w_f), const(b_f),
                  const(tri), const(pq), const(pk), const(qc), const(kc)],
        out_specs=[head_rows] * 2 + [row(WB)] * 2 + [row(HB)] + [row(WA)] * 8,
        out_shape=outs,
        scratch_shapes=[pltpu.VMEM((1, LANES), F32)],
        compiler_params=_cparams(("arbitrary",)),
        name="proj",
    )(x2d, sh, sc, g, w_main, w_f, b_f, tri, pq, pk, qc, kc)


def _attn_kernel(q_ref, qb_ref, k_ref, kb_ref, v_ref, lam_ref, o_ref, m_ref, l_ref, acc_ref,
                 *, tq, fox):
    qi = pl.program_id(2)
    q = q_ref[...].astype(F32)
    qb = qb_ref[...].astype(F32)
    lane = lax.broadcasted_iota(I32, (tq, LANES), 1)
    q_lo = jnp.where(lane < DA, q, 0.0)
    q_hi = jnp.where(lane >= DA, q, 0.0)
    if fox:
        qb_lo = jnp.where(lane < 6, qb, 0.0)
        qb_hi = jnp.where((lane >= 6) & (lane < 12), qb, 0.0)
    else:
        qb_lo = qb_hi = qb
    qq = jnp.concatenate([jnp.concatenate([q_lo, qb_lo], axis=1),
                          jnp.concatenate([q_hi, qb_hi], axis=1)], axis=0).astype(BF16)
    m_ref[...] = jnp.full(m_ref.shape, NEG_INF, F32)
    l_ref[...] = jnp.zeros(l_ref.shape, F32)
    acc_ref[...] = jnp.zeros(acc_ref.shape, F32)

    n_chunks = tq // LANES

    def step(j, masked):
        off = pl.multiple_of(j * tq, tq)
        kk = jnp.concatenate([k_ref[pl.ds(off, tq), :], kb_ref[pl.ds(off, tq), :]], axis=1)
        s = lax.dot_general(qq, kk, NT_DIMS, preferred_element_type=F32)
        if masked:
            r = lax.broadcasted_iota(I32, s.shape, 0)
            c = lax.broadcasted_iota(I32, s.shape, 1)
            r = jnp.where(r >= tq, r - tq, r)
            s = jnp.where(c <= r, s, NEG_INF)
        chunks = [s[:, c * LANES:(c + 1) * LANES] for c in range(n_chunks)]
        m_prev = m_ref[...]
        m_cur = functools.reduce(jnp.maximum, chunks)
        m_new = jnp.maximum(m_prev, jnp.max(m_cur, axis=1, keepdims=True))
        alpha = jnp.exp(m_prev - m_new)
        ps = [jnp.exp(ch - m_new) for ch in chunks]
        l_ref[...] = alpha * l_ref[...] + functools.reduce(jnp.add, ps)
        p = jnp.concatenate([x.astype(BF16) for x in ps], axis=1)
        acc_ref[...] = alpha * acc_ref[...] + jnp.dot(p, v_ref[pl.ds(off, tq), :],
                                                      preferred_element_type=F32)
        m_ref[...] = m_new

    def body(j, carry):
        step(j, False)
        return carry

    lax.fori_loop(0, qi, body, 0)
    step(qi, True)
    o = acc_ref[...] / jnp.sum(l_ref[...], axis=1, keepdims=True)
    if fox:
        o_ref[...] = jnp.where(lane < DB, o[0:tq], o[tq:2 * tq])
    else:
        o_ref[...] = o[0:tq] - lam_ref[0:1, :] * o[tq:2 * tq]


def _attention(q16, qbias, k16, kbias, v16, lam, batch, seq_len, tq, fox):
    n = q16.shape[0]
    groups = q16.shape[1] // LANES
    nq = seq_len // tq
    q_spec = pl.BlockSpec((tq, LANES), lambda b, g, i: (b * nq + i, g))
    kv_spec = pl.BlockSpec((seq_len, LANES), lambda b, g, i: (b, g))
    if fox:
        qb_spec, kb_spec = q_spec, kv_spec
    else:
        qb_spec = pl.BlockSpec((None, tq, LANES), lambda b, g, i: (g, i, 0))
        kb_spec = pl.BlockSpec((None, seq_len, LANES), lambda b, g, i: (g, 0, 0))
    return pl.pallas_call(
        functools.partial(_attn_kernel, tq=tq, fox=fox),
        grid=(batch, groups, nq),
        in_specs=[q_spec, qb_spec, kv_spec, kb_spec, kv_spec,
                  pl.BlockSpec((SUBLANES, LANES), lambda b, g, i: (0, 0))],
        out_specs=q_spec,
        out_shape=jax.ShapeDtypeStruct((n, groups * LANES), F32),
        scratch_shapes=[pltpu.VMEM((2 * tq, LANES), F32)] * 3,
        compiler_params=_cparams(("parallel", "parallel", "arbitrary")),
        name="attn_fox" if fox else "attn_diff",
    )(q16, qbias, k16, kbias, v16, lam)


def _alibi_bias(seq_len):
    slopes = 2.0 ** (-8.0 * np.arange(1, HA + 1, dtype=np.float64) / HA)
    pos = np.arange(seq_len)
    hi, lo = (pos // LANES) * float(LANES), (pos % LANES).astype(np.float64)
    qb = np.zeros((HA, seq_len, LANES), np.float32)
    kb = np.zeros((HA, seq_len, LANES), np.float32)
    for h in range(HA):
        qb[h, :, 0], qb[h, :, 1], qb[h, :, 2], qb[h, :, 3] = -slopes[h] * hi, -slopes[h] * lo, 1.0, 1.0
        kb[h, :, 0], kb[h, :, 1], kb[h, :, 2], kb[h, :, 3] = 1.0, 1.0, slopes[h] * hi, slopes[h] * lo
    return jnp.asarray(qb, BF16), jnp.asarray(kb, BF16)


def _decode_kernel(pt_ref, *refs, pages_per_step, n_steps, page):
    del pt_ref
    pp = pages_per_step
    paged = [refs[c * pp:(c + 1) * pp] for c in range(5)]
    (nka_ref, nva_ref, nkb_ref, nvb_ref, nlf_ref, wqa_ref, wqb_ref, biasa_ref, slope_ref,
     maska_ref, maskb_ref, triu_ref, dmaskb_ref, lam_ref) = refs[5 * pp:5 * pp + 14]
    oa_ref, ob_ref = refs[5 * pp + 14:5 * pp + 16]
    (ma_ref, la_ref, acca_ref, mb_ref, lb_ref, accb_ref, carry_ref) = refs[5 * pp + 16:]
    j = pl.program_id(1)
    half_rows = HA * QPAD

    @pl.when(j == 0)
    def _():
        ma_ref[...] = jnp.full(ma_ref.shape, NEG_INF, F32)
        mb_ref[...] = jnp.full(mb_ref.shape, NEG_INF, F32)
        la_ref[...] = jnp.zeros(la_ref.shape, F32)
        lb_ref[...] = jnp.zeros(lb_ref.shape, F32)
        acca_ref[...] = jnp.zeros(acca_ref.shape, F32)
        accb_ref[...] = jnp.zeros(accb_ref.shape, F32)
        carry_ref[...] = jnp.zeros(carry_ref.shape, F32)

    def softmax_step(chunks, m_ref, l_ref):
        m_prev = m_ref[...]
        m_cur = functools.reduce(jnp.maximum, chunks)
        m_new = jnp.maximum(m_prev, jnp.max(m_cur, axis=1, keepdims=True))
        alpha = jnp.exp(m_prev - m_new)
        ps = [jnp.exp(s - m_new) for s in chunks]
        l_ref[...] = alpha * l_ref[...] + functools.reduce(jnp.add, ps)
        m_ref[...] = m_new
        return alpha, ps

    def process(pages):
        chunks = []
        for ka_r, _, _, _, _, page_idx, masked in pages:
            parts = []
            for h in range(HA):
                k_h = ka_r[pl.ds(h, page, stride=HA), :].astype(BF16)
                parts.append(lax.dot_general(wqa_ref[h * 2 * QPAD:(h + 1) * 2 * QPAD, :], k_h, NT_DIMS,
                                             preferred_element_type=F32))
            sa = jnp.concatenate([x[0:QPAD] for x in parts] + [x[QPAD:2 * QPAD] for x in parts], axis=0)
            sa = sa + biasa_ref[...] + slope_ref[...] * (page_idx * float(page))
            if masked:
                sa = jnp.where(maska_ref[...] > 0.0, sa, NEG_INF)
            chunks.append(sa)
        alpha, ps = softmax_step(chunks, ma_ref, la_ref)
        acc = alpha * acca_ref[...]
        for (_, va_r, _, _, _, _, _), p in zip(pages, ps):
            pv = []
            for h in range(HA):
                v_h = va_r[pl.ds(h, page, stride=HA), :].astype(BF16)
                p_h = jnp.concatenate([p[h * QPAD:(h + 1) * QPAD],
                                       p[half_rows + h * QPAD:half_rows + (h + 1) * QPAD]], axis=0)
                pv.append(jnp.dot(p_h.astype(BF16), v_h, preferred_element_type=F32))
            acc = acc + jnp.concatenate([x[0:QPAD] for x in pv] + [x[QPAD:2 * QPAD] for x in pv], axis=0)
        acca_ref[...] = acc
        triu = triu_ref[...]
        carry = carry_ref[...]
        chunks = []
        for _, _, kb_r, _, lf_r, _, masked in pages:
            hi, mid, lo = _split3(lf_r[...])
            cum = (jnp.dot(hi, triu, preferred_element_type=F32)
                   + jnp.dot(mid, triu, preferred_element_type=F32)
                   + jnp.dot(lo, triu, preferred_element_type=F32)) + carry
            carry = jnp.broadcast_to(cum[:, page - 1:page], carry.shape)
            sb = jnp.dot(wqb_ref[...], kb_r[...].astype(BF16), preferred_element_type=F32)
            sb = sb - jnp.concatenate([cum] * 4, axis=0)
            if masked:
                sb = jnp.where(maskb_ref[...] > 0.0, sb, NEG_INF)
            chunks.append(sb)
        carry_ref[...] = carry
        alpha, ps = softmax_step(chunks, mb_ref, lb_ref)
        acc = jnp.concatenate([alpha] * (WB // LANES), axis=1) * accb_ref[...]
        for (_, _, _, vb_r, _, _, _), p in zip(pages, ps):
            acc = acc + lax.dot_general(p.astype(BF16), vb_r[...].astype(BF16), NT_DIMS,
                                        preferred_element_type=F32)
        accb_ref[...] = acc

    cached = [(paged[0][k], paged[1][k], paged[2][k], paged[3][k], paged[4][k],
               (j * pp + k).astype(F32), False) for k in range(pp)]

    @pl.when(j < n_steps - 1)
    def _():
        process(cached)

    @pl.when(j == n_steps - 1)
    def _():
        process(cached + [(nka_ref, nva_ref, nkb_ref, nvb_ref, nlf_ref, float(n_steps * pp), True)])
        oa = acca_ref[...] / jnp.sum(la_ref[...], axis=1, keepdims=True)
        oa_ref[...] = oa[0:half_rows] - lam_ref[0:1, :] * oa[half_rows:2 * half_rows]
        ob = accb_ref[...] / jnp.sum(lb_ref[...], axis=1, keepdims=True) * dmaskb_ref[...]
        ob_ref[...] = ob[:, 0:128] + ob[:, 128:256] + ob[:, 256:384] + ob[:, 384:512]


def _decode_attention(page_table, caches, new_pages, wqa, wqb, lam, new_tokens, pages_per_step):
    dec_batch, n_pages = page_table.shape
    page = caches[4].shape[2]
    pp = pages_per_step
    n_steps = n_pages // pp
    past = n_pages * page
    rows_a = 2 * HA * QPAD
    rows_b = new_tokens * HB
    slopes = 2.0 ** (-8.0 * np.arange(1, HA + 1, dtype=np.float64) / HA)
    ra = np.arange(rows_a)
    ha, qa_i = (ra // QPAD) % HA, np.minimum(ra % QPAD, new_tokens - 1)
    col = np.arange(page)
    biasa = slopes[ha][:, None] * (col[None, :] - past - qa_i[:, None])
    slope_rep = np.repeat(slopes[ha][:, None], page, axis=1)
    maska = (col[None, :] <= qa_i[:, None]).astype(np.float32)
    rb = np.arange(rows_b)
    maskb = (col[None, :] <= (rb // HB)[:, None]).astype(np.float32)
    triu = (np.arange(page)[:, None] <= np.arange(page)[None, :]).astype(np.float32)
    cb = np.arange(WB)
    dmaskb = (cb[None, :] // DB == (rb % HB)[:, None]).astype(np.float32)
    consts = [jnp.asarray(biasa, F32), jnp.asarray(slope_rep, F32), jnp.asarray(maska),
              jnp.asarray(maskb), jnp.asarray(triu, BF16), jnp.asarray(dmaskb), lam]

    def page_spec(shape, k):
        return pl.BlockSpec((None,) + shape,
                            lambda b, j, pt: (pt[b * n_pages + j * pp + k], 0, 0))

    in_specs, args = [], []
    for c in caches:
        for k in range(pp):
            in_specs.append(page_spec(c.shape[1:], k))
            args.append(c)
    for a in new_pages + [wqa, wqb]:
        in_specs.append(pl.BlockSpec((None,) + a.shape[1:], lambda b, j, pt: (b, 0, 0)))
        args.append(a)
    for a in consts:
        in_specs.append(pl.BlockSpec(a.shape, lambda b, j, pt: (0, 0)))
        args.append(a)
    grid_spec = pltpu.PrefetchScalarGridSpec(
        num_scalar_prefetch=1,
        grid=(dec_batch, n_steps),
        in_specs=in_specs,
        out_specs=[pl.BlockSpec((None, rows_a // 2, LANES), lambda b, j, pt: (b, 0, 0)),
                   pl.BlockSpec((None, rows_b, LANES), lambda b, j, pt: (b, 0, 0))],
        scratch_shapes=[pltpu.VMEM((rows_a, LANES), F32), pltpu.VMEM((rows_a, LANES), F32),
                        pltpu.VMEM((rows_a, DVA), F32),
                        pltpu.VMEM((rows_b, LANES), F32), pltpu.VMEM((rows_b, LANES), F32),
                        pltpu.VMEM((rows_b, WB), F32),
                        pltpu.VMEM((HB, page), F32)],
    )
    return pl.pallas_call(
        functools.partial(_decode_kernel, pages_per_step=pp, n_steps=n_steps, page=page),
        grid_spec=grid_spec,
        out_shape=[jax.ShapeDtypeStruct((dec_batch, rows_a // 2, LANES), F32),
                   jax.ShapeDtypeStruct((dec_batch, rows_b, LANES), F32)],
        compiler_params=_cparams(("parallel", "arbitrary")),
        name="decode_attn",
    )(page_table.reshape(-1), *args)


def _merge_kernel(oa_ref, ob_ref, x_ref, g1_ref, sh2_ref, sc2_ref, sub_ref, wo_ref, n2_ref,
                  xp_ref, h2_ref, h2t_ref, *, out_scale):
    oa = oa_ref[...]
    parts = []
    for h in range(HA):
        o = oa[:, h * DVA:(h + 1) * DVA]
        ms = jnp.mean(o * o, axis=-1, keepdims=True)
        parts.append(o * lax.rsqrt(ms + EPS) * sub_ref[...] * out_scale)
    o = jnp.concatenate(parts + [ob_ref[...]], axis=1).astype(BF16)
    y = jnp.dot(o, wo_ref[...], preferred_element_type=F32)
    xp = x_ref[...] + g1_ref[...] * y
    xp_ref[...] = xp
    ms = jnp.mean(xp * xp, axis=-1, keepdims=True)
    h2 = xp * lax.rsqrt(ms + EPS) * n2_ref[...]
    h2 = h2 * (1.0 + sc2_ref[...]) + sh2_ref[...]
    h2_ref[...] = h2
    tm = h2.shape[0]
    for c in range(SUBLANES):
        h2t_ref[pl.ds(c, tm, stride=SUBLANES), :] = h2[:, c * LANES:(c + 1) * LANES]


def _merge(oa, ob, x2d, g1, sh2, sc2, sub_g, w_out16, n2_g, per_row, rows_per_mod, tm, out_scale):
    n = x2d.shape[0]
    mod = _mod_spec(per_row, tm, rows_per_mod)
    row = lambda w: pl.BlockSpec((tm, w), lambda i: (i, 0))
    const = lambda a: pl.BlockSpec(a.shape, lambda i: (0,) * a.ndim)
    return pl.pallas_call(
        functools.partial(_merge_kernel, out_scale=out_scale),
        grid=(n // tm,),
        in_specs=[row(WA), row(WB), row(D_MODEL), mod, mod, mod, const(sub_g), const(w_out16),
                  const(n2_g)],
        out_specs=[row(D_MODEL), row(D_MODEL), pl.BlockSpec((tm * SUBLANES, LANES), lambda i: (i, 0))],
        out_shape=[jax.ShapeDtypeStruct((n, D_MODEL), F32)] * 2
        + [jax.ShapeDtypeStruct((n * SUBLANES, LANES), F32)],
        compiler_params=_cparams(("parallel",)),
        name="merge",
    )(oa, ob, x2d, g1, sh2, sc2, sub_g, w_out16, n2_g)


def _topk_kernel(h_ref, wq_ref, sk_ref, idx_ref, gate_ref, s_ref, ts_ref, ti_ref, be_ref, bg_ref, *, tb):
    hb = h_ref[...].astype(BF16)
    qt = lax.dot_general(wq_ref[...], hb, NT_DIMS, preferred_element_type=F32)
    for hp in range(2 * PEER_HEADS):
        sub = sk_ref[hp % 2]
        s_ref[hp] = jnp.dot(sub, qt[hp * N_KEYS:(hp + 1) * N_KEYS, :].astype(BF16),
                            preferred_element_type=F32)
    row_k = lax.broadcasted_iota(I32, (N_KEYS, tb), 0).astype(F32)

    def stage1(i, carry):
        hps = [LISTS_PER_TRIP * i + k for k in range(LISTS_PER_TRIP)]
        ss = [s_ref[hp] for hp in hps]
        for r in range(PEER_TOPK):
            for k, hp in enumerate(hps):
                s = ss[k]
                m = jnp.max(s, axis=0, keepdims=True)
                pick = jnp.min(jnp.where(s == m, row_k, float(N_KEYS)), axis=0, keepdims=True)
                ts_ref[hp, r:r + 1, :] = m
                ti_ref[hp, r:r + 1, :] = pick
                ss[k] = jnp.where(row_k == pick, NEG_INF, s)
        return carry

    lax.fori_loop(0, 2 * PEER_HEADS // LISTS_PER_TRIP, stage1, 0)
    half = PEER_TOPK // 2
    n_cand = PEER_TOPK + (half - 1) * half + half
    row_c = lax.broadcasted_iota(I32, (n_cand, tb), 0).astype(F32)

    def candidates(x, y, combine):
        rows = [combine(x[0:1, :], y)]
        rows += [combine(x[i:i + 1, :], y[0:half, :]) for i in range(1, half)]
        rows.append(combine(x[half:PEER_TOPK, :], y[0:1, :]))
        return jnp.concatenate(rows, axis=0)

    def stage2(i, carry):
        heads = [LISTS_PER_TRIP * i + k for k in range(LISTS_PER_TRIP)]
        css, ces = [], []
        for h in heads:
            a, b = ts_ref[2 * h], ts_ref[2 * h + 1]
            ia, ib = ti_ref[2 * h], ti_ref[2 * h + 1]
            css.append(candidates(a, b, lambda x, y: x + y))
            ces.append(candidates(ia, ib, lambda x, y: x * float(N_KEYS) + y))
        best = [[] for _ in heads]
        experts = [[] for _ in heads]
        for r in range(PEER_TOPK):
            for k in range(LISTS_PER_TRIP):
                cs = css[k]
                m = jnp.max(cs, axis=0, keepdims=True)
                pick = jnp.min(jnp.where(cs == m, row_c, float(n_cand)), axis=0, keepdims=True)
                sel = row_c == pick
                experts[k].append(jnp.max(jnp.where(sel, ces[k], -1.0), axis=0, keepdims=True))
                best[k].append(m)
                css[k] = jnp.where(sel, NEG_INF, cs)
        for k, h in enumerate(heads):
            bs = jnp.concatenate(best[k], axis=0)
            ex = jnp.exp(bs - bs[0:1, :])
            off = pl.multiple_of(h * PEER_TOPK, PEER_TOPK)
            bg_ref[pl.ds(off, PEER_TOPK), :] = ex / jnp.sum(ex, axis=0, keepdims=True)
            be_ref[pl.ds(off, PEER_TOPK), :] = jnp.concatenate(experts[k], axis=0)
        return carry

    lax.fori_loop(0, PEER_HEADS // LISTS_PER_TRIP, stage2, 0)
    idx_ref[...] = (be_ref[...].T * float(ROW_TILE)).astype(I32)
    gate_ref[...] = bg_ref[...].T


def _topk(h2, wq_t16, subkeys16, tb):
    n = h2.shape[0]
    const = lambda a: pl.BlockSpec(a.shape, lambda i: (0,) * a.ndim)
    return pl.pallas_call(
        functools.partial(_topk_kernel, tb=tb),
        grid=(n // tb,),
        in_specs=[pl.BlockSpec((tb, D_MODEL), lambda i: (i, 0)), const(wq_t16), const(subkeys16)],
        out_specs=[pl.BlockSpec((tb, N_SLOTS), lambda i: (i, 0))] * 2,
        out_shape=[jax.ShapeDtypeStruct((n, N_SLOTS), I32), jax.ShapeDtypeStruct((n, N_SLOTS), F32)],
        scratch_shapes=[pltpu.VMEM((2 * PEER_HEADS, N_KEYS, tb), F32),
                        pltpu.VMEM((2 * PEER_HEADS, PEER_TOPK, tb), F32),
                        pltpu.VMEM((2 * PEER_HEADS, PEER_TOPK, tb), F32),
                        pltpu.VMEM((N_SLOTS, tb), F32), pltpu.VMEM((N_SLOTS, tb), F32)],
        compiler_params=_cparams(("parallel",)),
        name="peer_topk",
    )(h2, wq_t16, subkeys16)


def _pack_kernel(t_ref, o_ref):
    x = t_ref[...]
    half = x.shape[1] // 2
    rows = x.shape[0]
    bits = lax.bitcast_convert_type(x.astype(BF16).astype(F32), I32)
    word = lax.shift_right_logical(bits[:, :half], 16) | bits[:, half:]
    for c in range(ROW_TILE):
        o_ref[pl.ds(c, rows, stride=ROW_TILE), :] = word[:, c * LANES:(c + 1) * LANES]


def _pack_table(t, rows=512):
    n_e, d = t.shape
    return pl.pallas_call(
        _pack_kernel,
        grid=(n_e // rows,),
        in_specs=[pl.BlockSpec((rows, d), lambda i: (i, 0))],
        out_specs=pl.BlockSpec((rows * ROW_TILE, LANES), lambda i: (i, 0)),
        out_shape=jax.ShapeDtypeStruct((n_e * ROW_TILE, LANES), I32),
        compiler_params=_cparams(("parallel",)),
        name="pack_table",
    )(t)


def _peer_u_kernel(idx_ref, h_ref, gate_ref, tbl_ref, mask_ref, sel_ref, w_ref, pair_ref, *stage_refs, tb):
    buf_a, buf_b, buf_c, buf_d = stage_refs

    def gather(t, stage):
        for e in range(N_SLOTS):
            row = pl.multiple_of(idx_ref[t, e], ROW_TILE)
            stage[e * ROW_TILE:(e + 1) * ROW_TILE, :] = tbl_ref[pl.ds(row, ROW_TILE), :]

    def contract(t, stage):
        h3 = jnp.concatenate(_split3(h_ref[t]), axis=0)
        acc = jnp.zeros((3 * SUBLANES, 2 * N_SLOTS), F32)
        for c in range(ROW_TILE):
            word = stage[pl.ds(c, N_SLOTS, stride=ROW_TILE), :]
            rhs = pltpu.bitcast(word, BF16)
            acc = acc + mask_ref[c] * lax.dot_general(h3, rhs, NT_DIMS, preferred_element_type=F32)
        pair_ref[pl.ds(t, 1), :] = jnp.sum(acc, axis=0, keepdims=True)

    for buf in (buf_c, buf_d):
        buf[...] = jnp.zeros(buf.shape, I32)

    def pair(t, fill, drain):
        gather(t, fill[0])
        gather(t + 1, fill[1])
        contract(jnp.maximum(t - 2, 0), drain[0])
        contract(jnp.maximum(t - 1, 0), drain[1])

    def tokens(i, carry):
        for k in range(PAIRS_PER_TRIP):
            t = 2 * (PAIRS_PER_TRIP * i + k)
            if k % 2 == 0:
                pair(t, (buf_a, buf_b), (buf_c, buf_d))
            else:
                pair(t, (buf_c, buf_d), (buf_a, buf_b))
        return carry

    lax.fori_loop(0, tb // (2 * PAIRS_PER_TRIP), tokens, 0)
    contract(tb - 2, buf_c)
    contract(tb - 1, buf_d)
    a = sum(jnp.dot(x, sel_ref[...], preferred_element_type=F32) for x in _split3(pair_ref[...]))
    w_ref[...] = gate_ref[...] * (0.5 * a * (1.0 + lax.erf(a * (2.0 ** -0.5))))


def _chunk_masks():
    m = np.zeros((ROW_TILE, 3 * SUBLANES, 2 * N_SLOTS), np.float32)
    for c in range(ROW_TILE):
        for term in range(3):
            m[c, term * SUBLANES + c, 0::2] = 1.0
            m[c, term * SUBLANES + ROW_TILE + c, 1::2] = 1.0
    sel = np.zeros((2 * N_SLOTS, N_SLOTS), np.float32)
    sel[np.arange(2 * N_SLOTS), np.arange(2 * N_SLOTS) // 2] = 1.0
    return jnp.asarray(m), jnp.asarray(sel, BF16)


def _peer_u(idx, h2r, gate, tbl, tb):
    n = idx.shape[0]
    masks, sel = _chunk_masks()
    return pl.pallas_call(
        functools.partial(_peer_u_kernel, tb=tb),
        grid=(n // tb,),
        in_specs=[pl.BlockSpec((tb, N_SLOTS), lambda i: (i, 0), memory_space=pltpu.SMEM),
                  pl.BlockSpec((tb, SUBLANES, LANES), lambda i: (i, 0, 0)),
                  pl.BlockSpec((tb, N_SLOTS), lambda i: (i, 0)),
                  pl.BlockSpec(memory_space=pltpu.VMEM),
                  pl.BlockSpec(masks.shape, lambda i: (0, 0, 0)),
                  pl.BlockSpec(sel.shape, lambda i: (0, 0))],
        out_specs=pl.BlockSpec((tb, N_SLOTS), lambda i: (i, 0)),
        out_shape=jax.ShapeDtypeStruct((n, N_SLOTS), F32),
        scratch_shapes=[pltpu.VMEM((tb, 2 * N_SLOTS), F32)]
        + [pltpu.VMEM((N_SLOTS * ROW_TILE, LANES), I32)] * 4,
        compiler_params=_cparams(("arbitrary",)),
        name="peer_u",
    )(idx, h2r, gate, tbl, masks, sel)


def _peer_v_kernel(idx_ref, w_ref, tbl_ref, elo_ref, ehi_ref, xp_ref, g2_ref, fg_ref, y_ref,
                   *scratch_refs, tb):
    o_ref = scratch_refs[0]
    wexp_refs = scratch_refs[1:3]
    buf_a, buf_b, buf_c, buf_d = scratch_refs[3:7]
    n_terms = 3

    for half_ref in wexp_refs:
        half_ref[...] = jnp.zeros(half_ref.shape, F32)
    for k, term in enumerate(_split3(w_ref[...])):
        for j, spread_ref in enumerate((elo_ref, ehi_ref)):
            spread = jnp.dot(term, spread_ref[...], preferred_element_type=F32)
            for half, half_ref in enumerate(wexp_refs):
                half_ref[pl.ds(j * n_terms + k, tb, stride=SUBLANES), :] = (
                    spread[:, half * LANES:(half + 1) * LANES])

    def gather(t, stage):
        for e in range(N_SLOTS):
            row = pl.multiple_of(idx_ref[t, e], ROW_TILE)
            stage[e * ROW_TILE:(e + 1) * ROW_TILE, :] = tbl_ref[pl.ds(row, ROW_TILE), :]

    def contract(t, stage):
        rows = pl.ds(pl.multiple_of(t * SUBLANES, SUBLANES), SUBLANES)
        lhs = jnp.concatenate([half_ref[rows, :] for half_ref in wexp_refs], axis=1).astype(BF16)
        lo, hi = [], []
        for c in range(ROW_TILE):
            rhs = pltpu.bitcast(stage[pl.ds(c, N_SLOTS, stride=ROW_TILE), :], BF16)
            r = jnp.dot(lhs, rhs, preferred_element_type=F32)
            lo.append(r[0:1] + r[1:2] + r[2:3])
            hi.append(r[3:4] + r[4:5] + r[5:6])
        o_ref[pl.ds(t, 1), :] = jnp.concatenate(lo + hi, axis=1)

    for buf in (buf_c, buf_d):
        buf[...] = jnp.zeros(buf.shape, I32)

    def pair(t, fill, drain):
        gather(t, fill[0])
        gather(t + 1, fill[1])
        contract(jnp.maximum(t - 2, 0), drain[0])
        contract(jnp.maximum(t - 1, 0), drain[1])

    def tokens(i, carry):
        for k in range(PAIRS_PER_TRIP):
            t = 2 * (PAIRS_PER_TRIP * i + k)
            if k % 2 == 0:
                pair(t, (buf_a, buf_b), (buf_c, buf_d))
            else:
                pair(t, (buf_c, buf_d), (buf_a, buf_b))
        return carry

    lax.fori_loop(0, tb // (2 * PAIRS_PER_TRIP), tokens, 0)
    contract(tb - 2, buf_c)
    contract(tb - 1, buf_d)
    x = xp_ref[...] + g2_ref[...] * o_ref[...]
    ms = jnp.mean(x * x, axis=-1, keepdims=True)
    y_ref[...] = x * lax.rsqrt(ms + EPS) * fg_ref[...]


def _half_spreads():
    lo = np.zeros((N_SLOTS, 2 * N_SLOTS), np.float32)
    hi = np.zeros((N_SLOTS, 2 * N_SLOTS), np.float32)
    lo[np.arange(N_SLOTS), 2 * np.arange(N_SLOTS)] = 1.0
    hi[np.arange(N_SLOTS), 2 * np.arange(N_SLOTS) + 1] = 1.0
    return jnp.asarray(lo, BF16), jnp.asarray(hi, BF16)


def _peer_v(idx, w, tbl, xp, g2, fg, per_row, rows_per_mod, tb):
    n = idx.shape[0]
    elo, ehi = _half_spreads()
    blk = pl.BlockSpec((tb, N_SLOTS), lambda i: (i, 0))
    row = pl.BlockSpec((tb, D_MODEL), lambda i: (i, 0))
    return pl.pallas_call(
        functools.partial(_peer_v_kernel, tb=tb),
        grid=(n // tb,),
        in_specs=[pl.BlockSpec((tb, N_SLOTS), lambda i: (i, 0), memory_space=pltpu.SMEM), blk,
                  pl.BlockSpec(memory_space=pltpu.VMEM),
                  pl.BlockSpec(elo.shape, lambda i: (0, 0)), pl.BlockSpec(ehi.shape, lambda i: (0, 0)),
                  row, _mod_spec(per_row, tb, rows_per_mod), pl.BlockSpec((1, D_MODEL), lambda i: (0, 0))],
        out_specs=row,
        out_shape=jax.ShapeDtypeStruct((n, D_MODEL), F32),
        scratch_shapes=[pltpu.VMEM((tb, D_MODEL), F32)]
        + [pltpu.VMEM((tb * SUBLANES, LANES), F32)] * 2
        + [pltpu.VMEM((N_SLOTS * ROW_TILE, LANES), I32)] * 4,
        compiler_params=_cparams(("arbitrary",)),
        name="peer_v",
    )(idx, w, tbl, elo, ehi, xp, g2, fg)


def _peer(h2, h2_tiles, xp, g2, fg, wq_t16, subkeys16, u_tbl, v_tbl, per_row, rows_per_mod, tb_topk,
          tb_gather):
    n = h2.shape[0]
    idx, gate = _topk(h2, wq_t16, subkeys16, tb_topk)
    w = _peer_u(idx, h2_tiles.reshape(n, SUBLANES, LANES), gate, u_tbl, tb_gather)
    return _peer_v(idx, w, v_tbl, xp, g2, fg, per_row, rows_per_mod, tb_gather)


def kernel(x_prompt, x_sample, c_prompt, c_sample, cache_a_k, cache_a_v, cache_b_k, cache_b_v, cache_b_logf, page_table, w_ada, b_ada, norm1_g, w_in, b_f, lambda_q1, lambda_k1, lambda_q2, lambda_k2, subln_g, w_out, norm2_g, peer_wq, peer_subkeys, peer_u, peer_v, final_g):
    batch, seq_len, d = x_prompt.shape
    dec_batch, new_tokens, _ = x_sample.shape
    depth = w_ada.shape[0]
    assert depth == 1 and d == D_MODEL and new_tokens == 4
    n_p, n_s = batch * seq_len, dec_batch * new_tokens
    n_pool, page = cache_a_k.shape[1], cache_a_k.shape[2]
    l = 0
    lam_init = 0.8 - 0.6 * math.exp(-0.3 * l)

    n_c = batch + dec_batch
    pad = (-n_c) % SUBLANES
    c_all = jnp.concatenate([c_prompt, c_sample, jnp.zeros((pad, d), F32)], axis=0)
    ada = _ada(c_all, w_ada[l], b_ada[l])
    mods_p = [m.reshape(batch, 1, d) for m in jnp.split(ada[:batch], 6, axis=-1)]
    mods_s = [jnp.repeat(m, new_tokens, axis=0) for m in jnp.split(ada[batch:n_c], 6, axis=-1)]
    lam = _lam(lambda_q1[l], lambda_k1[l], lambda_q2[l], lambda_k2[l], lam_init)

    n_main = 3 * WA + 3 * WB
    w_main = w_in[l][:, :n_main].astype(BF16)
    w_f = jnp.pad(w_in[l][:, n_main:], ((0, 0), (0, LANES - HB))).astype(BF16)
    bf = jnp.pad(b_f[l], (0, LANES - HB)).reshape(1, LANES)
    g1n = norm1_g[l].reshape(1, d)
    g2n = norm2_g[l].reshape(1, d)
    sub_g = subln_g[l].reshape(1, DVA)
    w_out16 = w_out[l].astype(BF16)
    wq_t16 = peer_wq[l].T.astype(BF16)
    subkeys16 = peer_subkeys[l].astype(BF16)
    u_tbl = _pack_table(peer_u[l])
    v_tbl = _pack_table(peer_v[l])
    fg = final_g.reshape(1, d)

    tm_p = 256
    xp2d = x_prompt.reshape(n_p, d)
    sh1, sc1, g1, sh2, sc2, g2 = mods_p
    (ka, va, kb, vb, lf, qa16, ka16, va16, qb16, kb16, vb16, qbias, kbias) = _proj(
        xp2d, sh1, sc1, g1n, w_main, w_f, bf, False, seq_len, seq_len, tm_p)
    qbias_a, kbias_a = _alibi_bias(seq_len)
    tq = 1024
    oa = _attention(qa16, qbias_a, ka16, kbias_a, va16, lam, batch, seq_len, tq, fox=False)
    ob = _attention(qb16, qbias, kb16, kbias, vb16, lam, batch, seq_len, tq, fox=True)
    xp_mid, h2, h2_t = _merge(oa, ob, xp2d, g1, sh2, sc2, sub_g, w_out16, g2n, False, seq_len, tm_p,
                        1.0 - lam_init)
    y_prompt = _peer(h2, h2_t, xp_mid, g2, fg, wq_t16, subkeys16, u_tbl, v_tbl, False, seq_len,
                     256, 128).reshape(batch, seq_len, d)

    tm_s = n_s
    xs2d = x_sample.reshape(n_s, d)
    sh1, sc1, g1, sh2, sc2, g2 = mods_s
    (ka_s, va_s, kb_s, vb_s, lf_s, qa16_s, _, _, qb16_s, _, _, _, _) = _proj(
        xs2d, sh1, sc1, g1n, w_main, w_f, bf, True, n_s, n_s, tm_s)
    assert page == LANES
    qa5 = qa16_s.reshape(dec_batch, new_tokens, HA, 2, DA)
    zeros = jnp.zeros_like(qa5[:, :, :, 0, :])
    wqa = jnp.stack([jnp.concatenate([qa5[:, :, :, 0, :], zeros], axis=-1),
                     jnp.concatenate([zeros, qa5[:, :, :, 1, :]], axis=-1)], axis=3)
    wqa = jnp.pad(wqa, ((0, 0), (0, QPAD - new_tokens), (0, 0), (0, 0), (0, 0)))
    wqa = wqa.transpose(0, 2, 3, 1, 4).reshape(dec_batch, HA * 2 * QPAD, 2 * DA)
    qb4 = qb16_s.reshape(dec_batch, new_tokens, HB, DB)
    wqb = (qb4[:, :, :, None, :] * jnp.eye(HB, dtype=BF16)[None, None, :, :, None])
    wqb = wqb.reshape(dec_batch, new_tokens * HB, WB)

    def pos_major(c):
        return c.reshape(n_pool, page * HA, c.shape[-1])

    def pos_minor(c):
        return c.transpose(0, 2, 3, 1).reshape(n_pool, HB * DB, page)

    caches = [pos_major(cache_a_k[l]), pos_major(cache_a_v[l]), pos_minor(cache_b_k[l]),
              pos_minor(cache_b_v[l]), cache_b_logf[l].transpose(0, 2, 1)]

    def new_pos_major(a):
        a = a.reshape(dec_batch, new_tokens * HA, DVA)
        return jnp.pad(a, ((0, 0), (0, (page - new_tokens) * HA), (0, 0)))

    def new_pos_minor(a, heads):
        w = a.shape[1] // heads
        a = a.reshape(dec_batch, new_tokens, heads, w).transpose(0, 2, 3, 1)
        return jnp.pad(a, ((0, 0), (0, 0), (0, 0), (0, page - new_tokens))).reshape(dec_batch, heads * w, page)

    new_pages = [new_pos_major(ka_s), new_pos_major(va_s), new_pos_minor(kb_s, HB),
                 new_pos_minor(vb_s, HB), new_pos_minor(lf_s, HB)]
    oa_s, ob_s = _decode_attention(page_table, caches, new_pages, wqa, wqb, lam, new_tokens, 16)
    oa_s = oa_s.reshape(dec_batch, HA, QPAD, DVA)[:, :, :new_tokens].transpose(0, 2, 1, 3).reshape(n_s, WA)
    ob_s = ob_s.reshape(dec_batch, new_tokens, HB, 2, DB)
    ob_s = jnp.where((jnp.arange(HB) % 2 == 0)[None, None, :, None], ob_s[:, :, :, 0, :], ob_s[:, :, :, 1, :])
    ob_s = ob_s.reshape(n_s, WB)
    xs_mid, h2_s, h2_st = _merge(oa_s, ob_s, xs2d, g1, sh2, sc2, sub_g, w_out16, g2n, True, n_s, tm_s,
                          1.0 - lam_init)
    y_sample = _peer(h2_s, h2_st, xs_mid, g2, fg, wq_t16, subkeys16, u_tbl, v_tbl, True, n_s,
                     n_s, 128).reshape(dec_batch, new_tokens, d)

    def kv(a, b, t, h, w):
        return a.reshape(1, b, t, h, w)

    return (y_prompt, y_sample,
            kv(ka, batch, seq_len, HA, 2 * DA), kv(va, batch, seq_len, HA, DVA),
            kv(kb, batch, seq_len, HB, DB), kv(vb, batch, seq_len, HB, DB),
            lf.reshape(1, batch, seq_len, HB),
            kv(ka_s, dec_batch, new_tokens, HA, 2 * DA), kv(va_s, dec_batch, new_tokens, HA, DVA),
            kv(kb_s, dec_batch, new_tokens, HB, DB), kv(vb_s, dec_batch, new_tokens, HB, DB),
            lf_s.reshape(1, dec_batch, new_tokens, HB))
```

```python
import functools
import math

import jax
import jax.numpy as jnp
import numpy as np
from jax import lax
from jax.experimental import pallas as pl
from jax.experimental.pallas import tpu as pltpu

F32, BF16, I32 = jnp.float32, jnp.bfloat16, jnp.int32
EPS = 1e-6
D_MODEL = 1024
HA, DA, DVA = 4, 64, 128
HB, DB = 8, 64
WA, WB = HA * DVA, HB * DB
N_KEYS = 128
PEER_HEADS = 8
PEER_TOPK = 16
N_SLOTS = PEER_HEADS * PEER_TOPK
N_EXPERTS = N_KEYS * N_KEYS
LANES = 128
SUBLANES = 8
ROW_TILE = D_MODEL // (2 * LANES)
QPAD = SUBLANES
PAIRS_PER_TRIP = 32
LISTS_PER_TRIP = 2
NEG_INF = float("-inf")
VMEM_LIMIT = 56 * 1024 * 1024
NT_DIMS = (((1,), (1,)), ((), ()))


def _cparams(sem):
    return pltpu.CompilerParams(dimension_semantics=sem, vmem_limit_bytes=VMEM_LIMIT)


def _split3(x):
    hi = x.astype(BF16)
    r = x - hi.astype(F32)
    mid = r.astype(BF16)
    lo = (r - mid.astype(F32)).astype(BF16)
    return hi, mid, lo


def _ada_kernel(c_ref, w_ref, b_ref, o_ref):
    c = c_ref[...]
    s = c * (1.0 / (1.0 + jnp.exp(-c)))
    o_ref[...] = jnp.dot(s.astype(BF16), w_ref[...].astype(BF16),
                         preferred_element_type=F32) + b_ref[...]


def _ada(c_all, w, b):
    r = c_all.shape[0]
    n_out = w.shape[1]
    return pl.pallas_call(
        _ada_kernel,
        grid=(n_out // D_MODEL,),
        in_specs=[pl.BlockSpec((r, D_MODEL), lambda j: (0, 0)),
                  pl.BlockSpec((D_MODEL, D_MODEL), lambda j: (0, j)),
                  pl.BlockSpec((1, D_MODEL), lambda j: (0, j))],
        out_specs=pl.BlockSpec((r, D_MODEL), lambda j: (0, j)),
        out_shape=jax.ShapeDtypeStruct((r, n_out), F32),
        compiler_params=_cparams(("parallel",)),
        name="ada",
    )(c_all, w, b.reshape(1, n_out))


def _lam_kernel(v_ref, o_ref, *, lam_init):
    v = v_ref[...]
    a = jnp.sum(v[0:1] * v[1:2], axis=1, keepdims=True)
    b = jnp.sum(v[2:3] * v[3:4], axis=1, keepdims=True)
    lam = jnp.exp(a) - jnp.exp(b) + lam_init
    o_ref[...] = jnp.broadcast_to(lam, o_ref.shape)


def _lam(lq1, lk1, lq2, lk2, lam_init):
    v = jnp.zeros((SUBLANES, LANES), F32)
    v = v.at[0:4, 0:DA].set(jnp.stack([lq1, lk1, lq2, lk2]))
    return pl.pallas_call(
        functools.partial(_lam_kernel, lam_init=lam_init),
        out_shape=jax.ShapeDtypeStruct((SUBLANES, LANES), F32),
        name="lam",
    )(v)


def _mod_spec(per_row, tm, rows_per_mod):
    if per_row:
        return pl.BlockSpec((tm, D_MODEL), lambda i: (i, 0))
    blocks = rows_per_mod // tm
    return pl.BlockSpec((None, 1, D_MODEL), lambda i: (i // blocks, 0, 0))


def _proj_kernel(x_ref, sh_ref, sc_ref, g_ref, w_ref, wf_ref, bf_ref, tri_ref, pq_ref, pk_ref,
                 qc_ref, kc_ref,
                 ka_ref, va_ref, kb_ref, vb_ref, lf_ref,
                 qa16_ref, ka16_ref, va16_ref, qb16_ref, kb16_ref, vb16_ref, qbias_ref, kbias_ref,
                 carry_ref, *, blocks_per_seq):
    i = pl.program_id(0)

    @pl.when(i % blocks_per_seq == 0)
    def _():
        carry_ref[...] = jnp.zeros_like(carry_ref)

    x = x_ref[...]
    ms = jnp.mean(x * x, axis=-1, keepdims=True)
    h = x * lax.rsqrt(ms + EPS) * g_ref[...]
    h = h * (1.0 + sc_ref[...]) + sh_ref[...]
    hb = h.astype(BF16)
    p = jnp.dot(hb, w_ref[...], preferred_element_type=F32)
    qa, ka, va = p[:, 0:WA], p[:, WA:2 * WA], p[:, 2 * WA:3 * WA]
    o = 3 * WA
    qb, kb, vb = p[:, o:o + WB], p[:, o + WB:o + 2 * WB], p[:, o + 2 * WB:o + 3 * WB]
    tm = x.shape[0]
    for hd in range(HA):
        ka_ref[pl.ds(hd, tm, stride=HA), :] = ka[:, hd * DVA:(hd + 1) * DVA]
        va_ref[pl.ds(hd, tm, stride=HA), :] = va[:, hd * DVA:(hd + 1) * DVA]
    kb_ref[...] = kb
    vb_ref[...] = vb
    qa16_ref[...] = (qa * (DA ** -0.5)).astype(BF16)
    ka16_ref[...] = ka.astype(BF16)
    va16_ref[...] = va.astype(BF16)
    qb16_ref[...] = (qb * (DB ** -0.5)).astype(BF16)
    kb16_ref[...] = kb.astype(BF16)
    vb16_ref[...] = vb.astype(BF16)

    z = jnp.dot(hb, wf_ref[...], preferred_element_type=F32) + bf_ref[...]
    lf = jnp.minimum(z, 0.0) - jnp.log1p(jnp.exp(-jnp.abs(z)))
    lane = lax.broadcasted_iota(I32, lf.shape, 1)
    lf = jnp.where(lane < HB, lf, 0.0)
    lf_ref[...] = lf[:, 0:HB]

    tri = tri_ref[...]
    hi, mid, lo = _split3(lf)
    cum = (jnp.dot(tri, hi, preferred_element_type=F32)
           + jnp.dot(tri, mid, preferred_element_type=F32)
           + jnp.dot(tri, lo, preferred_element_type=F32)) + carry_ref[...]
    tm = cum.shape[0]
    carry_ref[...] = cum[tm - 1:tm, :]
    ccat = jnp.concatenate(_split3(cum), axis=1)
    qbias_ref[...] = (jnp.dot(ccat, pq_ref[...], preferred_element_type=F32) + qc_ref[...]).astype(BF16)
    kbias_ref[...] = (jnp.dot(ccat, pk_ref[...], preferred_element_type=F32) + kc_ref[...]).astype(BF16)


def _decay_placement():
    pq = np.zeros((3 * LANES, WB), np.float32)
    pk = np.zeros((3 * LANES, WB), np.float32)
    qc = np.zeros((1, WB), np.float32)
    kc = np.zeros((1, WB), np.float32)
    for h in range(HB):
        pair, which = divmod(h, 2)
        base = pair * LANES + which * 6
        for t in range(3):
            pq[t * LANES + h, base + t] = 1.0
            qc[0, base + 3 + t] = 1.0
            kc[0, base + t] = 1.0
            pk[t * LANES + h, base + 3 + t] = -1.0
    return (jnp.asarray(pq, BF16), jnp.asarray(pk, BF16), jnp.asarray(qc), jnp.asarray(kc))


def _proj(x2d, sh, sc, g, w_main, w_f, b_f, per_row, rows_per_mod, seq_len, tm):
    n = x2d.shape[0]
    nw = w_main.shape[1]
    tri = (np.arange(tm)[None, :] <= np.arange(tm)[:, None]).astype(np.float32)
    pq, pk, qc, kc = _decay_placement()
    mod = _mod_spec(per_row, tm, rows_per_mod)
    row = lambda w: pl.BlockSpec((tm, w), lambda i: (i, 0))
    const = lambda a: pl.BlockSpec(a.shape, lambda i: (0,) * a.ndim)
    tri = jnp.asarray(tri, BF16)
    head_rows = pl.BlockSpec((tm * HA, DVA), lambda i: (i, 0))
    outs = ([jax.ShapeDtypeStruct((n * HA, DVA), F32)] * 2 + [jax.ShapeDtypeStruct((n, WB), F32)] * 2
            + [jax.ShapeDtypeStruct((n, HB), F32)] + [jax.ShapeDtypeStruct((n, WA), BF16)] * 8)
    return pl.pallas_call(
        functools.partial(_proj_kernel, blocks_per_seq=seq_len // tm),
        grid=(n // tm,),
        in_specs=[row(D_MODEL), mod, mod, const(g), const(w_main), const(w_f), const(b_f),
                  const(tri), const(pq), const(pk), const(qc), const(kc)],
        out_specs=[head_rows] * 2 + [row(WB)] * 2 + [row(HB)] + [row(WA)] * 8,
        out_shape=outs,
        scratch_shapes=[pltpu.VMEM((1, LANES), F32)],
        compiler_params=_cparams(("arbitrary",)),
        name="proj",
    )(x2d, sh, sc, g, w_main, w_f, b_f, tri, pq, pk, qc, kc)


def _attn_kernel(q_ref, qb_ref, k_ref, kb_ref, v_ref, lam_ref, o_ref, m_ref, l_ref, acc_ref,
                 *, tq, fox):
    qi = pl.program_id(2)
    q = q_ref[...].astype(F32)
    qb = qb_ref[...].astype(F32)
    lane = lax.broadcasted_iota(I32, (tq, LANES), 1)
    q_lo = jnp.where(lane < DA, q, 0.0)
    q_hi = jnp.where(lane >= DA, q, 0.0)
    if fox:
        qb_lo = jnp.where(lane < 6, qb, 0.0)
        qb_hi = jnp.where((lane >= 6) & (lane < 12), qb, 0.0)
    else:
        qb_lo = qb_hi = qb
    qq = jnp.concatenate([jnp.concatenate([q_lo, qb_lo], axis=1),
                          jnp.concatenate([q_hi, qb_hi], axis=1)], axis=0).astype(BF16)
    m_ref[...] = jnp.full(m_ref.shape, NEG_INF, F32)
    l_ref[...] = jnp.zeros(l_ref.shape, F32)
    acc_ref[...] = jnp.zeros(acc_ref.shape, F32)

    half = tq // 2

    def step(j, rows, k_lo, k_len, first_local):
        off = pl.multiple_of(j * tq, tq) + k_lo
        kk = jnp.concatenate([k_ref[pl.ds(off, k_len), :], kb_ref[pl.ds(off, k_len), :]], axis=1)
        qs = qq if len(rows) == 1 and rows[0] == (0, 2 * tq) else jnp.concatenate(
            [qq[a:b] for a, b in rows], axis=0)
        s = lax.dot_general(qs, kk, NT_DIMS, preferred_element_type=F32)
        if first_local is not None:
            n_r = rows[0][1] - rows[0][0]
            r = lax.broadcasted_iota(I32, s.shape, 0)
            c = lax.broadcasted_iota(I32, s.shape, 1)
            r = jnp.where(r >= n_r, r - n_r, r) + first_local
            s = jnp.where(c + k_lo <= r, s, NEG_INF)
        chunks = [s[:, c * LANES:(c + 1) * LANES] for c in range(k_len // LANES)]
        m_prev = jnp.concatenate([m_ref[a:b] for a, b in rows], axis=0)
        l_prev = jnp.concatenate([l_ref[a:b] for a, b in rows], axis=0)
        acc_prev = jnp.concatenate([acc_ref[a:b] for a, b in rows], axis=0)
        m_cur = functools.reduce(jnp.maximum, chunks)
        m_new = jnp.maximum(m_prev, jnp.max(m_cur, axis=1, keepdims=True))
        alpha = jnp.exp(m_prev - m_new)
        ps = [jnp.exp(ch - m_new) for ch in chunks]
        l_new = alpha * l_prev + functools.reduce(jnp.add, ps)
        p = jnp.concatenate([x.astype(BF16) for x in ps], axis=1)
        acc_new = alpha * acc_prev + jnp.dot(p, v_ref[pl.ds(off, k_len), :], preferred_element_type=F32)
        pos = 0
        for a, b in rows:
            m_ref[a:b] = m_new[pos:pos + b - a]
            l_ref[a:b] = l_new[pos:pos + b - a]
            acc_ref[a:b] = acc_new[pos:pos + b - a]
            pos += b - a

    every_row = [(0, 2 * tq)]

    def body(j, carry):
        step(j, every_row, 0, tq, None)
        return carry

    lax.fori_loop(0, qi, body, 0)
    step(qi, [(0, tq), (tq, 2 * tq)], 0, half, 0)
    step(qi, [(half, tq), (tq + half, 2 * tq)], half, half, half)
    o = acc_ref[...] / jnp.sum(l_ref[...], axis=1, keepdims=True)
    if fox:
        o_ref[...] = jnp.where(lane < DB, o[0:tq], o[tq:2 * tq])
    else:
        o_ref[...] = o[0:tq] - lam_ref[0:1, :] * o[tq:2 * tq]


def _attention(q16, qbias, k16, kbias, v16, lam, batch, seq_len, tq, fox):
    n = q16.shape[0]
    groups = q16.shape[1] // LANES
    nq = seq_len // tq
    q_spec = pl.BlockSpec((tq, LANES), lambda b, g, i: (b * nq + i, g))
    kv_spec = pl.BlockSpec((seq_len, LANES), lambda b, g, i: (b, g))
    if fox:
        qb_spec, kb_spec = q_spec, kv_spec
    else:
        qb_spec = pl.BlockSpec((None, tq, LANES), lambda b, g, i: (g, i, 0))
        kb_spec = pl.BlockSpec((None, seq_len, LANES), lambda b, g, i: (g, 0, 0))
    return pl.pallas_call(
        functools.partial(_attn_kernel, tq=tq, fox=fox),
        grid=(batch, groups, nq),
        in_specs=[q_spec, qb_spec, kv_spec, kb_spec, kv_spec,
                  pl.BlockSpec((SUBLANES, LANES), lambda b, g, i: (0, 0))],
        out_specs=q_spec,
        out_shape=jax.ShapeDtypeStruct((n, groups * LANES), F32),
        scratch_shapes=[pltpu.VMEM((2 * tq, LANES), F32)] * 3,
        compiler_params=_cparams(("parallel", "parallel", "arbitrary")),
        name="attn_fox" if fox else "attn_diff",
    )(q16, qbias, k16, kbias, v16, lam)


def _alibi_bias(seq_len):
    slopes = 2.0 ** (-8.0 * np.arange(1, HA + 1, dtype=np.float64) / HA)
    pos = np.arange(seq_len)
    hi, lo = (pos // LANES) * float(LANES), (pos % LANES).astype(np.float64)
    qb = np.zeros((HA, seq_len, LANES), np.float32)
    kb = np.zeros((HA, seq_len, LANES), np.float32)
    for h in range(HA):
        qb[h, :, 0], qb[h, :, 1], qb[h, :, 2], qb[h, :, 3] = -slopes[h] * hi, -slopes[h] * lo, 1.0, 1.0
        kb[h, :, 0], kb[h, :, 1], kb[h, :, 2], kb[h, :, 3] = 1.0, 1.0, slopes[h] * hi, slopes[h] * lo
    return jnp.asarray(qb, BF16), jnp.asarray(kb, BF16)


def _decode_kernel(pt_ref, *refs, pages_per_step, n_steps, page):
    del pt_ref
    pp = pages_per_step
    paged = [refs[c * pp:(c + 1) * pp] for c in range(5)]
    (nka_ref, nva_ref, nkb_ref, nvb_ref, nlf_ref, wqa_ref, wqb_ref, biasa_ref, slope_ref,
     maska_ref, maskb_ref, triu_ref, dmaskb_ref, lam_ref) = refs[5 * pp:5 * pp + 14]
    oa_ref, ob_ref = refs[5 * pp + 14:5 * pp + 16]
    (ma_ref, la_ref, acca_ref, mb_ref, lb_ref, accb_ref, carry_ref) = refs[5 * pp + 16:]
    j = pl.program_id(1)
    half_rows = HA * QPAD

    @pl.when(j == 0)
    def _():
        ma_ref[...] = jnp.full(ma_ref.shape, NEG_INF, F32)
        mb_ref[...] = jnp.full(mb_ref.shape, NEG_INF, F32)
        la_ref[...] = jnp.zeros(la_ref.shape, F32)
        lb_ref[...] = jnp.zeros(lb_ref.shape, F32)
        acca_ref[...] = jnp.zeros(acca_ref.shape, F32)
        accb_ref[...] = jnp.zeros(accb_ref.shape, F32)
        carry_ref[...] = jnp.zeros(carry_ref.shape, F32)

    def softmax_step(chunks, m_ref, l_ref):
        m_prev = m_ref[...]
        m_cur = functools.reduce(jnp.maximum, chunks)
        m_new = jnp.maximum(m_prev, jnp.max(m_cur, axis=1, keepdims=True))
        alpha = jnp.exp(m_prev - m_new)
        ps = [jnp.exp(s - m_new) for s in chunks]
        l_ref[...] = alpha * l_ref[...] + functools.reduce(jnp.add, ps)
        m_ref[...] = m_new
        return alpha, ps

    def process(pages):
        chunks = []
        for ka_r, _, _, _, _, page_idx, masked in pages:
            parts = []
            for h in range(HA):
                k_h = ka_r[pl.ds(h, page, stride=HA), :].astype(BF16)
                parts.append(lax.dot_general(wqa_ref[h * 2 * QPAD:(h + 1) * 2 * QPAD, :], k_h, NT_DIMS,
                                             preferred_element_type=F32))
            sa = jnp.concatenate([x[0:QPAD] for x in parts] + [x[QPAD:2 * QPAD] for x in parts], axis=0)
            sa = sa + biasa_ref[...] + slope_ref[...] * (page_idx * float(page))
            if masked:
                sa = jnp.where(maska_ref[...] > 0.0, sa, NEG_INF)
            chunks.append(sa)
        alpha, ps = softmax_step(chunks, ma_ref, la_ref)
        acc = alpha * acca_ref[...]
        for (_, va_r, _, _, _, _, _), p in zip(pages, ps):
            pv = []
            for h in range(HA):
                v_h = va_r[pl.ds(h, page, stride=HA), :].astype(BF16)
                p_h = jnp.concatenate([p[h * QPAD:(h + 1) * QPAD],
                                       p[half_rows + h * QPAD:half_rows + (h + 1) * QPAD]], axis=0)
                pv.append(jnp.dot(p_h.astype(BF16), v_h, preferred_element_type=F32))
            acc = acc + jnp.concatenate([x[0:QPAD] for x in pv] + [x[QPAD:2 * QPAD] for x in pv], axis=0)
        acca_ref[...] = acc
        triu = triu_ref[...]
        carry = carry_ref[...]
        chunks = []
        for _, _, kb_r, _, lf_r, _, masked in pages:
            hi, mid, lo = _split3(lf_r[...])
            cum = (jnp.dot(hi, triu, preferred_element_type=F32)
                   + jnp.dot(mid, triu, preferred_element_type=F32)
                   + jnp.dot(lo, triu, preferred_element_type=F32)) + carry
            carry = jnp.broadcast_to(cum[:, page - 1:page], carry.shape)
            sb = jnp.dot(wqb_ref[...], kb_r[...].astype(BF16), preferred_element_type=F32)
            sb = sb - jnp.concatenate([cum] * 4, axis=0)
            if masked:
                sb = jnp.where(maskb_ref[...] > 0.0, sb, NEG_INF)
            chunks.append(sb)
        carry_ref[...] = carry
        alpha, ps = softmax_step(chunks, mb_ref, lb_ref)
        acc = jnp.concatenate([alpha] * (WB // LANES), axis=1) * accb_ref[...]
        for (_, _, _, vb_r, _, _, _), p in zip(pages, ps):
            acc = acc + lax.dot_general(p.astype(BF16), vb_r[...].astype(BF16), NT_DIMS,
                                        preferred_element_type=F32)
        accb_ref[...] = acc

    cached = [(paged[0][k], paged[1][k], paged[2][k], paged[3][k], paged[4][k],
               (j * pp + k).astype(F32), False) for k in range(pp)]

    @pl.when(j < n_steps - 1)
    def _():
        process(cached)

    @pl.when(j == n_steps - 1)
    def _():
        process(cached + [(nka_ref, nva_ref, nkb_ref, nvb_ref, nlf_ref, float(n_steps * pp), True)])
        oa = acca_ref[...] / jnp.sum(la_ref[...], axis=1, keepdims=True)
        oa_ref[...] = oa[0:half_rows] - lam_ref[0:1, :] * oa[half_rows:2 * half_rows]
        ob = accb_ref[...] / jnp.sum(lb_ref[...], axis=1, keepdims=True) * dmaskb_ref[...]
        ob_ref[...] = ob[:, 0:128] + ob[:, 128:256] + ob[:, 256:384] + ob[:, 384:512]


def _decode_attention(page_table, caches, new_pages, wqa, wqb, lam, new_tokens, pages_per_step):
    dec_batch, n_pages = page_table.shape
    page = caches[4].shape[2]
    pp = pages_per_step
    n_steps = n_pages // pp
    past = n_pages * page
    rows_a = 2 * HA * QPAD
    rows_b = new_tokens * HB
    slopes = 2.0 ** (-8.0 * np.arange(1, HA + 1, dtype=np.float64) / HA)
    ra = np.arange(rows_a)
    ha, qa_i = (ra // QPAD) % HA, np.minimum(ra % QPAD, new_tokens - 1)
    col = np.arange(page)
    biasa = slopes[ha][:, None] * (col[None, :] - past - qa_i[:, None])
    slope_rep = np.repeat(slopes[ha][:, None], page, axis=1)
    maska = (col[None, :] <= qa_i[:, None]).astype(np.float32)
    rb = np.arange(rows_b)
    maskb = (col[None, :] <= (rb // HB)[:, None]).astype(np.float32)
    triu = (np.arange(page)[:, None] <= np.arange(page)[None, :]).astype(np.float32)
    cb = np.arange(WB)
    dmaskb = (cb[None, :] // DB == (rb % HB)[:, None]).astype(np.float32)
    consts = [jnp.asarray(biasa, F32), jnp.asarray(slope_rep, F32), jnp.asarray(maska),
              jnp.asarray(maskb), jnp.asarray(triu, BF16), jnp.asarray(dmaskb), lam]

    def page_spec(shape, k):
        return pl.BlockSpec((None,) + shape,
                            lambda b, j, pt: (pt[b * n_pages + j * pp + k], 0, 0))

    in_specs, args = [], []
    for c in caches:
        for k in range(pp):
            in_specs.append(page_spec(c.shape[1:], k))
            args.append(c)
    for a in new_pages + [wqa, wqb]:
        in_specs.append(pl.BlockSpec((None,) + a.shape[1:], lambda b, j, pt: (b, 0, 0)))
        args.append(a)
    for a in consts:
        in_specs.append(pl.BlockSpec(a.shape, lambda b, j, pt: (0, 0)))
        args.append(a)
    grid_spec = pltpu.PrefetchScalarGridSpec(
        num_scalar_prefetch=1,
        grid=(dec_batch, n_steps),
        in_specs=in_specs,
        out_specs=[pl.BlockSpec((None, rows_a // 2, LANES), lambda b, j, pt: (b, 0, 0)),
                   pl.BlockSpec((None, rows_b, LANES), lambda b, j, pt: (b, 0, 0))],
        scratch_shapes=[pltpu.VMEM((rows_a, LANES), F32), pltpu.VMEM((rows_a, LANES), F32),
                        pltpu.VMEM((rows_a, DVA), F32),
                        pltpu.VMEM((rows_b, LANES), F32), pltpu.VMEM((rows_b, LANES), F32),
                        pltpu.VMEM((rows_b, WB), F32),
                        pltpu.VMEM((HB, page), F32)],
    )
    return pl.pallas_call(
        functools.partial(_decode_kernel, pages_per_step=pp, n_steps=n_steps, page=page),
        grid_spec=grid_spec,
        out_shape=[jax.ShapeDtypeStruct((dec_batch, rows_a // 2, LANES), F32),
                   jax.ShapeDtypeStruct((dec_batch, rows_b, LANES), F32)],
        compiler_params=_cparams(("parallel", "arbitrary")),
        name="decode_attn",
    )(page_table.reshape(-1), *args)


def _merge_kernel(oa_ref, ob_ref, x_ref, g1_ref, sh2_ref, sc2_ref, sub_ref, wo_ref, n2_ref,
                  xp_ref, h2_ref, h2t_ref, *, out_scale):
    oa = oa_ref[...]
    parts = []
    for h in range(HA):
        o = oa[:, h * DVA:(h + 1) * DVA]
        ms = jnp.mean(o * o, axis=-1, keepdims=True)
        parts.append(o * lax.rsqrt(ms + EPS) * sub_ref[...] * out_scale)
    o = jnp.concatenate(parts + [ob_ref[...]], axis=1).astype(BF16)
    y = jnp.dot(o, wo_ref[...], preferred_element_type=F32)
    xp = x_ref[...] + g1_ref[...] * y
    xp_ref[...] = xp
    ms = jnp.mean(xp * xp, axis=-1, keepdims=True)
    h2 = xp * lax.rsqrt(ms + EPS) * n2_ref[...]
    h2 = h2 * (1.0 + sc2_ref[...]) + sh2_ref[...]
    h2_ref[...] = h2
    tm = h2.shape[0]
    for c in range(SUBLANES):
        h2t_ref[pl.ds(c, tm, stride=SUBLANES), :] = h2[:, c * LANES:(c + 1) * LANES]


def _merge(oa, ob, x2d, g1, sh2, sc2, sub_g, w_out16, n2_g, per_row, rows_per_mod, tm, out_scale):
    n = x2d.shape[0]
    mod = _mod_spec(per_row, tm, rows_per_mod)
    row = lambda w: pl.BlockSpec((tm, w), lambda i: (i, 0))
    const = lambda a: pl.BlockSpec(a.shape, lambda i: (0,) * a.ndim)
    return pl.pallas_call(
        functools.partial(_merge_kernel, out_scale=out_scale),
        grid=(n // tm,),
        in_specs=[row(WA), row(WB), row(D_MODEL), mod, mod, mod, const(sub_g), const(w_out16),
                  const(n2_g)],
        out_specs=[row(D_MODEL), row(D_MODEL), pl.BlockSpec((tm * SUBLANES, LANES), lambda i: (i, 0))],
        out_shape=[jax.ShapeDtypeStruct((n, D_MODEL), F32)] * 2
        + [jax.ShapeDtypeStruct((n * SUBLANES, LANES), F32)],
        compiler_params=_cparams(("parallel",)),
        name="merge",
    )(oa, ob, x2d, g1, sh2, sc2, sub_g, w_out16, n2_g)


def _topk_kernel(h_ref, wq_ref, sk_ref, idx_ref, gate_ref, s_ref, ts_ref, ti_ref, be_ref, bg_ref, *, tb):
    hb = h_ref[...].astype(BF16)
    qt = lax.dot_general(wq_ref[...], hb, NT_DIMS, preferred_element_type=F32)
    for hp in range(2 * PEER_HEADS):
        sub = sk_ref[hp % 2]
        s_ref[hp] = jnp.dot(sub, qt[hp * N_KEYS:(hp + 1) * N_KEYS, :].astype(BF16),
                            preferred_element_type=F32)
    row_k = lax.broadcasted_iota(I32, (N_KEYS, tb), 0).astype(F32)

    def stage1(i, carry):
        hps = [LISTS_PER_TRIP * i + k for k in range(LISTS_PER_TRIP)]
        ss = [s_ref[hp] for hp in hps]
        for r in range(PEER_TOPK):
            for k, hp in enumerate(hps):
                s = ss[k]
                m = jnp.max(s, axis=0, keepdims=True)
                pick = jnp.min(jnp.where(s == m, row_k, float(N_KEYS)), axis=0, keepdims=True)
                ts_ref[hp, r:r + 1, :] = m
                ti_ref[hp, r:r + 1, :] = pick
                ss[k] = jnp.where(row_k == pick, NEG_INF, s)
        return carry

    lax.fori_loop(0, 2 * PEER_HEADS // LISTS_PER_TRIP, stage1, 0)
    half = PEER_TOPK // 2
    n_cand = PEER_TOPK + (half - 1) * half + half
    row_c = lax.broadcasted_iota(I32, (n_cand, tb), 0).astype(F32)

    def candidates(x, y, combine):
        rows = [combine(x[0:1, :], y)]
        rows += [combine(x[i:i + 1, :], y[0:half, :]) for i in range(1, half)]
        rows.append(combine(x[half:PEER_TOPK, :], y[0:1, :]))
        return jnp.concatenate(rows, axis=0)

    def stage2(i, carry):
        heads = [LISTS_PER_TRIP * i + k for k in range(LISTS_PER_TRIP)]
        css, ces = [], []
        for h in heads:
            a, b = ts_ref[2 * h], ts_ref[2 * h + 1]
            ia, ib = ti_ref[2 * h], ti_ref[2 * h + 1]
            css.append(candidates(a, b, lambda x, y: x + y))
            ces.append(candidates(ia, ib, lambda x, y: x * float(N_KEYS) + y))
        best = [[] for _ in heads]
        experts = [[] for _ in heads]
        for r in range(PEER_TOPK):
            for k in range(LISTS_PER_TRIP):
                cs = css[k]
                m = jnp.max(cs, axis=0, keepdims=True)
                pick = jnp.min(jnp.where(cs == m, row_c, float(n_cand)), axis=0, keepdims=True)
                sel = row_c == pick
                experts[k].append(jnp.max(jnp.where(sel, ces[k], -1.0), axis=0, keepdims=True))
                best[k].append(m)
                css[k] = jnp.where(sel, NEG_INF, cs)
        for k, h in enumerate(heads):
            bs = jnp.concatenate(best[k], axis=0)
            ex = jnp.exp(bs - bs[0:1, :])
            off = pl.multiple_of(h * PEER_TOPK, PEER_TOPK)
            bg_ref[pl.ds(off, PEER_TOPK), :] = ex / jnp.sum(ex, axis=0, keepdims=True)
            be_ref[pl.ds(off, PEER_TOPK), :] = jnp.concatenate(experts[k], axis=0)
        return carry

    lax.fori_loop(0, PEER_HEADS // LISTS_PER_TRIP, stage2, 0)
    idx_ref[...] = (be_ref[...].T * float(ROW_TILE)).astype(I32)
    gate_ref[...] = bg_ref[...].T


def _topk(h2, wq_t16, subkeys16, tb):
    n = h2.shape[0]
    const = lambda a: pl.BlockSpec(a.shape, lambda i: (0,) * a.ndim)
    return pl.pallas_call(
        functools.partial(_topk_kernel, tb=tb),
        grid=(n // tb,),
        in_specs=[pl.BlockSpec((tb, D_MODEL), lambda i: (i, 0)), const(wq_t16), const(subkeys16)],
        out_specs=[pl.BlockSpec((tb, N_SLOTS), lambda i: (i, 0))] * 2,
        out_shape=[jax.ShapeDtypeStruct((n, N_SLOTS), I32), jax.ShapeDtypeStruct((n, N_SLOTS), F32)],
        scratch_shapes=[pltpu.VMEM((2 * PEER_HEADS, N_KEYS, tb), F32),
                        pltpu.VMEM((2 * PEER_HEADS, PEER_TOPK, tb), F32),
                        pltpu.VMEM((2 * PEER_HEADS, PEER_TOPK, tb), F32),
                        pltpu.VMEM((N_SLOTS, tb), F32), pltpu.VMEM((N_SLOTS, tb), F32)],
        compiler_params=_cparams(("parallel",)),
        name="peer_topk",
    )(h2, wq_t16, subkeys16)


def _pack_kernel(t_ref, o_ref):
    x = t_ref[...]
    half = x.shape[1] // 2
    rows = x.shape[0]
    bits = lax.bitcast_convert_type(x.astype(BF16).astype(F32), I32)
    word = lax.shift_right_logical(bits[:, :half], 16) | bits[:, half:]
    for c in range(ROW_TILE):
        o_ref[pl.ds(c, rows, stride=ROW_TILE), :] = word[:, c * LANES:(c + 1) * LANES]


def _pack_table(t, rows=512):
    n_e, d = t.shape
    return pl.pallas_call(
        _pack_kernel,
        grid=(n_e // rows,),
        in_specs=[pl.BlockSpec((rows, d), lambda i: (i, 0))],
        out_specs=pl.BlockSpec((rows * ROW_TILE, LANES), lambda i: (i, 0)),
        out_shape=jax.ShapeDtypeStruct((n_e * ROW_TILE, LANES), I32),
        compiler_params=_cparams(("parallel",)),
        name="pack_table",
    )(t)


def _peer_u_kernel(idx_ref, h_ref, gate_ref, tbl_ref, mask_ref, sel_ref, w_ref, pair_ref, *stage_refs, tb):
    buf_a, buf_b, buf_c, buf_d = stage_refs

    def gather(t, stage):
        for e in range(N_SLOTS):
            row = pl.multiple_of(idx_ref[t, e], ROW_TILE)
            stage[e * ROW_TILE:(e + 1) * ROW_TILE, :] = tbl_ref[pl.ds(row, ROW_TILE), :]

    def contract(t, stage):
        h3 = jnp.concatenate(_split3(h_ref[t]), axis=0)
        acc = jnp.zeros((3 * SUBLANES, 2 * N_SLOTS), F32)
        for c in range(ROW_TILE):
            word = stage[pl.ds(c, N_SLOTS, stride=ROW_TILE), :]
            rhs = pltpu.bitcast(word, BF16)
            acc = acc + mask_ref[c] * lax.dot_general(h3, rhs, NT_DIMS, preferred_element_type=F32)
        pair_ref[pl.ds(t, 1), :] = jnp.sum(acc, axis=0, keepdims=True)

    for buf in (buf_c, buf_d):
        buf[...] = jnp.zeros(buf.shape, I32)

    def pair(t, fill, drain):
        gather(t, fill[0])
        gather(t + 1, fill[1])
        contract(jnp.maximum(t - 2, 0), drain[0])
        contract(jnp.maximum(t - 1, 0), drain[1])

    def tokens(i, carry):
        for k in range(PAIRS_PER_TRIP):
            t = 2 * (PAIRS_PER_TRIP * i + k)
            if k % 2 == 0:
                pair(t, (buf_a, buf_b), (buf_c, buf_d))
            else:
                pair(t, (buf_c, buf_d), (buf_a, buf_b))
        return carry

    lax.fori_loop(0, tb // (2 * PAIRS_PER_TRIP), tokens, 0)
    contract(tb - 2, buf_c)
    contract(tb - 1, buf_d)
    a = sum(jnp.dot(x, sel_ref[...], preferred_element_type=F32) for x in _split3(pair_ref[...]))
    w_ref[...] = gate_ref[...] * (0.5 * a * (1.0 + lax.erf(a * (2.0 ** -0.5))))


def _chunk_masks():
    m = np.zeros((ROW_TILE, 3 * SUBLANES, 2 * N_SLOTS), np.float32)
    for c in range(ROW_TILE):
        for term in range(3):
            m[c, term * SUBLANES + c, 0::2] = 1.0
            m[c, term * SUBLANES + ROW_TILE + c, 1::2] = 1.0
    sel = np.zeros((2 * N_SLOTS, N_SLOTS), np.float32)
    sel[np.arange(2 * N_SLOTS), np.arange(2 * N_SLOTS) // 2] = 1.0
    return jnp.asarray(m), jnp.asarray(sel, BF16)


def _peer_u(idx, h2r, gate, tbl, tb):
    n = idx.shape[0]
    masks, sel = _chunk_masks()
    return pl.pallas_call(
        functools.partial(_peer_u_kernel, tb=tb),
        grid=(n // tb,),
        in_specs=[pl.BlockSpec((tb, N_SLOTS), lambda i: (i, 0), memory_space=pltpu.SMEM),
                  pl.BlockSpec((tb, SUBLANES, LANES), lambda i: (i, 0, 0)),
                  pl.BlockSpec((tb, N_SLOTS), lambda i: (i, 0)),
                  pl.BlockSpec(memory_space=pltpu.VMEM),
                  pl.BlockSpec(masks.shape, lambda i: (0, 0, 0)),
                  pl.BlockSpec(sel.shape, lambda i: (0, 0))],
        out_specs=pl.BlockSpec((tb, N_SLOTS), lambda i: (i, 0)),
        out_shape=jax.ShapeDtypeStruct((n, N_SLOTS), F32),
        scratch_shapes=[pltpu.VMEM((tb, 2 * N_SLOTS), F32)]
        + [pltpu.VMEM((N_SLOTS * ROW_TILE, LANES), I32)] * 4,
        compiler_params=_cparams(("arbitrary",)),
        name="peer_u",
    )(idx, h2r, gate, tbl, masks, sel)


def _peer_v_kernel(idx_ref, w_ref, tbl_ref, elo_ref, ehi_ref, xp_ref, g2_ref, fg_ref, y_ref,
                   *scratch_refs, tb):
    o_ref = scratch_refs[0]
    wexp_refs = scratch_refs[1:3]
    buf_a, buf_b, buf_c, buf_d = scratch_refs[3:7]
    n_terms = 3

    for half_ref in wexp_refs:
        half_ref[...] = jnp.zeros(half_ref.shape, F32)
    for k, term in enumerate(_split3(w_ref[...])):
        for j, spread_ref in enumerate((elo_ref, ehi_ref)):
            spread = jnp.dot(term, spread_ref[...], preferred_element_type=F32)
            for half, half_ref in enumerate(wexp_refs):
                half_ref[pl.ds(j * n_terms + k, tb, stride=SUBLANES), :] = (
                    spread[:, half * LANES:(half + 1) * LANES])

    def gather(t, stage):
        for e in range(N_SLOTS):
            row = pl.multiple_of(idx_ref[t, e], ROW_TILE)
            stage[e * ROW_TILE:(e + 1) * ROW_TILE, :] = tbl_ref[pl.ds(row, ROW_TILE), :]

    def contract(t, stage):
        rows = pl.ds(pl.multiple_of(t * SUBLANES, SUBLANES), SUBLANES)
        lhs = jnp.concatenate([half_ref[rows, :] for half_ref in wexp_refs], axis=1).astype(BF16)
        lo, hi = [], []
        for c in range(ROW_TILE):
            rhs = pltpu.bitcast(stage[pl.ds(c, N_SLOTS, stride=ROW_TILE), :], BF16)
            r = jnp.dot(lhs, rhs, preferred_element_type=F32)
            lo.append(r[0:1] + r[1:2] + r[2:3])
            hi.append(r[3:4] + r[4:5] + r[5:6])
        o_ref[pl.ds(t, 1), :] = jnp.concatenate(lo + hi, axis=1)

    for buf in (buf_c, buf_d):
        buf[...] = jnp.zeros(buf.shape, I32)

    def pair(t, fill, drain):
        gather(t, fill[0])
        gather(t + 1, fill[1])
        contract(jnp.maximum(t - 2, 0), drain[0])
        contract(jnp.maximum(t - 1, 0), drain[1])

    def tokens(i, carry):
        for k in range(PAIRS_PER_TRIP):
            t = 2 * (PAIRS_PER_TRIP * i + k)
            if k % 2 == 0:
                pair(t, (buf_a, buf_b), (buf_c, buf_d))
            else:
                pair(t, (buf_c, buf_d), (buf_a, buf_b))
        return carry

    lax.fori_loop(0, tb // (2 * PAIRS_PER_TRIP), tokens, 0)
    contract(tb - 2, buf_c)
    contract(tb - 1, buf_d)
    x = xp_ref[...] + g2_ref[...] * o_ref[...]
    ms = jnp.mean(x * x, axis=-1, keepdims=True)
    y_ref[...] = x * lax.rsqrt(ms + EPS) * fg_ref[...]


def _half_spreads():
    lo = np.zeros((N_SLOTS, 2 * N_SLOTS), np.float32)
    hi = np.zeros((N_SLOTS, 2 * N_SLOTS), np.float32)
    lo[np.arange(N_SLOTS), 2 * np.arange(N_SLOTS)] = 1.0
    hi[np.arange(N_SLOTS), 2 * np.arange(N_SLOTS) + 1] = 1.0
    return jnp.asarray(lo, BF16), jnp.asarray(hi, BF16)


def _peer_v(idx, w, tbl, xp, g2, fg, per_row, rows_per_mod, tb):
    n = idx.shape[0]
    elo, ehi = _half_spreads()
    blk = pl.BlockSpec((tb, N_SLOTS), lambda i: (i, 0))
    row = pl.BlockSpec((tb, D_MODEL), lambda i: (i, 0))
    return pl.pallas_call(
        functools.partial(_peer_v_kernel, tb=tb),
        grid=(n // tb,),
        in_specs=[pl.BlockSpec((tb, N_SLOTS), lambda i: (i, 0), memory_space=pltpu.SMEM), blk,
                  pl.BlockSpec(memory_space=pltpu.VMEM),
                  pl.BlockSpec(elo.shape, lambda i: (0, 0)), pl.BlockSpec(ehi.shape, lambda i: (0, 0)),
                  row, _mod_spec(per_row, tb, rows_per_mod), pl.BlockSpec((1, D_MODEL), lambda i: (0, 0))],
        out_specs=row,
        out_shape=jax.ShapeDtypeStruct((n, D_MODEL), F32),
        scratch_shapes=[pltpu.VMEM((tb, D_MODEL), F32)]
        + [pltpu.VMEM((tb * SUBLANES, LANES), F32)] * 2
        + [pltpu.VMEM((N_SLOTS * ROW_TILE, LANES), I32)] * 4,
        compiler_params=_cparams(("arbitrary",)),
        name="peer_v",
    )(idx, w, tbl, elo, ehi, xp, g2, fg)


def _peer(h2, h2_tiles, xp, g2, fg, wq_t16, subkeys16, u_tbl, v_tbl, per_row, rows_per_mod, tb_topk,
          tb_gather):
    n = h2.shape[0]
    idx, gate = _topk(h2, wq_t16, subkeys16, tb_topk)
    w = _peer_u(idx, h2_tiles.reshape(n, SUBLANES, LANES), gate, u_tbl, tb_gather)
    return _peer_v(idx, w, v_tbl, xp, g2, fg, per_row, rows_per_mod, tb_gather)


def kernel(x_prompt, x_sample, c_prompt, c_sample, cache_a_k, cache_a_v, cache_b_k, cache_b_v, cache_b_logf, page_table, w_ada, b_ada, norm1_g, w_in, b_f, lambda_q1, lambda_k1, lambda_q2, lambda_k2, subln_g, w_out, norm2_g, peer_wq, peer_subkeys, peer_u, peer_v, final_g):
    batch, seq_len, d = x_prompt.shape
    dec_batch, new_tokens, _ = x_sample.shape
    depth = w_ada.shape[0]
    assert depth == 1 and d == D_MODEL and new_tokens == 4
    n_p, n_s = batch * seq_len, dec_batch * new_tokens
    n_pool, page = cache_a_k.shape[1], cache_a_k.shape[2]
    l = 0
    lam_init = 0.8 - 0.6 * math.exp(-0.3 * l)

    n_c = batch + dec_batch
    pad = (-n_c) % SUBLANES
    c_all = jnp.concatenate([c_prompt, c_sample, jnp.zeros((pad, d), F32)], axis=0)
    ada = _ada(c_all, w_ada[l], b_ada[l])
    mods_p = [m.reshape(batch, 1, d) for m in jnp.split(ada[:batch], 6, axis=-1)]
    mods_s = [jnp.repeat(m, new_tokens, axis=0) for m in jnp.split(ada[batch:n_c], 6, axis=-1)]
    lam = _lam(lambda_q1[l], lambda_k1[l], lambda_q2[l], lambda_k2[l], lam_init)

    n_main = 3 * WA + 3 * WB
    w_main = w_in[l][:, :n_main].astype(BF16)
    w_f = jnp.pad(w_in[l][:, n_main:], ((0, 0), (0, LANES - HB))).astype(BF16)
    bf = jnp.pad(b_f[l], (0, LANES - HB)).reshape(1, LANES)
    g1n = norm1_g[l].reshape(1, d)
    g2n = norm2_g[l].reshape(1, d)
    sub_g = subln_g[l].reshape(1, DVA)
    w_out16 = w_out[l].astype(BF16)
    wq_t16 = peer_wq[l].T.astype(BF16)
    subkeys16 = peer_subkeys[l].astype(BF16)
    u_tbl = _pack_table(peer_u[l])
    v_tbl = _pack_table(peer_v[l])
    fg = final_g.reshape(1, d)

    tm_p = 256
    xp2d = x_prompt.reshape(n_p, d)
    sh1, sc1, g1, sh2, sc2, g2 = mods_p
    (ka, va, kb, vb, lf, qa16, ka16, va16, qb16, kb16, vb16, qbias, kbias) = _proj(
        xp2d, sh1, sc1, g1n, w_main, w_f, bf, False, seq_len, seq_len, tm_p)
    qbias_a, kbias_a = _alibi_bias(seq_len)
    tq = 1024
    oa = _attention(qa16, qbias_a, ka16, kbias_a, va16, lam, batch, seq_len, tq, fox=False)
    ob = _attention(qb16, qbias, kb16, kbias, vb16, lam, batch, seq_len, tq, fox=True)
    xp_mid, h2, h2_t = _merge(oa, ob, xp2d, g1, sh2, sc2, sub_g, w_out16, g2n, False, seq_len, tm_p,
                        1.0 - lam_init)
    y_prompt = _peer(h2, h2_t, xp_mid, g2, fg, wq_t16, subkeys16, u_tbl, v_tbl, False, seq_len,
                     256, 128).reshape(batch, seq_len, d)

    tm_s = n_s
    xs2d = x_sample.reshape(n_s, d)
    sh1, sc1, g1, sh2, sc2, g2 = mods_s
    (ka_s, va_s, kb_s, vb_s, lf_s, qa16_s, _, _, qb16_s, _, _, _, _) = _proj(
        xs2d, sh1, sc1, g1n, w_main, w_f, bf, True, n_s, n_s, tm_s)
    assert page == LANES
    qa5 = qa16_s.reshape(dec_batch, new_tokens, HA, 2, DA)
    zeros = jnp.zeros_like(qa5[:, :, :, 0, :])
    wqa = jnp.stack([jnp.concatenate([qa5[:, :, :, 0, :], zeros], axis=-1),
                     jnp.concatenate([zeros, qa5[:, :, :, 1, :]], axis=-1)], axis=3)
    wqa = jnp.pad(wqa, ((0, 0), (0, QPAD - new_tokens), (0, 0), (0, 0), (0, 0)))
    wqa = wqa.transpose(0, 2, 3, 1, 4).reshape(dec_batch, HA * 2 * QPAD, 2 * DA)
    qb4 = qb16_s.reshape(dec_batch, new_tokens, HB, DB)
    wqb = (qb4[:, :, :, None, :] * jnp.eye(HB, dtype=BF16)[None, None, :, :, None])
    wqb = wqb.reshape(dec_batch, new_tokens * HB, WB)

    def pos_major(c):
        return c.reshape(n_pool, page * HA, c.shape[-1])

    def pos_minor(c):
        return c.transpose(0, 2, 3, 1).reshape(n_pool, HB * DB, page)

    caches = [pos_major(cache_a_k[l]), pos_major(cache_a_v[l]), pos_minor(cache_b_k[l]),
              pos_minor(cache_b_v[l]), cache_b_logf[l].transpose(0, 2, 1)]

    def new_pos_major(a):
        a = a.reshape(dec_batch, new_tokens * HA, DVA)
        return jnp.pad(a, ((0, 0), (0, (page - new_tokens) * HA), (0, 0)))

    def new_pos_minor(a, heads):
        w = a.shape[1] // heads
        a = a.reshape(dec_batch, new_tokens, heads, w).transpose(0, 2, 3, 1)
        return jnp.pad(a, ((0, 0), (0, 0), (0, 0), (0, page - new_tokens))).reshape(dec_batch, heads * w, page)

    new_pages = [new_pos_major(ka_s), new_pos_major(va_s), new_pos_minor(kb_s, HB),
                 new_pos_minor(vb_s, HB), new_pos_minor(lf_s, HB)]
    oa_s, ob_s = _decode_attention(page_table, caches, new_pages, wqa, wqb, lam, new_tokens, 16)
    oa_s = oa_s.reshape(dec_batch, HA, QPAD, DVA)[:, :, :new_tokens].transpose(0, 2, 1, 3).reshape(n_s, WA)
    ob_s = ob_s.reshape(dec_batch, new_tokens, HB, 2, DB)
    ob_s = jnp.where((jnp.arange(HB) % 2 == 0)[None, None, :, None], ob_s[:, :, :, 0, :], ob_s[:, :, :, 1, :])
    ob_s = ob_s.reshape(n_s, WB)
    xs_mid, h2_s, h2_st = _merge(oa_s, ob_s, xs2d, g1, sh2, sc2, sub_g, w_out16, g2n, True, n_s, tm_s,
                          1.0 - lam_init)
    y_sample = _peer(h2_s, h2_st, xs_mid, g2, fg, wq_t16, subkeys16, u_tbl, v_tbl, True, n_s,
                     n_s, 128).reshape(dec_batch, new_tokens, d)

    def kv(a, b, t, h, w):
        return a.reshape(1, b, t, h, w)

    return (y_prompt, y_sample,
            kv(ka, batch, seq_len, HA, 2 * DA), kv(va, batch, seq_len, HA, DVA),
            kv(kb, batch, seq_len, HB, DB), kv(vb, batch, seq_len, HB, DB),
            lf.reshape(1, batch, seq_len, HB),
            kv(ka_s, dec_batch, new_tokens, HA, 2 * DA), kv(va_s, dec_batch, new_tokens, HA, DVA),
            kv(kb_s, dec_batch, new_tokens, HB, DB), kv(vb_s, dec_batch, new_tokens, HB, DB),
            lf_s.reshape(1, dec_batch, new_tokens, HB))
```
